```python
import math
import numpy as np
import jax
import jax.numpy as jnp
from jax import lax

D_MODEL = 1024
BATCH = 8
SEQ = 4096
DEPTH = 1

HEAD_DIM = 64
ROPE_THETA = 10000.0
NORM_EPS = 1e-6
NEG_INF = -1e30

DIFF_HEADS = 4
DIFF_V_DIM = 2 * HEAD_DIM
DIFF_WIDTH = DIFF_HEADS * DIFF_V_DIM
Q_BLOCK = 128

NSA_HEADS = 8
NSA_KV_GROUPS = 2
NSA_HEADS_PER_GROUP = NSA_HEADS // NSA_KV_GROUPS
NSA_WIDTH = NSA_HEADS * HEAD_DIM
KV_COLS = NSA_KV_GROUPS * HEAD_DIM
CMP_BLOCK = 32
CMP_STRIDE = 16
CMP_HIDDEN = 4 * HEAD_DIM
SLC_BLOCK = 64
SLC_TOPK = 16
WINDOW = 512
NSA_Q_BLOCK = 64
N_BRANCHES = 3

MIX_WIDTH = DIFF_WIDTH + NSA_WIDTH
IN_COLS = 2 * DIFF_HEADS * 2 * HEAD_DIM + DIFF_WIDTH + NSA_WIDTH + 6 * KV_COLS + NSA_HEADS * N_BRANCHES
D_FF = ((8 * D_MODEL + 3 * 256 - 1) // (3 * 256)) * 256

kernel_name = 'hybrid_diffattn_nsa_block'


def rms_norm(x, g):
    xf = x.astype(jnp.float32)
    y = xf * lax.rsqrt(jnp.mean(xf * xf, axis=-1, keepdims=True) + NORM_EPS)
    return (y * g.astype(jnp.float32)).astype(x.dtype)


def rope_tables(S):
    inv = 1.0 / (ROPE_THETA ** (jnp.arange(0, HEAD_DIM, 2, dtype=jnp.float32) / HEAD_DIM))
    ang = jnp.arange(S, dtype=jnp.float32)[:, None] * inv[None, :]
    return jnp.cos(ang), jnp.sin(ang)


def apply_rope(x, cos, sin):
    x1, x2 = jnp.split(x, 2, axis=-1)
    c = cos.astype(x.dtype)
    s = sin.astype(x.dtype)
    return jnp.concatenate([x1 * c - x2 * s, x1 * s + x2 * c], axis=-1)


def masked_softmax(s, mask):
    p = jax.nn.softmax(jnp.where(mask, s, NEG_INF), axis=-1)
    return jnp.where(mask, p, 0.0)


def diff_attention(q, k, v, lam):
    B, H, _, S, d = q.shape
    scale = d ** -0.5
    kpos = jnp.arange(S)

    def block(i):
        s0 = i * Q_BLOCK
        qb = lax.dynamic_slice_in_dim(q, s0, Q_BLOCK, axis=3)
        s = jnp.einsum('bhmqd,bhmkd->bhmqk', qb, k).astype(jnp.float32) * scale
        qpos = s0 + jnp.arange(Q_BLOCK)
        mask = kpos[None, :] <= qpos[:, None]
        p = jax.nn.softmax(jnp.where(mask, s, NEG_INF), axis=-1)
        a = p[:, :, 0] - lam * p[:, :, 1]
        return jnp.einsum('bhqk,bhke->bhqe', a.astype(v.dtype), v)

    o = lax.map(block, jnp.arange(S // Q_BLOCK))
    return o.transpose(1, 0, 3, 2, 4).reshape(B, S, H, v.shape[-1])


def compress_tokens(t, tok_idx, pos, w1, w2):
    blocks = t[:, :, tok_idx] + pos.astype(t.dtype)
    flat = blocks.reshape(blocks.shape[0], blocks.shape[1], blocks.shape[2], -1)
    return jax.nn.silu(flat @ w1) @ w2


def selection_overlap(n_cmp, n_sel):
    c0 = np.arange(n_cmp)[:, None] * CMP_STRIDE
    s0 = np.arange(n_sel)[None, :] * SLC_BLOCK
    ov = np.clip(np.minimum(c0 + CMP_BLOCK, s0 + SLC_BLOCK) - np.maximum(c0, s0), 0, None)
    return (ov / CMP_BLOCK).astype(np.float32)


def nsa_attention(q, kc_tok, vc_tok, ks, vs, kw, vw, gates,
                  k_pos, k_w1, k_w2, v_pos, v_w1, v_w2):
    B, G, Hg, S, d = q.shape
    dt = q.dtype
    scale = d ** -0.5
    QB = NSA_Q_BLOCK
    n_cmp = (S - CMP_BLOCK) // CMP_STRIDE + 1
    n_sel = S // SLC_BLOCK
    top_n = min(SLC_TOPK, n_sel)

    tok_idx = np.arange(n_cmp)[:, None] * CMP_STRIDE + np.arange(CMP_BLOCK)[None, :]
    kc = compress_tokens(kc_tok, tok_idx, k_pos, k_w1, k_w2)
    vc = compress_tokens(vc_tok, tok_idx, v_pos, v_w1, v_w2)
    cmp_end = jnp.asarray(tok_idx[:, -1])
    overlap = jnp.asarray(selection_overlap(n_cmp, n_sel))

    ks_blk = ks.reshape(B, G, n_sel, SLC_BLOCK, d)
    vs_blk = vs.reshape(B, G, n_sel, SLC_BLOCK, d)
    kw_pad = jnp.pad(kw, ((0, 0), (0, 0), (WINDOW, 0), (0, 0)))
    vw_pad = jnp.pad(vw, ((0, 0), (0, 0), (WINDOW, 0), (0, 0)))
    gather_blocks = jax.vmap(jax.vmap(lambda t, i: t[i]))
    blk = jnp.arange(n_sel)

    def block(i):
        s0 = i * QB
        qb = lax.dynamic_slice_in_dim(q, s0, QB, axis=3)
        gb = lax.dynamic_slice_in_dim(gates, s0, QB, axis=3)
        qpos = s0 + jnp.arange(QB)

        sc = jnp.einsum('bghqd,bgnd->bghqn', qb, kc).astype(jnp.float32) * scale
        pc = masked_softmax(sc, cmp_end[None, :] <= qpos[:, None])
        o_cmp = jnp.einsum('bghqn,bgnd->bghqd', pc.astype(dt), vc)

        imp = jnp.einsum('bghqn,nj->bgqj', pc, overlap)
        cur = qpos // SLC_BLOCK
        valid = blk[None, :] * SLC_BLOCK <= qpos[:, None]
        forced = (blk[None, :] == 0) | (blk[None, :] == cur[:, None]) | (blk[None, :] == cur[:, None] - 1)
        imp = jnp.where(forced, jnp.inf, jnp.where(valid, imp, -jnp.inf))
        _, sel = lax.top_k(imp, top_n)
        kg = gather_blocks(ks_blk, sel)
        vg = gather_blocks(vs_blk, sel)
        tpos = sel[..., None] * SLC_BLOCK + jnp.arange(SLC_BLOCK)
        smask = (tpos <= qpos[:, None, None]).reshape(B, G, 1, QB, top_n * SLC_BLOCK)
        ss = jnp.einsum('bghqd,bgqnld->bghqnl', qb, kg).astype(jnp.float32) * scale
        ps = masked_softmax(ss.reshape(B, G, Hg, QB, top_n * SLC_BLOCK), smask)
        o_slc = jnp.einsum('bghqm,bgqmd->bghqd', ps.astype(dt),
                           vg.reshape(B, G, QB, top_n * SLC_BLOCK, d))

        kwb = lax.dynamic_slice_in_dim(kw_pad, s0, WINDOW + QB, axis=2)
        vwb = lax.dynamic_slice_in_dim(vw_pad, s0, WINDOW + QB, axis=2)
        wpos = s0 - WINDOW + jnp.arange(WINDOW + QB)
        wmask = (wpos[None, :] <= qpos[:, None]) & (wpos[None, :] > qpos[:, None] - WINDOW) & (wpos[None, :] >= 0)
        sw = jnp.einsum('bghqd,bgkd->bghqk', qb, kwb).astype(jnp.float32) * scale
        pw = masked_softmax(sw, wmask)
        o_win = jnp.einsum('bghqk,bgkd->bghqd', pw.astype(dt), vwb)

        return gb[..., 0:1] * o_cmp + gb[..., 1:2] * o_slc + gb[..., 2:3] * o_win

    o = lax.map(block, jnp.arange(S // QB))
    return o.transpose(1, 0, 4, 2, 3, 5).reshape(B, S, G * Hg * d)


def hybrid_mixer(h, layer, w_in, lambda_q1, lambda_k1, lambda_q2, lambda_k2, diff_subln,
                 k_cmp_pos, k_cmp_w1, k_cmp_w2, v_cmp_pos, v_cmp_w1, v_cmp_w2, w_out):
    B, S, _ = h.shape
    cos, sin = rope_tables(S)
    proj = h @ w_in
    sizes = [DIFF_HEADS * 2 * HEAD_DIM, DIFF_HEADS * 2 * HEAD_DIM, DIFF_WIDTH, NSA_WIDTH,
             KV_COLS, KV_COLS, KV_COLS, KV_COLS, KV_COLS, KV_COLS, NSA_HEADS * N_BRANCHES]
    dq, dk, dv, nq, kc, vc, ks, vs, kw, vw, gt = jnp.split(
        proj, np.cumsum(sizes)[:-1].tolist(), axis=-1)

    dq = apply_rope(dq.reshape(B, S, DIFF_HEADS, 2, HEAD_DIM).transpose(0, 2, 3, 1, 4), cos, sin)
    dk = apply_rope(dk.reshape(B, S, DIFF_HEADS, 2, HEAD_DIM).transpose(0, 2, 3, 1, 4), cos, sin)
    dv = dv.reshape(B, S, DIFF_HEADS, DIFF_V_DIM).transpose(0, 2, 1, 3)
    lam_init = 0.8 - 0.6 * math.exp(-0.3 * layer)
    f32 = jnp.float32
    lam = (jnp.exp(jnp.sum(lambda_q1.astype(f32) * lambda_k1.astype(f32)))
           - jnp.exp(jnp.sum(lambda_q2.astype(f32) * lambda_k2.astype(f32))) + lam_init)
    o_diff = diff_attention(dq, dk, dv, lam)
    o_diff = (rms_norm(o_diff, diff_subln) * (1.0 - lam_init)).reshape(B, S, DIFF_WIDTH)

    def q_heads(t):
        return t.reshape(B, S, NSA_KV_GROUPS, NSA_HEADS_PER_GROUP, HEAD_DIM).transpose(0, 2, 3, 1, 4)

    def kv_heads(t):
        return t.reshape(B, S, NSA_KV_GROUPS, HEAD_DIM).transpose(0, 2, 1, 3)

    nq = apply_rope(q_heads(nq), cos, sin)
    kc = apply_rope(kv_heads(kc), cos, sin)
    ks = apply_rope(kv_heads(ks), cos, sin)
    kw = apply_rope(kv_heads(kw), cos, sin)
    gates = jax.nn.sigmoid(gt.reshape(B, S, NSA_KV_GROUPS, NSA_HEADS_PER_GROUP, N_BRANCHES)
                           .transpose(0, 2, 3, 1, 4))
    o_nsa = nsa_attention(nq, kc, kv_heads(vc), ks, kv_heads(vs), kw, kv_heads(vw), gates,
                          k_cmp_pos, k_cmp_w1, k_cmp_w2, v_cmp_pos, v_cmp_w1, v_cmp_w2)

    return jnp.concatenate([o_diff, o_nsa], axis=-1) @ w_out


def setup_inputs(seed: int = 0) -> dict:
    key = jax.random.key(seed)
    k = jax.random.split(key, 24)
    L = DEPTH

    def nrm(kk, shape, scale):
        return jax.random.normal(kk, shape, jnp.float32) * scale

    def gain(kk, n):
        return 1.0 + 0.05 * jax.random.normal(kk, (L, n), jnp.float32)

    cmp_in = CMP_BLOCK * HEAD_DIM
    return {
        'x': nrm(k[0], (BATCH, SEQ, D_MODEL), 1.0),
        'attn_pre_norm': gain(k[1], D_MODEL),
        'w_in': nrm(k[2], (L, D_MODEL, IN_COLS), D_MODEL ** -0.5),
        'lambda_q1': nrm(k[3], (L, HEAD_DIM), 0.1),
        'lambda_k1': nrm(k[4], (L, HEAD_DIM), 0.1),
        'lambda_q2': nrm(k[5], (L, HEAD_DIM), 0.1),
        'lambda_k2': nrm(k[6], (L, HEAD_DIM), 0.1),
        'diff_subln': gain(k[7], DIFF_V_DIM),
        'k_cmp_pos': nrm(k[8], (L, CMP_BLOCK, HEAD_DIM), 0.1),
        'k_cmp_w1': nrm(k[9], (L, cmp_in, CMP_HIDDEN), cmp_in ** -0.5),
        'k_cmp_w2': nrm(k[10], (L, CMP_HIDDEN, HEAD_DIM), CMP_HIDDEN ** -0.5),
        'v_cmp_pos': nrm(k[11], (L, CMP_BLOCK, HEAD_DIM), 0.1),
        'v_cmp_w1': nrm(k[12], (L, cmp_in, CMP_HIDDEN), cmp_in ** -0.5),
        'v_cmp_w2': nrm(k[13], (L, CMP_HIDDEN, HEAD_DIM), CMP_HIDDEN ** -0.5),
        'w_out': nrm(k[14], (L, MIX_WIDTH, D_MODEL), MIX_WIDTH ** -0.5),
        'attn_post_norm': gain(k[15], D_MODEL),
        'ffn_pre_norm': gain(k[16], D_MODEL),
        'w_gate': nrm(k[17], (L, D_MODEL, D_FF), D_MODEL ** -0.5),
        'w_up': nrm(k[18], (L, D_MODEL, D_FF), D_MODEL ** -0.5),
        'w_down': nrm(k[19], (L, D_FF, D_MODEL), D_FF ** -0.5),
        'ffn_post_norm': gain(k[20], D_MODEL),
    }


def reference(x, attn_pre_norm, w_in, lambda_q1, lambda_k1, lambda_q2, lambda_k2, diff_subln,
              k_cmp_pos, k_cmp_w1, k_cmp_w2, v_cmp_pos, v_cmp_w1, v_cmp_w2, w_out,
              attn_post_norm, ffn_pre_norm, w_gate, w_up, w_down, ffn_post_norm):
    for l in range(DEPTH):
        h = rms_norm(x, attn_pre_norm[l])
        mix = hybrid_mixer(h, l, w_in[l], lambda_q1[l], lambda_k1[l], lambda_q2[l], lambda_k2[l],
                           diff_subln[l], k_cmp_pos[l], k_cmp_w1[l], k_cmp_w2[l],
                           v_cmp_pos[l], v_cmp_w1[l], v_cmp_w2[l], w_out[l])
        x = x + rms_norm(mix, attn_post_norm[l])
        h = rms_norm(x, ffn_pre_norm[l])
        f = (jax.nn.silu(h @ w_gate[l]) * (h @ w_up[l])) @ w_down[l]
        x = x + rms_norm(f, ffn_post_norm[l])
    return x
```

```python
import functools
import math

import numpy as np
import jax
import jax.numpy as jnp
from jax import lax
from jax.experimental import pallas as pl
from jax.experimental.pallas import tpu as pltpu

F32 = jnp.float32
MXU_DTYPE = jnp.bfloat16

LANES = 128
HEAD_DIM = 64
ROPE_THETA = 10000.0
NORM_EPS = 1e-6
NEG_INF = -1e30
FORCED_IMPORTANCE = 3e38

DIFF_HEADS = 4
NSA_HEADS_PER_GROUP = 4
NSA_KV_GROUPS = 2
CMP_BLOCK = 32
CMP_STRIDE = 16
CMP_HIDDEN = 4 * HEAD_DIM
SLC_BLOCK = 64
SLC_TOPK = 16
WINDOW = 512
N_BRANCHES = 3

N_MAIN_SLABS = 20
N_AUX_SLABS = 3
N_P_SLABS = 21
VMEM_LIMIT = 56 * 1024 * 1024


def _nt_dot(a, b):
    return lax.dot_general(a, b, (((1,), (1,)), ((), ())), preferred_element_type=F32)


def _dot(a, b):
    return jnp.dot(a, b, preferred_element_type=F32)


def _rms(x, g):
    return x * lax.rsqrt(jnp.mean(x * x, axis=-1, keepdims=True) + NORM_EPS) * g


def _inproj_kernel(x_ref, g_ref, w_ref, cos_ref, sin_ref, p_ref, aux_ref, *, tm, seq_tiles):
    h = _rms(x_ref[...], g_ref[...]).astype(MXU_DTYPE)
    cos = cos_ref[...]
    sin = sin_ref[...]
    lane = lax.broadcasted_iota(jnp.int32, (tm, LANES), 1)
    low_half = (lane & (HEAD_DIM - 1)) < HEAD_DIM // 2

    def rope(y):
        fwd = pltpu.roll(y, HEAD_DIM // 2, 1)
        bwd = pltpu.roll(y, LANES - HEAD_DIM // 2, 1)
        return y * cos + jnp.where(low_half, bwd, fwd) * sin

    pos = (pl.program_id(0) % seq_tiles) * tm + lax.broadcasted_iota(jnp.int32, (tm, LANES), 0)
    blk = pos // SLC_BLOCK
    scale = HEAD_DIM ** -0.5

    n_slabs = N_MAIN_SLABS + N_AUX_SLABS
    for c0 in range(0, n_slabs, 2):
        c1 = min(c0 + 2, n_slabs)
        y2 = _dot(h, w_ref[:, c0 * LANES:c1 * LANES])
        for s in range(c0, c1):
            y = y2[:, (s - c0) * LANES:(s - c0 + 1) * LANES]
            if s < 4 or 12 <= s < 16:
                p_ref[:, s * LANES:(s + 1) * LANES] = (rope(y) * scale).astype(p_ref.dtype)
            elif s < 8:
                p_ref[:, s * LANES:(s + 1) * LANES] = rope(y).astype(p_ref.dtype)
            elif s < 12:
                p_ref[:, s * LANES:(s + 1) * LANES] = y.astype(p_ref.dtype)
            elif s == 16:
                r = rope(y)
                ind_hi = jnp.where(lane - HEAD_DIM == blk, 1.0, 0.0)
                ind_lo = jnp.where(lane == blk, 1.0, 0.0)
                p_ref[:, 16 * LANES:17 * LANES] = jnp.where(lane < HEAD_DIM, r, ind_hi).astype(p_ref.dtype)
                p_ref[:, 17 * LANES:18 * LANES] = jnp.where(lane >= HEAD_DIM, r, ind_lo).astype(p_ref.dtype)
            elif s == 17:
                p_ref[:, 18 * LANES:19 * LANES] = y.astype(p_ref.dtype)
            elif s == 18:
                p_ref[:, 19 * LANES:20 * LANES] = rope(y).astype(p_ref.dtype)
            elif s == 19:
                p_ref[:, 20 * LANES:21 * LANES] = y.astype(p_ref.dtype)
            elif s == 20:
                aux_ref[:, 0:LANES] = rope(y)
            elif s == 21:
                aux_ref[:, LANES:2 * LANES] = y
            else:
                aux_ref[:, 2 * LANES:3 * LANES] = jax.nn.sigmoid(y)


def _compress_kernel(ck_ref, cv_ref, kpl_ref, kph_ref, vpl_ref, vph_ref,
                     kw1_ref, kw2_ref, vw1_ref, vw2_ref, ko_ref, vo_ref, *, nc):
    half = CMP_STRIDE * HEAD_DIM

    def compress(c_ref, plo_ref, phi_ref, w1_ref, w2_ref):
        out = None
        for g in range(NSA_KV_GROUPS):
            c = c_ref[g]
            a = _dot((c + plo_ref[...]).astype(MXU_DTYPE), w1_ref[:half, :])
            b = _dot((c + phi_ref[...]).astype(MXU_DTYPE), w1_ref[half:, :])
            hid = a + pltpu.roll(b, nc - 1, 0)
            act = hid * jax.nn.sigmoid(hid)
            o = _dot(act.astype(MXU_DTYPE), w2_ref[g])
            out = o if out is None else out + o
        return out

    ko_ref[...] = compress(ck_ref, kpl_ref, kph_ref, kw1_ref, kw2_ref).astype(ko_ref.dtype)
    vo_ref[...] = compress(cv_ref, vpl_ref, vph_ref, vw1_ref, vw2_ref).astype(vo_ref.dtype)


def _flash_init(m_ref, l_ref, acc_ref):
    m_ref[...] = jnp.full(m_ref.shape, NEG_INF, F32)
    l_ref[...] = jnp.zeros(l_ref.shape, F32)
    acc_ref[...] = jnp.zeros(acc_ref.shape, F32)


def _flash_step(q, k, v, m_ref, l_ref, acc_ref, mask=None):
    s = _nt_dot(q, k)
    if mask is not None:
        s = jnp.where(mask, s, NEG_INF)
    m_prev = m_ref[...]
    m_new = jnp.maximum(m_prev, jnp.max(s, axis=-1, keepdims=True))
    alpha = jnp.exp(m_prev - m_new)
    p = jnp.exp(s - m_new)
    l_ref[...] = alpha * l_ref[...] + jnp.sum(p, axis=-1, keepdims=True)
    acc_ref[...] = alpha * acc_ref[...] + _dot(p.astype(MXU_DTYPE), v)
    m_ref[...] = m_new


def _diff_kernel(lq1_ref, lk1_ref, lq2_ref, lk2_ref, subln_ref, q_ref, k_ref, v_ref, o_ref,
                 m_ref, l_ref, acc_ref, *, tq, tk, lam_init):
    qi = pl.program_id(2)
    s0 = qi * tq
    lane = lax.broadcasted_iota(jnp.int32, (tq, LANES), 1)
    qf = q_ref[...].astype(F32)
    zero = jnp.zeros_like(qf)
    q2 = jnp.concatenate([jnp.where(lane < HEAD_DIM, qf, zero),
                          jnp.where(lane >= HEAD_DIM, qf, zero)], axis=0).astype(MXU_DTYPE)

    _flash_init(m_ref, l_ref, acc_ref)
    sub = tq // tk

    def body(j, carry):
        k0 = pl.multiple_of(j * tk, tk)
        _flash_step(q2, k_ref[pl.ds(k0, tk), :], v_ref[pl.ds(k0, tk), :], m_ref, l_ref, acc_ref)
        return carry

    lax.fori_loop(0, qi * sub, body, 0)
    rowpos = lax.broadcasted_iota(jnp.int32, (2 * tq, tk), 0) & (tq - 1)
    col = lax.broadcasted_iota(jnp.int32, (2 * tq, tk), 1)
    for t in range(sub):
        k0 = pl.multiple_of(s0 + t * tk, tk)
        _flash_step(q2, k_ref[pl.ds(k0, tk), :], v_ref[pl.ds(k0, tk), :], m_ref, l_ref, acc_ref,
                    mask=(col + t * tk) <= rowpos)

    lam = (jnp.exp(jnp.sum(lq1_ref[...] * lk1_ref[...], axis=-1, keepdims=True))
           - jnp.exp(jnp.sum(lq2_ref[...] * lk2_ref[...], axis=-1, keepdims=True)) + lam_init)
    o12 = acc_ref[...] / l_ref[...]
    o = o12[:tq] - lam * o12[tq:]
    o_ref[...] = (_rms(o, subln_ref[...]) * (1.0 - lam_init)).astype(o_ref.dtype)


def _nsa_kernel(q_ref, gate_ref, kc_ref, vc_ref, ovt_ref, ka0_ref, ka1_ref, vs_ref, kw_ref, vw_ref,
                o_ref, m_ref, l_ref, acc_ref, *, tq, tk, nc, n_sel, top_n):
    hg = NSA_HEADS_PER_GROUP
    rows = hg * tq
    qi = pl.program_id(1)
    s0 = qi * tq
    lane = lax.broadcasted_iota(jnp.int32, (tq, LANES), 1)
    qf = [q_ref[:, h * LANES:(h + 1) * LANES].astype(F32) for h in range(hg)]
    gates = gate_ref[...]
    zero = jnp.zeros((tq, LANES), F32)

    rowpos = lax.broadcasted_iota(jnp.int32, (rows, tk), 0) & (tq - 1)
    col = lax.broadcasted_iota(jnp.int32, (rows, tk), 1)
    rowpos_w = lax.broadcasted_iota(jnp.int32, (rows, tq), 0) & (tq - 1)
    col_w = lax.broadcasted_iota(jnp.int32, (rows, tq), 1)
    sub = tq // tk
    n_prev = WINDOW // tq

    outs = []
    for g in range(NSA_KV_GROUPS):
        in_half = (lane >= HEAD_DIM) if g else (lane < HEAD_DIM)
        q_plain = jnp.concatenate([jnp.where(in_half, qf[h], zero) for h in range(hg)],
                                  axis=0).astype(MXU_DTYPE)

        sc = _nt_dot(q_plain, kc_ref[...])
        n_idx = lax.broadcasted_iota(jnp.int32, (rows, nc), 1)
        qpos_c = s0 + (lax.broadcasted_iota(jnp.int32, (rows, nc), 0) & (tq - 1))
        cmask = n_idx * CMP_STRIDE + (CMP_BLOCK - 1) <= qpos_c
        sc = jnp.where(cmask, sc, NEG_INF)
        pc = jnp.where(cmask, jnp.exp(sc - jnp.max(sc, axis=-1, keepdims=True)), 0.0)
        lc = jnp.sum(pc, axis=-1, keepdims=True)
        pc = (pc / jnp.where(lc > 0.0, lc, 1.0)).astype(MXU_DTYPE)
        o_cmp = _dot(pc, vc_ref[...])

        imp_t = None
        for h in range(hg):
            part = _nt_dot(ovt_ref[...], pc[h * tq:(h + 1) * tq])
            imp_t = part if imp_t is None else imp_t + part
        imp_t = imp_t[:SLC_BLOCK]
        jb = lax.broadcasted_iota(jnp.int32, (SLC_BLOCK, tq), 0)
        qp = s0 + lax.broadcasted_iota(jnp.int32, (SLC_BLOCK, tq), 1)
        cur = qp // SLC_BLOCK
        valid = jb * SLC_BLOCK <= qp
        forced = (jb == 0) | (jb == cur) | (jb == cur - 1)
        key = jnp.where(forced, FORCED_IMPORTANCE, jnp.where(valid, imp_t, -1.0))
        rank = jnp.zeros((SLC_BLOCK, tq), jnp.int32)
        for i in range(n_sel):
            ri = key[i:i + 1, :]
            rank = rank + jnp.where(ri > key, 1, 0) + jnp.where((ri == key) & (jb > i), 1, 0)
        selected = (rank < top_n) & valid
        bias_t = jnp.where(selected, 0.0, NEG_INF)
        pad_t = jnp.zeros((SLC_BLOCK, tq), F32)
        bias = jnp.transpose(jnp.concatenate([pad_t, bias_t] if g == 0 else [bias_t, pad_t], axis=0))
        q_sel = jnp.concatenate([jnp.where(in_half, qf[h], bias) for h in range(hg)],
                                axis=0).astype(MXU_DTYPE)

        ka_ref = ka1_ref if g else ka0_ref
        _flash_init(m_ref, l_ref, acc_ref)

        def slc_body(j, carry, q_sel=q_sel, ka_ref=ka_ref):
            k0 = pl.multiple_of(j * tk, tk)
            _flash_step(q_sel, ka_ref[pl.ds(k0, tk), :], vs_ref[pl.ds(k0, tk), :], m_ref, l_ref, acc_ref)
            return carry

        lax.fori_loop(0, qi * sub, slc_body, 0)
        for t in range(sub):
            k0 = pl.multiple_of(s0 + t * tk, tk)
            _flash_step(q_sel, ka_ref[pl.ds(k0, tk), :], vs_ref[pl.ds(k0, tk), :], m_ref, l_ref, acc_ref,
                        mask=(col + t * tk) <= rowpos)
        o_slc = acc_ref[...] / l_ref[...]

        _flash_init(m_ref, l_ref, acc_ref)
        d0 = pl.multiple_of(s0, tq)
        _flash_step(q_plain, kw_ref[pl.ds(d0, tq), :], vw_ref[pl.ds(d0, tq), :], m_ref, l_ref, acc_ref,
                    mask=col_w <= rowpos_w)
        for t in range(1, n_prev + 1):
            @pl.when(qi >= t)
            def _(t=t, q_plain=q_plain):
                k0 = pl.multiple_of(s0 - t * tq, tq)
                _flash_step(q_plain, kw_ref[pl.ds(k0, tq), :], vw_ref[pl.ds(k0, tq), :],
                            m_ref, l_ref, acc_ref,
                            mask=(col_w > rowpos_w) if t == n_prev else None)
        o_win = acc_ref[...] / l_ref[...]

        per_head = []
        for h in range(hg):
            c = (g * hg + h) * N_BRANCHES
            r = slice(h * tq, (h + 1) * tq)
            per_head.append(gates[:, c:c + 1] * o_cmp[r] + gates[:, c + 1:c + 2] * o_slc[r]
                            + gates[:, c + 2:c + 3] * o_win[r])
        outs.append(per_head)

    for h in range(hg):
        o_ref[:, h * LANES:(h + 1) * LANES] = jnp.where(lane < HEAD_DIM, outs[0][h], outs[1][h]).astype(o_ref.dtype)


def _post_kernel(x_ref, od_ref, on_ref, wod_ref, won_ref, gpost_ref, gpre_ref, wg_ref, wu_ref, wd_ref,
                 gffn_ref, o_ref, act_ref, *, d_ff, ff_chunk):
    mix = _dot(od_ref[...], wod_ref[...]) + _dot(on_ref[...], won_ref[...])
    x1 = x_ref[...] + _rms(mix, gpost_ref[...])
    h = _rms(x1, gpre_ref[...]).astype(MXU_DTYPE)
    for c in range(0, d_ff, ff_chunk):
        gate = _dot(h, wg_ref[:, c:c + ff_chunk])
        up = _dot(h, wu_ref[:, c:c + ff_chunk])
        act_ref[:, c:c + ff_chunk] = (gate * jax.nn.sigmoid(gate) * up).astype(act_ref.dtype)
    f = _dot(act_ref[...], wd_ref[...])
    o_ref[...] = x1 + _rms(f, gffn_ref[...])


def _resident(shape):
    nd = len(shape)
    return pl.BlockSpec(shape, lambda *_: (0,) * nd, pipeline_mode=pl.Buffered(1))


def _params(sem):
    return pltpu.CompilerParams(dimension_semantics=sem, vmem_limit_bytes=VMEM_LIMIT)


def _rope_tables(S):
    inv = 1.0 / (ROPE_THETA ** (jnp.arange(0, HEAD_DIM, 2, dtype=F32) / HEAD_DIM))
    ang = jnp.arange(S, dtype=F32)[:, None] * inv[None, :]
    cos, sin = jnp.cos(ang), jnp.sin(ang)
    return jnp.tile(cos, (1, 4)), jnp.concatenate([-sin, sin, -sin, sin], axis=1)


def _selection_overlap_t(nc, n_cmp, n_sel):
    c0 = np.arange(n_cmp)[:, None] * CMP_STRIDE
    b0 = np.arange(n_sel)[None, :] * SLC_BLOCK
    ov = np.clip(np.minimum(c0 + CMP_BLOCK, b0 + SLC_BLOCK) - np.maximum(c0, b0), 0, None) / CMP_BLOCK
    full = np.zeros((LANES, nc), np.float32)
    full[:n_sel, :n_cmp] = ov.T
    return full


def _layer(x, layer, attn_pre_norm, w_in, lq1, lk1, lq2, lk2, diff_subln, k_pos, k_w1, k_w2,
           v_pos, v_w1, v_w2, w_out, attn_post_norm, ffn_pre_norm, w_gate, w_up, w_down, ffn_post_norm):
    B, S, D = x.shape
    N = B * S
    d_ff = w_gate.shape[1]
    nc = S // CMP_STRIDE
    n_cmp = (S - CMP_BLOCK) // CMP_STRIDE + 1
    n_sel = S // SLC_BLOCK
    top_n = min(SLC_TOPK, n_sel)
    assert n_sel <= SLC_BLOCK and S % 512 == 0
    lam_init = 0.8 - 0.6 * math.exp(-0.3 * layer)
    dt = MXU_DTYPE

    hg, G, d = NSA_HEADS_PER_GROUP, NSA_KV_GROUPS, HEAD_DIM
    nq_perm = np.array([1536 + (g * hg + h) * d + e for h in range(hg) for g in range(G) for e in range(d)])
    cols = np.concatenate([np.arange(0, 1536), nq_perm,
                           np.arange(2304, 2816),
                           np.arange(2048, 2304),
                           np.arange(2816, 2840)])
    w_cat = jnp.pad(w_in[:, cols], ((0, 0), (0, LANES - hg * G * N_BRANCHES))).astype(dt)
    w_out_d = w_out[:512].astype(dt)
    w_out_n = w_out[nq_perm - 1536 + 512].astype(dt)
    cos_t, sin_t = _rope_tables(S)

    tm = 512
    seq_tiles = S // tm
    n_cols = (N_MAIN_SLABS + N_AUX_SLABS) * LANES
    p, aux = pl.pallas_call(
        functools.partial(_inproj_kernel, tm=tm, seq_tiles=seq_tiles),
        grid=(N // tm,),
        in_specs=[pl.BlockSpec((tm, D), lambda i: (i, 0)),
                  _resident((1, D)),
                  _resident((D, n_cols)),
                  pl.BlockSpec((tm, LANES), lambda i: (i % seq_tiles, 0)),
                  pl.BlockSpec((tm, LANES), lambda i: (i % seq_tiles, 0))],
        out_specs=[pl.BlockSpec((tm, N_P_SLABS * LANES), lambda i: (i, 0)),
                   pl.BlockSpec((tm, N_AUX_SLABS * LANES), lambda i: (i, 0))],
        out_shape=[jax.ShapeDtypeStruct((N, N_P_SLABS * LANES), dt),
                   jax.ShapeDtypeStruct((N, N_AUX_SLABS * LANES), F32)],
        compiler_params=_params(("parallel",)),
        name="inproj",
    )(x.reshape(N, D), attn_pre_norm.reshape(1, D), w_cat, cos_t, sin_t)
    p = p.reshape(B, S, N_P_SLABS * LANES)
    aux = aux.reshape(B, S, N_AUX_SLABS * LANES)

    def chunks(t):
        return t.reshape(B, nc, CMP_STRIDE, G, d).transpose(0, 3, 1, 2, 4).reshape(B, G, nc, CMP_STRIDE * d)

    def pos_rows(pos):
        return pos[:CMP_STRIDE].reshape(1, -1), pos[CMP_STRIDE:].reshape(1, -1)

    def w2_halves(w2):
        return jnp.stack([jnp.pad(w2, ((0, 0), (g * d, (G - 1 - g) * d))) for g in range(G)]).astype(dt)

    kpl, kph = pos_rows(k_pos)
    vpl, vph = pos_rows(v_pos)
    chunk_w = CMP_STRIDE * d
    chunk_spec = pl.BlockSpec((None, G, nc, chunk_w), lambda b: (b, 0, 0, 0))
    kcmp, vcmp = pl.pallas_call(
        functools.partial(_compress_kernel, nc=nc),
        grid=(B,),
        in_specs=[chunk_spec, chunk_spec,
                  _resident((1, chunk_w)), _resident((1, chunk_w)), _resident((1, chunk_w)), _resident((1, chunk_w)),
                  _resident((2 * chunk_w, CMP_HIDDEN)), _resident((G, CMP_HIDDEN, LANES)),
                  _resident((2 * chunk_w, CMP_HIDDEN)), _resident((G, CMP_HIDDEN, LANES))],
        out_specs=[pl.BlockSpec((None, nc, LANES), lambda b: (b, 0, 0)),
                   pl.BlockSpec((None, nc, LANES), lambda b: (b, 0, 0))],
        out_shape=[jax.ShapeDtypeStruct((B, nc, LANES), dt), jax.ShapeDtypeStruct((B, nc, LANES), dt)],
        compiler_params=_params(("parallel",)),
        name="compress",
    )(chunks(aux[..., :LANES]), chunks(aux[..., LANES:2 * LANES]), kpl, kph, vpl, vph,
      k_w1.astype(dt), w2_halves(k_w2), v_w1.astype(dt), w2_halves(v_w2))

    tq_d, tk_d = 256, 256
    lam_vec = [v.reshape(1, d) for v in (lq1, lk1, lq2, lk2)]
    o_diff = pl.pallas_call(
        functools.partial(_diff_kernel, tq=tq_d, tk=tk_d, lam_init=lam_init),
        grid=(B, DIFF_HEADS, S // tq_d),
        in_specs=[_resident((1, d))] * 4 + [_resident((1, LANES)),
                  pl.BlockSpec((None, tq_d, LANES), lambda b, h, i: (b, i, h)),
                  pl.BlockSpec((None, S, LANES), lambda b, h, i: (b, 0, 4 + h)),
                  pl.BlockSpec((None, S, LANES), lambda b, h, i: (b, 0, 8 + h))],
        out_specs=pl.BlockSpec((None, tq_d, LANES), lambda b, h, i: (b, i, h)),
        out_shape=jax.ShapeDtypeStruct((B, S, DIFF_HEADS * LANES), dt),
        scratch_shapes=[pltpu.VMEM((2 * tq_d, 1), F32), pltpu.VMEM((2 * tq_d, 1), F32),
                        pltpu.VMEM((2 * tq_d, LANES), F32)],
        compiler_params=_params(("parallel", "parallel", "arbitrary")),
        name="diff_attn",
    )(*lam_vec, diff_subln.reshape(1, LANES), p, p, p)

    tq_n, tk_n = 256, 256
    ovt = jnp.asarray(_selection_overlap_t(nc, n_cmp, n_sel)).astype(dt)
    rows = NSA_HEADS_PER_GROUP * tq_n

    def seq_slab(c):
        return pl.BlockSpec((None, S, LANES), lambda b, i: (b, 0, c))

    o_nsa = pl.pallas_call(
        functools.partial(_nsa_kernel, tq=tq_n, tk=tk_n, nc=nc, n_sel=n_sel, top_n=top_n),
        grid=(B, S // tq_n),
        in_specs=[pl.BlockSpec((None, tq_n, 4 * LANES), lambda b, i: (b, i, 3)),
                  pl.BlockSpec((None, tq_n, LANES), lambda b, i: (b, i, 2)),
                  pl.BlockSpec((None, nc, LANES), lambda b, i: (b, 0, 0)),
                  pl.BlockSpec((None, nc, LANES), lambda b, i: (b, 0, 0)),
                  _resident((LANES, nc)),
                  seq_slab(16), seq_slab(17), seq_slab(18), seq_slab(19), seq_slab(20)],
        out_specs=pl.BlockSpec((None, tq_n, 4 * LANES), lambda b, i: (b, i, 0)),
        out_shape=jax.ShapeDtypeStruct((B, S, 4 * LANES), dt),
        scratch_shapes=[pltpu.VMEM((rows, 1), F32), pltpu.VMEM((rows, 1), F32),
                        pltpu.VMEM((rows, LANES), F32)],
        compiler_params=_params(("parallel", "arbitrary")),
        name="nsa_attn",
    )(p, aux, kcmp, vcmp, ovt, p, p, p, p, p)

    tm2 = 512
    row = lambda i: (i, 0)
    out = pl.pallas_call(
        functools.partial(_post_kernel, d_ff=d_ff, ff_chunk=256),
        grid=(N // tm2,),
        in_specs=[pl.BlockSpec((tm2, D), row),
                  pl.BlockSpec((tm2, 512), row), pl.BlockSpec((tm2, 512), row),
                  _resident((512, D)), _resident((512, D)), _resident((1, D)), _resident((1, D)),
                  _resident((D, d_ff)), _resident((D, d_ff)), _resident((d_ff, D)), _resident((1, D))],
        out_specs=pl.BlockSpec((tm2, D), row),
        out_shape=jax.ShapeDtypeStruct((N, D), F32),
        scratch_shapes=[pltpu.VMEM((tm2, d_ff), dt)],
        compiler_params=_params(("parallel",)),
        name="post",
    )(x.reshape(N, D), o_diff.reshape(N, 512), o_nsa.reshape(N, 512), w_out_d, w_out_n,
      attn_post_norm.reshape(1, D), ffn_pre_norm.reshape(1, D),
      w_gate.astype(dt), w_up.astype(dt), w_down.astype(dt), ffn_post_norm.reshape(1, D))
    return out.reshape(B, S, D)


def kernel(x, attn_pre_norm, w_in, lambda_q1, lambda_k1, lambda_q2, lambda_k2, diff_subln, k_cmp_pos, k_cmp_w1, k_cmp_w2, v_cmp_pos, v_cmp_w1, v_cmp_w2, w_out, attn_post_norm, ffn_pre_norm, w_gate, w_up, w_down, ffn_post_norm):
    for l in range(w_in.shape[0]):
        x = _layer(x, l, attn_pre_norm[l], w_in[l], lambda_q1[l], lambda_k1[l], lambda_q2[l], lambda_k2[l],
                   diff_subln[l], k_cmp_pos[l], k_cmp_w1[l], k_cmp_w2[l], v_cmp_pos[l], v_cmp_w1[l], v_cmp_w2[l],
                   w_out[l], attn_post_norm[l], ffn_pre_norm[l], w_gate[l], w_up[l], w_down[l], ffn_post_norm[l])
    return x
```

```python
import functools
import math

import numpy as np
import jax
import jax.numpy as jnp
from jax import lax
from jax.experimental import pallas as pl
from jax.experimental.pallas import tpu as pltpu

F32 = jnp.float32
MXU_DTYPE = jnp.bfloat16

LANES = 128
HEAD_DIM = 64
ROPE_THETA = 10000.0
NORM_EPS = 1e-6
NEG_INF = -1e30
FORCED_IMPORTANCE = 3e38

DIFF_HEADS = 4
NSA_HEADS_PER_GROUP = 4
NSA_KV_GROUPS = 2
CMP_BLOCK = 32
CMP_STRIDE = 16
CMP_HIDDEN = 4 * HEAD_DIM
SLC_BLOCK = 64
SLC_TOPK = 16
WINDOW = 512
N_BRANCHES = 3

N_MAIN_SLABS = 20
N_AUX_SLABS = 3
N_P_SLABS = 21
VMEM_LIMIT = 56 * 1024 * 1024


def _nt_dot(a, b):
    return lax.dot_general(a, b, (((1,), (1,)), ((), ())), preferred_element_type=F32)


def _dot(a, b):
    return jnp.dot(a, b, preferred_element_type=F32)


def _rms(x, g):
    return x * lax.rsqrt(jnp.mean(x * x, axis=-1, keepdims=True) + NORM_EPS) * g


def _inproj_kernel(x_ref, g_ref, w_ref, cos_ref, sin_ref, p_ref, aux_ref, *, tm, seq_tiles):
    h = _rms(x_ref[...], g_ref[...]).astype(MXU_DTYPE)
    cos = cos_ref[...]
    sin = sin_ref[...]
    lane = lax.broadcasted_iota(jnp.int32, (tm, LANES), 1)
    low_half = (lane & (HEAD_DIM - 1)) < HEAD_DIM // 2

    def rope(y):
        fwd = pltpu.roll(y, HEAD_DIM // 2, 1)
        bwd = pltpu.roll(y, LANES - HEAD_DIM // 2, 1)
        return y * cos + jnp.where(low_half, bwd, fwd) * sin

    pos = (pl.program_id(0) % seq_tiles) * tm + lax.broadcasted_iota(jnp.int32, (tm, LANES), 0)
    blk = pos // SLC_BLOCK
    scale = HEAD_DIM ** -0.5

    n_slabs = N_MAIN_SLABS + N_AUX_SLABS
    for c0 in range(0, n_slabs, 2):
        c1 = min(c0 + 2, n_slabs)
        y2 = _dot(h, w_ref[:, c0 * LANES:c1 * LANES])
        for s in range(c0, c1):
            y = y2[:, (s - c0) * LANES:(s - c0 + 1) * LANES]
            if s < 4 or 12 <= s < 16:
                p_ref[:, s * LANES:(s + 1) * LANES] = (rope(y) * scale).astype(p_ref.dtype)
            elif s < 8:
                p_ref[:, s * LANES:(s + 1) * LANES] = rope(y).astype(p_ref.dtype)
            elif s < 12:
                p_ref[:, s * LANES:(s + 1) * LANES] = y.astype(p_ref.dtype)
            elif s == 16:
                r = rope(y)
                ind_hi = jnp.where(lane - HEAD_DIM == blk, 1.0, 0.0)
                ind_lo = jnp.where(lane == blk, 1.0, 0.0)
                p_ref[:, 16 * LANES:17 * LANES] = jnp.where(lane < HEAD_DIM, r, ind_hi).astype(p_ref.dtype)
                p_ref[:, 17 * LANES:18 * LANES] = jnp.where(lane >= HEAD_DIM, r, ind_lo).astype(p_ref.dtype)
            elif s == 17:
                p_ref[:, 18 * LANES:19 * LANES] = y.astype(p_ref.dtype)
            elif s == 18:
                p_ref[:, 19 * LANES:20 * LANES] = rope(y).astype(p_ref.dtype)
            elif s == 19:
                p_ref[:, 20 * LANES:21 * LANES] = y.astype(p_ref.dtype)
            elif s == 20:
                aux_ref[:, 0:LANES] = rope(y)
            elif s == 21:
                aux_ref[:, LANES:2 * LANES] = y
            else:
                aux_ref[:, 2 * LANES:3 * LANES] = jax.nn.sigmoid(y)


def _compress_kernel(ck_ref, cv_ref, kpl_ref, kph_ref, vpl_ref, vph_ref,
                     kw1_ref, kw2_ref, vw1_ref, vw2_ref, ko_ref, vo_ref, *, nc):
    half = CMP_STRIDE * HEAD_DIM

    def compress(c_ref, plo_ref, phi_ref, w1_ref, w2_ref):
        out = None
        for g in range(NSA_KV_GROUPS):
            c = c_ref[g]
            a = _dot((c + plo_ref[...]).astype(MXU_DTYPE), w1_ref[:half, :])
            b = _dot((c + phi_ref[...]).astype(MXU_DTYPE), w1_ref[half:, :])
            hid = a + pltpu.roll(b, nc - 1, 0)
            act = hid * jax.nn.sigmoid(hid)
            o = _dot(act.astype(MXU_DTYPE), w2_ref[g])
            out = o if out is None else out + o
        return out

    ko_ref[...] = compress(ck_ref, kpl_ref, kph_ref, kw1_ref, kw2_ref).astype(ko_ref.dtype)
    vo_ref[...] = compress(cv_ref, vpl_ref, vph_ref, vw1_ref, vw2_ref).astype(vo_ref.dtype)


def _flash_init(m_ref, l_ref, acc_ref):
    m_ref[...] = jnp.full(m_ref.shape, NEG_INF, F32)
    l_ref[...] = jnp.zeros(l_ref.shape, F32)
    acc_ref[...] = jnp.zeros(acc_ref.shape, F32)


def _lane_tiles(x):
    return [x[:, c:c + LANES] for c in range(0, x.shape[1], LANES)]


def _flash_step(q, k, v, m_ref, l_ref, acc_ref, mask=None):
    s = _nt_dot(q, k)
    if mask is not None:
        s = jnp.where(mask, s, NEG_INF)
    n_tiles = s.shape[1] // LANES
    m_prev = m_ref[...]
    m_new = jnp.maximum(m_prev, jnp.max(s, axis=-1, keepdims=True))
    alpha = jnp.exp(m_prev - m_new)
    p = jnp.exp(s - jnp.concatenate([m_new] * n_tiles, axis=1))
    l_ref[...] = alpha * l_ref[...] + functools.reduce(lambda a, b: a + b, _lane_tiles(p))
    acc_ref[...] = alpha * acc_ref[...] + _dot(p.astype(MXU_DTYPE), v)
    m_ref[...] = m_new


def _flash_finish(l_ref, acc_ref):
    return acc_ref[...] / jnp.sum(l_ref[...], axis=-1, keepdims=True)


def _causal_flash(q, k_ref, v_ref, m_ref, l_ref, acc_ref, s0, tq, tk):
    rows = q.shape[0]
    _flash_init(m_ref, l_ref, acc_ref)

    def body(j, carry):
        k0 = pl.multiple_of(j * tk, tk)
        _flash_step(q, k_ref[pl.ds(k0, tk), :], v_ref[pl.ds(k0, tk), :], m_ref, l_ref, acc_ref)
        return carry

    n_full = s0 // tk
    lax.fori_loop(0, n_full, body, 0)
    qpos = s0 + (lax.broadcasted_iota(jnp.int32, (rows, tk), 0) & (tq - 1))
    col = lax.broadcasted_iota(jnp.int32, (rows, tk), 1)
    for t in range(max(tq // tk, 1)):
        k0 = pl.multiple_of((n_full + t) * tk, tk)
        _flash_step(q, k_ref[pl.ds(k0, tk), :], v_ref[pl.ds(k0, tk), :], m_ref, l_ref, acc_ref,
                    mask=(k0 + col) <= qpos)
    return _flash_finish(l_ref, acc_ref)


def _diff_kernel(lq1_ref, lk1_ref, lq2_ref, lk2_ref, subln_ref, q_ref, k_ref, v_ref, o_ref,
                 m_ref, l_ref, acc_ref, *, tq, tk, lam_init):
    qi = pl.program_id(2)
    s0 = qi * tq
    lane = lax.broadcasted_iota(jnp.int32, (tq, LANES), 1)
    qf = q_ref[...].astype(F32)
    zero = jnp.zeros_like(qf)
    q2 = jnp.concatenate([jnp.where(lane < HEAD_DIM, qf, zero),
                          jnp.where(lane >= HEAD_DIM, qf, zero)], axis=0).astype(MXU_DTYPE)

    o12 = _causal_flash(q2, k_ref, v_ref, m_ref, l_ref, acc_ref, s0, tq, tk)
    lam = (jnp.exp(jnp.sum(lq1_ref[...] * lk1_ref[...], axis=-1, keepdims=True))
           - jnp.exp(jnp.sum(lq2_ref[...] * lk2_ref[...], axis=-1, keepdims=True)) + lam_init)
    o = o12[:tq] - lam * o12[tq:]
    o_ref[...] = (_rms(o, subln_ref[...]) * (1.0 - lam_init)).astype(o_ref.dtype)


def _nsa_kernel(q_ref, gate_ref, kc_ref, vc_ref, ovt_ref, ka0_ref, ka1_ref, vs_ref, kw_ref, vw_ref,
                o_ref, m_ref, l_ref, acc_ref, *, tq, tk, nc, n_sel, top_n):
    hg = NSA_HEADS_PER_GROUP
    rows = hg * tq
    qi = pl.program_id(1)
    s0 = qi * tq
    lane = lax.broadcasted_iota(jnp.int32, (tq, LANES), 1)
    qf = [q_ref[:, h * LANES:(h + 1) * LANES].astype(F32) for h in range(hg)]
    gates = gate_ref[...]
    zero = jnp.zeros((tq, LANES), F32)

    wk = WINDOW + tq
    w0 = pl.multiple_of(jnp.maximum(s0 - WINDOW, 0), tq)
    back = (s0 - w0) + (lax.broadcasted_iota(jnp.int32, (rows, wk), 0) & (tq - 1)) \
        - lax.broadcasted_iota(jnp.int32, (rows, wk), 1)
    win_mask = (back >= 0) & (back < WINDOW)

    outs = []
    for g in range(NSA_KV_GROUPS):
        in_half = (lane >= HEAD_DIM) if g else (lane < HEAD_DIM)
        q_plain = jnp.concatenate([jnp.where(in_half, qf[h], zero) for h in range(hg)],
                                  axis=0).astype(MXU_DTYPE)

        sc = _nt_dot(q_plain, kc_ref[...])
        n_idx = lax.broadcasted_iota(jnp.int32, (rows, nc), 1)
        qpos_c = s0 + (lax.broadcasted_iota(jnp.int32, (rows, nc), 0) & (tq - 1))
        cmask = n_idx * CMP_STRIDE + (CMP_BLOCK - 1) <= qpos_c
        sc = jnp.where(cmask, sc, NEG_INF)
        pc = jnp.where(cmask, jnp.exp(sc - jnp.max(sc, axis=-1, keepdims=True)), 0.0)
        lc = jnp.sum(pc, axis=-1, keepdims=True)
        pc = (pc / jnp.where(lc > 0.0, lc, 1.0)).astype(MXU_DTYPE)
        o_cmp = _dot(pc, vc_ref[...])

        imp_t = None
        for h in range(hg):
            part = _nt_dot(ovt_ref[...], pc[h * tq:(h + 1) * tq])
            imp_t = part if imp_t is None else imp_t + part
        imp_t = imp_t[:SLC_BLOCK]
        jb = lax.broadcasted_iota(jnp.int32, (SLC_BLOCK, tq), 0)
        qp = s0 + lax.broadcasted_iota(jnp.int32, (SLC_BLOCK, tq), 1)
        cur = qp // SLC_BLOCK
        valid = jb * SLC_BLOCK <= qp
        forced = (jb == 0) | (jb == cur) | (jb == cur - 1)
        key = jnp.where(forced, FORCED_IMPORTANCE, jnp.where(valid, imp_t, -1.0))
        rank = jnp.zeros((SLC_BLOCK, tq), jnp.int32)
        for i in range(n_sel):
            ri = key[i:i + 1, :]
            rank = rank + jnp.where(ri > key, 1, 0) + jnp.where((ri == key) & (jb > i), 1, 0)
        selected = (rank < top_n) & valid
        bias_t = jnp.where(selected, 0.0, NEG_INF)
        pad_t = jnp.zeros((SLC_BLOCK, tq), F32)
        bias = jnp.transpose(jnp.concatenate([pad_t, bias_t] if g == 0 else [bias_t, pad_t], axis=0))
        q_sel = jnp.concatenate([jnp.where(in_half, qf[h], bias) for h in range(hg)],
                                axis=0).astype(MXU_DTYPE)

        o_slc = _causal_flash(q_sel, ka1_ref if g else ka0_ref, vs_ref, m_ref, l_ref, acc_ref, s0, tq, tk)

        sw = _nt_dot(q_plain, kw_ref[pl.ds(w0, wk), :])
        sw = jnp.where(win_mask, sw, NEG_INF)
        pw = jnp.exp(sw - jnp.max(sw, axis=-1, keepdims=True))
        o_win = _dot(pw.astype(MXU_DTYPE), vw_ref[pl.ds(w0, wk), :]) / jnp.sum(pw, axis=-1, keepdims=True)

        per_head = []
        for h in range(hg):
            c = (g * hg + h) * N_BRANCHES
            r = slice(h * tq, (h + 1) * tq)
            per_head.append(gates[:, c:c + 1] * o_cmp[r] + gates[:, c + 1:c + 2] * o_slc[r]
                            + gates[:, c + 2:c + 3] * o_win[r])
        outs.append(per_head)

    for h in range(hg):
        o_ref[:, h * LANES:(h + 1) * LANES] = jnp.where(lane < HEAD_DIM, outs[0][h], outs[1][h]).astype(o_ref.dtype)


def _post_kernel(x_ref, od_ref, on_ref, wod_ref, won_ref, gpost_ref, gpre_ref, wg_ref, wu_ref, wd_ref,
                 gffn_ref, o_ref, act_ref, *, d_ff, ff_chunk):
    mix = _dot(od_ref[...], wod_ref[...]) + _dot(on_ref[...], won_ref[...])
    x1 = x_ref[...] + _rms(mix, gpost_ref[...])
    h = _rms(x1, gpre_ref[...]).astype(MXU_DTYPE)
    for c in range(0, d_ff, ff_chunk):
        gate = _dot(h, wg_ref[:, c:c + ff_chunk])
        up = _dot(h, wu_ref[:, c:c + ff_chunk])
        act_ref[:, c:c + ff_chunk] = (gate * jax.nn.sigmoid(gate) * up).astype(act_ref.dtype)
    f = _dot(act_ref[...], wd_ref[...])
    o_ref[...] = x1 + _rms(f, gffn_ref[...])


def _resident(shape):
    nd = len(shape)
    return pl.BlockSpec(shape, lambda *_: (0,) * nd, pipeline_mode=pl.Buffered(1))


def _params(sem):
    return pltpu.CompilerParams(dimension_semantics=sem, vmem_limit_bytes=VMEM_LIMIT)


def _rope_tables(S):
    inv = 1.0 / (ROPE_THETA ** (jnp.arange(0, HEAD_DIM, 2, dtype=F32) / HEAD_DIM))
    ang = jnp.arange(S, dtype=F32)[:, None] * inv[None, :]
    cos, sin = jnp.cos(ang), jnp.sin(ang)
    return jnp.tile(cos, (1, 4)), jnp.concatenate([-sin, sin, -sin, sin], axis=1)


def _selection_overlap_t(nc, n_cmp, n_sel):
    c0 = np.arange(n_cmp)[:, None] * CMP_STRIDE
    b0 = np.arange(n_sel)[None, :] * SLC_BLOCK
    ov = np.clip(np.minimum(c0 + CMP_BLOCK, b0 + SLC_BLOCK) - np.maximum(c0, b0), 0, None) / CMP_BLOCK
    full = np.zeros((LANES, nc), np.float32)
    full[:n_sel, :n_cmp] = ov.T
    return full


def _layer(x, layer, attn_pre_norm, w_in, lq1, lk1, lq2, lk2, diff_subln, k_pos, k_w1, k_w2,
           v_pos, v_w1, v_w2, w_out, attn_post_norm, ffn_pre_norm, w_gate, w_up, w_down, ffn_post_norm):
    B, S, D = x.shape
    N = B * S
    d_ff = w_gate.shape[1]
    nc = S // CMP_STRIDE
    n_cmp = (S - CMP_BLOCK) // CMP_STRIDE + 1
    n_sel = S // SLC_BLOCK
    top_n = min(SLC_TOPK, n_sel)
    assert n_sel <= SLC_BLOCK and S % 512 == 0
    lam_init = 0.8 - 0.6 * math.exp(-0.3 * layer)
    dt = MXU_DTYPE

    hg, G, d = NSA_HEADS_PER_GROUP, NSA_KV_GROUPS, HEAD_DIM
    nq_perm = np.array([1536 + (g * hg + h) * d + e for h in range(hg) for g in range(G) for e in range(d)])
    cols = np.concatenate([np.arange(0, 1536), nq_perm,
                           np.arange(2304, 2816),
                           np.arange(2048, 2304),
                           np.arange(2816, 2840)])
    w_cat = jnp.pad(w_in[:, cols], ((0, 0), (0, LANES - hg * G * N_BRANCHES))).astype(dt)
    w_out_d = w_out[:512].astype(dt)
    w_out_n = w_out[nq_perm - 1536 + 512].astype(dt)
    cos_t, sin_t = _rope_tables(S)

    tm = 512
    seq_tiles = S // tm
    n_cols = (N_MAIN_SLABS + N_AUX_SLABS) * LANES
    p, aux = pl.pallas_call(
        functools.partial(_inproj_kernel, tm=tm, seq_tiles=seq_tiles),
        grid=(N // tm,),
        in_specs=[pl.BlockSpec((tm, D), lambda i: (i, 0)),
                  _resident((1, D)),
                  _resident((D, n_cols)),
                  pl.BlockSpec((tm, LANES), lambda i: (i % seq_tiles, 0)),
                  pl.BlockSpec((tm, LANES), lambda i: (i % seq_tiles, 0))],
        out_specs=[pl.BlockSpec((tm, N_P_SLABS * LANES), lambda i: (i, 0)),
                   pl.BlockSpec((tm, N_AUX_SLABS * LANES), lambda i: (i, 0))],
        out_shape=[jax.ShapeDtypeStruct((N, N_P_SLABS * LANES), dt),
                   jax.ShapeDtypeStruct((N, N_AUX_SLABS * LANES), F32)],
        compiler_params=_params(("parallel",)),
        name="inproj",
    )(x.reshape(N, D), attn_pre_norm.reshape(1, D), w_cat, cos_t, sin_t)
    p = p.reshape(B, S, N_P_SLABS * LANES)
    aux = aux.reshape(B, S, N_AUX_SLABS * LANES)

    def chunks(t):
        return t.reshape(B, nc, CMP_STRIDE, G, d).transpose(0, 3, 1, 2, 4).reshape(B, G, nc, CMP_STRIDE * d)

    def pos_rows(pos):
        return pos[:CMP_STRIDE].reshape(1, -1), pos[CMP_STRIDE:].reshape(1, -1)

    def w2_halves(w2):
        return jnp.stack([jnp.pad(w2, ((0, 0), (g * d, (G - 1 - g) * d))) for g in range(G)]).astype(dt)

    kpl, kph = pos_rows(k_pos)
    vpl, vph = pos_rows(v_pos)
    chunk_w = CMP_STRIDE * d
    chunk_spec = pl.BlockSpec((None, G, nc, chunk_w), lambda b: (b, 0, 0, 0))
    kcmp, vcmp = pl.pallas_call(
        functools.partial(_compress_kernel, nc=nc),
        grid=(B,),
        in_specs=[chunk_spec, chunk_spec,
                  _resident((1, chunk_w)), _resident((1, chunk_w)), _resident((1, chunk_w)), _resident((1, chunk_w)),
                  _resident((2 * chunk_w, CMP_HIDDEN)), _resident((G, CMP_HIDDEN, LANES)),
                  _resident((2 * chunk_w, CMP_HIDDEN)), _resident((G, CMP_HIDDEN, LANES))],
        out_specs=[pl.BlockSpec((None, nc, LANES), lambda b: (b, 0, 0)),
                   pl.BlockSpec((None, nc, LANES), lambda b: (b, 0, 0))],
        out_shape=[jax.ShapeDtypeStruct((B, nc, LANES), dt), jax.ShapeDtypeStruct((B, nc, LANES), dt)],
        compiler_params=_params(("parallel",)),
        name="compress",
    )(chunks(aux[..., :LANES]), chunks(aux[..., LANES:2 * LANES]), kpl, kph, vpl, vph,
      k_w1.astype(dt), w2_halves(k_w2), v_w1.astype(dt), w2_halves(v_w2))

    tq_d, tk_d = 512, 512
    lam_vec = [v.reshape(1, d) for v in (lq1, lk1, lq2, lk2)]
    o_diff = pl.pallas_call(
        functools.partial(_diff_kernel, tq=tq_d, tk=tk_d, lam_init=lam_init),
        grid=(B, DIFF_HEADS, S // tq_d),
        in_specs=[_resident((1, d))] * 4 + [_resident((1, LANES)),
                  pl.BlockSpec((None, tq_d, LANES), lambda b, h, i: (b, i, h)),
                  pl.BlockSpec((None, S, LANES), lambda b, h, i: (b, 0, 4 + h)),
                  pl.BlockSpec((None, S, LANES), lambda b, h, i: (b, 0, 8 + h))],
        out_specs=pl.BlockSpec((None, tq_d, LANES), lambda b, h, i: (b, i, h)),
        out_shape=jax.ShapeDtypeStruct((B, S, DIFF_HEADS * LANES), dt),
        scratch_shapes=[pltpu.VMEM((2 * tq_d, LANES), F32)] * 3,
        compiler_params=_params(("parallel", "parallel", "arbitrary")),
        name="diff_attn",
    )(*lam_vec, diff_subln.reshape(1, LANES), p, p, p)

    tq_n, tk_n = 256, 512
    ovt = jnp.asarray(_selection_overlap_t(nc, n_cmp, n_sel)).astype(dt)
    rows = NSA_HEADS_PER_GROUP * tq_n

    def seq_slab(c):
        return pl.BlockSpec((None, S, LANES), lambda b, i: (b, 0, c))

    o_nsa = pl.pallas_call(
        functools.partial(_nsa_kernel, tq=tq_n, tk=tk_n, nc=nc, n_sel=n_sel, top_n=top_n),
        grid=(B, S // tq_n),
        in_specs=[pl.BlockSpec((None, tq_n, 4 * LANES), lambda b, i: (b, i, 3)),
                  pl.BlockSpec((None, tq_n, LANES), lambda b, i: (b, i, 2)),
                  pl.BlockSpec((None, nc, LANES), lambda b, i: (b, 0, 0)),
                  pl.BlockSpec((None, nc, LANES), lambda b, i: (b, 0, 0)),
                  _resident((LANES, nc)),
                  seq_slab(16), seq_slab(17), seq_slab(18), seq_slab(19), seq_slab(20)],
        out_specs=pl.BlockSpec((None, tq_n, 4 * LANES), lambda b, i: (b, i, 0)),
        out_shape=jax.ShapeDtypeStruct((B, S, 4 * LANES), dt),
        scratch_shapes=[pltpu.VMEM((rows, LANES), F32)] * 3,
        compiler_params=_params(("parallel", "arbitrary")),
        name="nsa_attn",
    )(p, aux, kcmp, vcmp, ovt, p, p, p, p, p)

    tm2 = 512
    row = lambda i: (i, 0)
    out = pl.pallas_call(
        functools.partial(_post_kernel, d_ff=d_ff, ff_chunk=256),
        grid=(N // tm2,),
        in_specs=[pl.BlockSpec((tm2, D), row),
                  pl.BlockSpec((tm2, 512), row), pl.BlockSpec((tm2, 512), row),
                  _resident((512, D)), _resident((512, D)), _resident((1, D)), _resident((1, D)),
                  _resident((D, d_ff)), _resident((D, d_ff)), _resident((d_ff, D)), _resident((1, D))],
        out_specs=pl.BlockSpec((tm2, D), row),
        out_shape=jax.ShapeDtypeStruct((N, D), F32),
        scratch_shapes=[pltpu.VMEM((tm2, d_ff), dt)],
        compiler_params=_params(("parallel",)),
        name="post",
    )(x.reshape(N, D), o_diff.reshape(N, 512), o_nsa.reshape(N, 512), w_out_d, w_out_n,
      attn_post_norm.reshape(1, D), ffn_pre_norm.reshape(1, D),
      w_gate.astype(dt), w_up.astype(dt), w_down.astype(dt), ffn_post_norm.reshape(1, D))
    return out.reshape(B, S, D)


def kernel(x, attn_pre_norm, w_in, lambda_q1, lambda_k1, lambda_q2, lambda_k2, diff_subln, k_cmp_pos, k_cmp_w1, k_cmp_w2, v_cmp_pos, v_cmp_w1, v_cmp_w2, w_out, attn_post_norm, ffn_pre_norm, w_gate, w_up, w_down, ffn_post_norm):
    for l in range(w_in.shape[0]):
        x = _layer(x, l, attn_pre_norm[l], w_in[l], lambda_q1[l], lambda_k1[l], lambda_q2[l], lambda_k2[l],
                   diff_subln[l], k_cmp_pos[l], k_cmp_w1[l], k_cmp_w2[l], v_cmp_pos[l], v_cmp_w1[l], v_cmp_w2[l],
                   w_out[l], attn_post_norm[l], ffn_pre_norm[l], w_gate[l], w_up[l], w_down[l], ffn_post_norm[l])
    return x
```

```python
import functools
import math

import numpy as np
import jax
import jax.numpy as jnp
from jax import lax
from jax.experimental import pallas as pl
from jax.experimental.pallas import tpu as pltpu

F32 = jnp.float32
MXU_DTYPE = jnp.bfloat16

LANES = 128
SUBLANES = 8
HEAD_DIM = 64
ROPE_THETA = 10000.0
NORM_EPS = 1e-6
NEG_INF = -1e30
LOG2_E = 1.4426950408889634
FORCED_IMPORTANCE = 3e38

DIFF_HEADS = 4
NSA_HEADS_PER_GROUP = 4
NSA_KV_GROUPS = 2
CMP_BLOCK = 32
CMP_STRIDE = 16
CMP_HIDDEN = 4 * HEAD_DIM
SLC_BLOCK = 64
SLC_TOPK = 16
WINDOW = 512
N_BRANCHES = 3

N_MAIN_SLABS = 20
N_AUX_SLABS = 3
N_P_SLABS = 21
VMEM_LIMIT = 56 * 1024 * 1024


def _nt_dot(a, b):
    return lax.dot_general(a, b, (((1,), (1,)), ((), ())), preferred_element_type=F32)


def _dot(a, b):
    return jnp.dot(a, b, preferred_element_type=F32)


def _rms(x, g):
    return x * lax.rsqrt(jnp.mean(x * x, axis=-1, keepdims=True) + NORM_EPS) * g


def _inproj_kernel(x_ref, g_ref, w_ref, cos_ref, sin_ref, p_ref, aux_ref, *, tm, seq_tiles):
    h = _rms(x_ref[...], g_ref[...]).astype(MXU_DTYPE)
    cos = cos_ref[...]
    sin = sin_ref[...]
    lane = lax.broadcasted_iota(jnp.int32, (tm, LANES), 1)
    low_half = (lane & (HEAD_DIM - 1)) < HEAD_DIM // 2

    def rope(y):
        fwd = pltpu.roll(y, HEAD_DIM // 2, 1)
        bwd = pltpu.roll(y, LANES - HEAD_DIM // 2, 1)
        return y * cos + jnp.where(low_half, bwd, fwd) * sin

    pos = (pl.program_id(0) % seq_tiles) * tm + lax.broadcasted_iota(jnp.int32, (tm, LANES), 0)
    blk = pos // SLC_BLOCK
    scale = HEAD_DIM ** -0.5 * LOG2_E

    n_slabs = N_MAIN_SLABS + N_AUX_SLABS
    for c0 in range(0, n_slabs, 2):
        c1 = min(c0 + 2, n_slabs)
        y2 = _dot(h, w_ref[:, c0 * LANES:c1 * LANES])
        for s in range(c0, c1):
            y = y2[:, (s - c0) * LANES:(s - c0 + 1) * LANES]
            if s < 4 or 12 <= s < 16:
                p_ref[:, s * LANES:(s + 1) * LANES] = (rope(y) * scale).astype(p_ref.dtype)
            elif s < 8:
                p_ref[:, s * LANES:(s + 1) * LANES] = rope(y).astype(p_ref.dtype)
            elif s < 12:
                p_ref[:, s * LANES:(s + 1) * LANES] = y.astype(p_ref.dtype)
            elif s == 16:
                r = rope(y)
                ind_hi = jnp.where(lane - HEAD_DIM == blk, 1.0, 0.0)
                ind_lo = jnp.where(lane == blk, 1.0, 0.0)
                p_ref[:, 16 * LANES:17 * LANES] = jnp.where(lane < HEAD_DIM, r, ind_hi).astype(p_ref.dtype)
                p_ref[:, 17 * LANES:18 * LANES] = jnp.where(lane >= HEAD_DIM, r, ind_lo).astype(p_ref.dtype)
            elif s == 17:
                p_ref[:, 18 * LANES:19 * LANES] = y.astype(p_ref.dtype)
            elif s == 18:
                p_ref[:, 19 * LANES:20 * LANES] = rope(y).astype(p_ref.dtype)
            elif s == 19:
                p_ref[:, 20 * LANES:21 * LANES] = y.astype(p_ref.dtype)
            elif s == 20:
                aux_ref[:, 0:LANES] = rope(y)
            elif s == 21:
                aux_ref[:, LANES:2 * LANES] = y
            else:
                aux_ref[:, 2 * LANES:3 * LANES] = jax.nn.sigmoid(y)


def _compress_kernel(tk_ref, tv_ref, kpl_ref, kph_ref, vpl_ref, vph_ref,
                     kwl_ref, kwh_ref, kw2_ref, vwl_ref, vwh_ref, vw2_ref, ko_ref, vo_ref, *, nc):
    def compress(t_ref, plo_ref, phi_ref, wlo_ref, whi_ref, w2_ref):
        x = jnp.concatenate([t_ref[pl.ds(l, nc, stride=CMP_STRIDE), :] for l in range(CMP_STRIDE)], axis=1)
        a = _dot((x + plo_ref[...]).astype(MXU_DTYPE), wlo_ref[...])
        b = _dot((x + phi_ref[...]).astype(MXU_DTYPE), whi_ref[...])
        hid = a + pltpu.roll(b, nc - 1, 0)
        act = hid * jax.nn.sigmoid(hid)
        return _dot(act.astype(MXU_DTYPE), w2_ref[...])

    ko_ref[...] = compress(tk_ref, kpl_ref, kph_ref, kwl_ref, kwh_ref, kw2_ref).astype(ko_ref.dtype)
    vo_ref[...] = compress(tv_ref, vpl_ref, vph_ref, vwl_ref, vwh_ref, vw2_ref).astype(vo_ref.dtype)


def _flash_init(m_ref, l_ref, acc_ref):
    m_ref[...] = jnp.full(m_ref.shape, NEG_INF, F32)
    l_ref[...] = jnp.zeros(l_ref.shape, F32)
    acc_ref[...] = jnp.zeros(acc_ref.shape, F32)


def _lane_tiles(x):
    return [x[:, c:c + LANES] for c in range(0, x.shape[1], LANES)]


def _flash_step(q, k, v, m_ref, l_ref, acc_ref, mask=None):
    s = _nt_dot(q, k)
    if mask is not None:
        s = jnp.where(mask, s, NEG_INF)
    n_tiles = s.shape[1] // LANES
    m_prev = m_ref[...]
    m_new = jnp.maximum(m_prev, jnp.max(s, axis=-1, keepdims=True))
    alpha = jnp.exp2(m_prev - m_new)
    p = jnp.exp2(s - jnp.concatenate([m_new] * n_tiles, axis=1))
    l_ref[...] = alpha * l_ref[...] + functools.reduce(lambda a, b: a + b, _lane_tiles(p))
    acc_ref[...] = alpha * acc_ref[...] + _dot(p.astype(MXU_DTYPE), v)
    m_ref[...] = m_new


def _flash_finish(l_ref, acc_ref):
    return acc_ref[...] / jnp.sum(l_ref[...], axis=-1, keepdims=True)


def _causal_flash(q, k_ref, v_ref, m_ref, l_ref, acc_ref, s0, tq, tk):
    rows = q.shape[0]
    _flash_init(m_ref, l_ref, acc_ref)

    def step(j):
        k0 = pl.multiple_of(j * tk, tk)
        _flash_step(q, k_ref[pl.ds(k0, tk), :], v_ref[pl.ds(k0, tk), :], m_ref, l_ref, acc_ref)

    def pair(jj, carry):
        step(2 * jj)
        step(2 * jj + 1)
        return carry

    n_full = s0 // tk
    lax.fori_loop(0, n_full // 2, pair, 0)

    @pl.when(n_full % 2 == 1)
    def _():
        step(n_full - 1)

    qpos = s0 + (lax.broadcasted_iota(jnp.int32, (rows, tk), 0) & (tq - 1))
    col = lax.broadcasted_iota(jnp.int32, (rows, tk), 1)
    for t in range(max(tq // tk, 1)):
        k0 = pl.multiple_of((n_full + t) * tk, tk)
        _flash_step(q, k_ref[pl.ds(k0, tk), :], v_ref[pl.ds(k0, tk), :], m_ref, l_ref, acc_ref,
                    mask=(k0 + col) <= qpos)
    return _flash_finish(l_ref, acc_ref)


def _diff_kernel(lq1_ref, lk1_ref, lq2_ref, lk2_ref, subln_ref, q_ref, k_ref, v_ref, o_ref,
                 m_ref, l_ref, acc_ref, *, tq, tk, lam_init):
    qi = pl.program_id(2)
    s0 = qi * tq
    lane = lax.broadcasted_iota(jnp.int32, (tq, LANES), 1)
    qf = q_ref[...].astype(F32)
    zero = jnp.zeros_like(qf)
    q2 = jnp.concatenate([jnp.where(lane < HEAD_DIM, qf, zero),
                          jnp.where(lane >= HEAD_DIM, qf, zero)], axis=0).astype(MXU_DTYPE)

    o12 = _causal_flash(q2, k_ref, v_ref, m_ref, l_ref, acc_ref, s0, tq, tk)
    lam = (jnp.exp(jnp.sum(lq1_ref[...] * lk1_ref[...], axis=-1, keepdims=True))
           - jnp.exp(jnp.sum(lq2_ref[...] * lk2_ref[...], axis=-1, keepdims=True)) + lam_init)
    o = o12[:tq] - lam * o12[tq:]
    o_ref[...] = (_rms(o, subln_ref[...]) * (1.0 - lam_init)).astype(o_ref.dtype)


def _nsa_kernel(q_ref, gate_ref, kc_ref, vc_ref, ovt_ref, ka0_ref, ka1_ref, vs_ref, kw_ref, vw_ref,
                o_ref, m_ref, l_ref, acc_ref, *, tq, tk, nc, n_sel, top_n):
    hg = NSA_HEADS_PER_GROUP
    rows = hg * tq
    qi = pl.program_id(1)
    s0 = qi * tq
    lane = lax.broadcasted_iota(jnp.int32, (tq, LANES), 1)
    qf = [q_ref[:, h * LANES:(h + 1) * LANES].astype(F32) for h in range(hg)]
    gates = gate_ref[...]
    zero = jnp.zeros((tq, LANES), F32)

    wk = WINDOW + tq
    w0 = pl.multiple_of(jnp.maximum(s0 - WINDOW, 0), tq)
    back = (s0 - w0) + (lax.broadcasted_iota(jnp.int32, (rows, wk), 0) & (tq - 1)) \
        - lax.broadcasted_iota(jnp.int32, (rows, wk), 1)
    win_mask = (back >= 0) & (back < WINDOW)

    outs = []
    for g in range(NSA_KV_GROUPS):
        in_half = (lane >= HEAD_DIM) if g else (lane < HEAD_DIM)
        q_plain = jnp.concatenate([jnp.where(in_half, qf[h], zero) for h in range(hg)],
                                  axis=0).astype(MXU_DTYPE)

        sc = _nt_dot(q_plain, kc_ref[...])
        n_idx = lax.broadcasted_iota(jnp.int32, (rows, nc), 1)
        qpos_c = s0 + (lax.broadcasted_iota(jnp.int32, (rows, nc), 0) & (tq - 1))
        cmask = n_idx * CMP_STRIDE + (CMP_BLOCK - 1) <= qpos_c
        sc = jnp.where(cmask, sc, NEG_INF)
        pc = jnp.where(cmask, jnp.exp2(sc - jnp.max(sc, axis=-1, keepdims=True)), 0.0)
        lc = jnp.sum(pc, axis=-1, keepdims=True)
        pc = (pc / jnp.where(lc > 0.0, lc, 1.0)).astype(MXU_DTYPE)
        o_cmp = _dot(pc, vc_ref[...])

        imp_t = None
        for h in range(hg):
            part = _nt_dot(ovt_ref[...], pc[h * tq:(h + 1) * tq])
            imp_t = part if imp_t is None else imp_t + part
        imp_t = imp_t[:SLC_BLOCK]
        jb = lax.broadcasted_iota(jnp.int32, (SLC_BLOCK, tq), 0)
        qp = s0 + lax.broadcasted_iota(jnp.int32, (SLC_BLOCK, tq), 1)
        cur = qp // SLC_BLOCK
        valid = jb * SLC_BLOCK <= qp
        forced = (jb == 0) | (jb == cur) | (jb == cur - 1)
        key = jnp.where(forced, FORCED_IMPORTANCE, jnp.where(valid, imp_t, -1.0))
        n_grp = SLC_BLOCK // SUBLANES
        key_g = [key[a * SUBLANES:(a + 1) * SUBLANES] for a in range(n_grp)]
        jb_g = lax.broadcasted_iota(jnp.int32, (SUBLANES, tq), 0)
        rank_g = [jnp.zeros((SUBLANES, tq), jnp.int32) for _ in range(n_grp)]
        for i in range(n_sel):
            ri = key[i:i + 1, :]
            for a in range(n_grp):
                if a < i // SUBLANES:
                    ahead = jnp.where(ri > key_g[a], 1, 0)
                elif a > i // SUBLANES:
                    ahead = jnp.where(ri >= key_g[a], 1, 0)
                else:
                    ahead = (jnp.where(ri > key_g[a], 1, 0)
                             + jnp.where((ri == key_g[a]) & (jb_g > i % SUBLANES), 1, 0))
                rank_g[a] = rank_g[a] + ahead
        selected = (jnp.concatenate(rank_g, axis=0) < top_n) & valid
        bias_t = jnp.where(selected, 0.0, NEG_INF)
        pad_t = jnp.zeros((SLC_BLOCK, tq), F32)
        bias = jnp.transpose(jnp.concatenate([pad_t, bias_t] if g == 0 else [bias_t, pad_t], axis=0))
        q_sel = jnp.concatenate([jnp.where(in_half, qf[h], bias) for h in range(hg)],
                                axis=0).astype(MXU_DTYPE)

        o_slc = _causal_flash(q_sel, ka1_ref if g else ka0_ref, vs_ref, m_ref, l_ref, acc_ref, s0, tq, tk)

        sw = _nt_dot(q_plain, kw_ref[pl.ds(w0, wk), :])
        sw = jnp.where(win_mask, sw, NEG_INF)
        pw = jnp.exp2(sw - jnp.max(sw, axis=-1, keepdims=True))
        o_win = _dot(pw.astype(MXU_DTYPE), vw_ref[pl.ds(w0, wk), :]) / jnp.sum(pw, axis=-1, keepdims=True)

        per_head = []
        for h in range(hg):
            c = (g * hg + h) * N_BRANCHES
            r = slice(h * tq, (h + 1) * tq)
            per_head.append(gates[:, c:c + 1] * o_cmp[r] + gates[:, c + 1:c + 2] * o_slc[r]
                            + gates[:, c + 2:c + 3] * o_win[r])
        outs.append(per_head)

    for h in range(hg):
        o_ref[:, h * LANES:(h + 1) * LANES] = jnp.where(lane < HEAD_DIM, outs[0][h], outs[1][h]).astype(o_ref.dtype)


def _post_kernel(x_ref, od_ref, on_ref, wod_ref, won_ref, gpost_ref, gpre_ref, wg_ref, wu_ref, wd_ref,
                 gffn_ref, o_ref, act_ref, *, d_ff, ff_chunk):
    mix = _dot(od_ref[...], wod_ref[...]) + _dot(on_ref[...], won_ref[...])
    x1 = x_ref[...] + _rms(mix, gpost_ref[...])
    h = _rms(x1, gpre_ref[...]).astype(MXU_DTYPE)
    for c in range(0, d_ff, ff_chunk):
        gate = _dot(h, wg_ref[:, c:c + ff_chunk])
        up = _dot(h, wu_ref[:, c:c + ff_chunk])
        act_ref[:, c:c + ff_chunk] = (gate * jax.nn.sigmoid(gate) * up).astype(act_ref.dtype)
    f = _dot(act_ref[...], wd_ref[...])
    o_ref[...] = x1 + _rms(f, gffn_ref[...])


def _resident(shape):
    nd = len(shape)
    return pl.BlockSpec(shape, lambda *_: (0,) * nd, pipeline_mode=pl.Buffered(1))


def _params(sem):
    return pltpu.CompilerParams(dimension_semantics=sem, vmem_limit_bytes=VMEM_LIMIT)


def _rope_tables(S):
    inv = 1.0 / (ROPE_THETA ** (jnp.arange(0, HEAD_DIM, 2, dtype=F32) / HEAD_DIM))
    ang = jnp.arange(S, dtype=F32)[:, None] * inv[None, :]
    cos, sin = jnp.cos(ang), jnp.sin(ang)
    return jnp.tile(cos, (1, 4)), jnp.concatenate([-sin, sin, -sin, sin], axis=1)


def _selection_overlap_t(nc, n_cmp, n_sel):
    c0 = np.arange(n_cmp)[:, None] * CMP_STRIDE
    b0 = np.arange(n_sel)[None, :] * SLC_BLOCK
    ov = np.clip(np.minimum(c0 + CMP_BLOCK, b0 + SLC_BLOCK) - np.maximum(c0, b0), 0, None) / CMP_BLOCK
    full = np.zeros((LANES, nc), np.float32)
    full[:n_sel, :n_cmp] = ov.T
    return full


def _layer(x, layer, attn_pre_norm, w_in, lq1, lk1, lq2, lk2, diff_subln, k_pos, k_w1, k_w2,
           v_pos, v_w1, v_w2, w_out, attn_post_norm, ffn_pre_norm, w_gate, w_up, w_down, ffn_post_norm):
    B, S, D = x.shape
    N = B * S
    d_ff = w_gate.shape[1]
    nc = S // CMP_STRIDE
    n_cmp = (S - CMP_BLOCK) // CMP_STRIDE + 1
    n_sel = S // SLC_BLOCK
    top_n = min(SLC_TOPK, n_sel)
    assert n_sel <= SLC_BLOCK and S % 512 == 0
    lam_init = 0.8 - 0.6 * math.exp(-0.3 * layer)
    dt = MXU_DTYPE

    hg, G, d = NSA_HEADS_PER_GROUP, NSA_KV_GROUPS, HEAD_DIM
    nq_perm = np.array([1536 + (g * hg + h) * d + e for h in range(hg) for g in range(G) for e in range(d)])
    cols = np.concatenate([np.arange(0, 1536), nq_perm,
                           np.arange(2304, 2816),
                           np.arange(2048, 2304),
                           np.arange(2816, 2840)])
    w_cat = jnp.pad(w_in[:, cols], ((0, 0), (0, LANES - hg * G * N_BRANCHES))).astype(dt)
    w_out_d = w_out[:512].astype(dt)
    w_out_n = w_out[nq_perm - 1536 + 512].astype(dt)
    cos_t, sin_t = _rope_tables(S)

    tm = 512
    seq_tiles = S // tm
    n_cols = (N_MAIN_SLABS + N_AUX_SLABS) * LANES
    p, aux = pl.pallas_call(
        functools.partial(_inproj_kernel, tm=tm, seq_tiles=seq_tiles),
        grid=(N // tm,),
        in_specs=[pl.BlockSpec((tm, D), lambda i: (i, 0)),
                  _resident((1, D)),
                  _resident((D, n_cols)),
                  pl.BlockSpec((tm, LANES), lambda i: (i % seq_tiles, 0)),
                  pl.BlockSpec((tm, LANES), lambda i: (i % seq_tiles, 0))],
        out_specs=[pl.BlockSpec((tm, N_P_SLABS * LANES), lambda i: (i, 0)),
                   pl.BlockSpec((tm, N_AUX_SLABS * LANES), lambda i: (i, 0))],
        out_shape=[jax.ShapeDtypeStruct((N, N_P_SLABS * LANES), dt),
                   jax.ShapeDtypeStruct((N, N_AUX_SLABS * LANES), F32)],
        compiler_params=_params(("parallel",)),
        name="inproj",
    )(x.reshape(N, D), attn_pre_norm.reshape(1, D), w_cat, cos_t, sin_t)
    p = p.reshape(B, S, N_P_SLABS * LANES)
    aux = aux.reshape(B, S, N_AUX_SLABS * LANES)

    eye_g = jnp.eye(G, dtype=F32)

    def pos_rows(pos):
        tiled = jnp.broadcast_to(pos.reshape(2, CMP_STRIDE, 1, d), (2, CMP_STRIDE, G, d))
        return tiled[0].reshape(1, -1), tiled[1].reshape(1, -1)

    def w1_blocks(w1):
        w = w1.reshape(2, CMP_STRIDE, d, CMP_HIDDEN)
        blk = jnp.einsum('pldj,gh->plgdhj', w, eye_g).reshape(2, CMP_STRIDE * G * d, G * CMP_HIDDEN)
        return blk[0].astype(dt), blk[1].astype(dt)

    def w2_blocks(w2):
        return jnp.einsum('jd,gh->gjhd', w2, eye_g).reshape(G * CMP_HIDDEN, G * d).astype(dt)

    kpl, kph = pos_rows(k_pos)
    vpl, vph = pos_rows(v_pos)
    kwl, kwh = w1_blocks(k_w1)
    vwl, vwh = w1_blocks(v_w1)
    chunk_w = CMP_STRIDE * G * d
    w1_spec = _resident((chunk_w, G * CMP_HIDDEN))
    w2_spec = _resident((G * CMP_HIDDEN, LANES))
    kcmp, vcmp = pl.pallas_call(
        functools.partial(_compress_kernel, nc=nc),
        grid=(B,),
        in_specs=[pl.BlockSpec((None, S, LANES), lambda b: (b, 0, 0)),
                  pl.BlockSpec((None, S, LANES), lambda b: (b, 0, 1)),
                  _resident((1, chunk_w)), _resident((1, chunk_w)), _resident((1, chunk_w)), _resident((1, chunk_w)),
                  w1_spec, w1_spec, w2_spec, w1_spec, w1_spec, w2_spec],
        out_specs=[pl.BlockSpec((None, nc, LANES), lambda b: (b, 0, 0)),
                   pl.BlockSpec((None, nc, LANES), lambda b: (b, 0, 0))],
        out_shape=[jax.ShapeDtypeStruct((B, nc, LANES), dt), jax.ShapeDtypeStruct((B, nc, LANES), dt)],
        compiler_params=_params(("parallel",)),
        name="compress",
    )(aux, aux, kpl, kph, vpl, vph, kwl, kwh, w2_blocks(k_w2), vwl, vwh, w2_blocks(v_w2))

    tq_d, tk_d = 512, 512
    lam_vec = [v.reshape(1, d) for v in (lq1, lk1, lq2, lk2)]
    o_diff = pl.pallas_call(
        functools.partial(_diff_kernel, tq=tq_d, tk=tk_d, lam_init=lam_init),
        grid=(B, DIFF_HEADS, S // tq_d),
        in_specs=[_resident((1, d))] * 4 + [_resident((1, LANES)),
                  pl.BlockSpec((None, tq_d, LANES), lambda b, h, i: (b, i, h)),
                  pl.BlockSpec((None, S, LANES), lambda b, h, i: (b, 0, 4 + h)),
                  pl.BlockSpec((None, S, LANES), lambda b, h, i: (b, 0, 8 + h))],
        out_specs=pl.BlockSpec((None, tq_d, LANES), lambda b, h, i: (b, i, h)),
        out_shape=jax.ShapeDtypeStruct((B, S, DIFF_HEADS * LANES), dt),
        scratch_shapes=[pltpu.VMEM((2 * tq_d, LANES), F32)] * 3,
        compiler_params=_params(("parallel", "parallel", "arbitrary")),
        name="diff_attn",
    )(*lam_vec, diff_subln.reshape(1, LANES), p, p, p)

    tq_n, tk_n = 256, 512
    ovt = jnp.asarray(_selection_overlap_t(nc, n_cmp, n_sel)).astype(dt)
    rows = NSA_HEADS_PER_GROUP * tq_n

    def seq_slab(c):
        return pl.BlockSpec((None, S, LANES), lambda b, i: (b, 0, c))

    o_nsa = pl.pallas_call(
        functools.partial(_nsa_kernel, tq=tq_n, tk=tk_n, nc=nc, n_sel=n_sel, top_n=top_n),
        grid=(B, S // tq_n),
        in_specs=[pl.BlockSpec((None, tq_n, 4 * LANES), lambda b, i: (b, i, 3)),
                  pl.BlockSpec((None, tq_n, LANES), lambda b, i: (b, i, 2)),
                  pl.BlockSpec((None, nc, LANES), lambda b, i: (b, 0, 0)),
                  pl.BlockSpec((None, nc, LANES), lambda b, i: (b, 0, 0)),
                  _resident((LANES, nc)),
                  seq_slab(16), seq_slab(17), seq_slab(18), seq_slab(19), seq_slab(20)],
        out_specs=pl.BlockSpec((None, tq_n, 4 * LANES), lambda b, i: (b, i, 0)),
        out_shape=jax.ShapeDtypeStruct((B, S, 4 * LANES), dt),
        scratch_shapes=[pltpu.VMEM((rows, LANES), F32)] * 3,
        compiler_params=_params(("parallel", "arbitrary")),
        name="nsa_attn",
    )(p, aux, kcmp, vcmp, ovt, p, p, p, p, p)

    tm2 = 512
    row = lambda i: (i, 0)
    out = pl.pallas_call(
        functools.partial(_post_kernel, d_ff=d_ff, ff_chunk=256),
        grid=(N // tm2,),
        in_specs=[pl.BlockSpec((tm2, D), row),
                  pl.BlockSpec((tm2, 512), row), pl.BlockSpec((tm2, 512), row),
                  _resident((512, D)), _resident((512, D)), _resident((1, D)), _resident((1, D)),
                  _resident((D, d_ff)), _resident((D, d_ff)), _resident((d_ff, D)), _resident((1, D))],
        out_specs=pl.BlockSpec((tm2, D), row),
        out_shape=jax.ShapeDtypeStruct((N, D), F32),
        scratch_shapes=[pltpu.VMEM((tm2, d_ff), dt)],
        compiler_params=_params(("parallel",)),
        name="post",
    )(x.reshape(N, D), o_diff.reshape(N, 512), o_nsa.reshape(N, 512), w_out_d, w_out_n,
      attn_post_norm.reshape(1, D), ffn_pre_norm.reshape(1, D),
      w_gate.astype(dt), w_up.astype(dt), w_down.astype(dt), ffn_post_norm.reshape(1, D))
    return out.reshape(B, S, D)


def kernel(x, attn_pre_norm, w_in, lambda_q1, lambda_k1, lambda_q2, lambda_k2, diff_subln, k_cmp_pos, k_cmp_w1, k_cmp_w2, v_cmp_pos, v_cmp_w1, v_cmp_w2, w_out, attn_post_norm, ffn_pre_norm, w_gate, w_up, w_down, ffn_post_norm):
    for l in range(w_in.shape[0]):
        x = _layer(x, l, attn_pre_norm[l], w_in[l], lambda_q1[l], lambda_k1[l], lambda_q2[l], lambda_k2[l],
                   diff_subln[l], k_cmp_pos[l], k_cmp_w1[l], k_cmp_w2[l], v_cmp_pos[l], v_cmp_w1[l], v_cmp_w2[l],
                   w_out[l], attn_post_norm[l], ffn_pre_norm[l], w_gate[l], w_up[l], w_down[l], ffn_post_norm[l])
    return x
```

```python
import functools
import math

import numpy as np
import jax
import jax.numpy as jnp
from jax import lax
from jax.experimental import pallas as pl
from jax.experimental.pallas import tpu as pltpu

F32 = jnp.float32
MXU_DTYPE = jnp.bfloat16

LANES = 128
SUBLANES = 8
HEAD_DIM = 64
ROPE_THETA = 10000.0
NORM_EPS = 1e-6
NEG_INF = -1e30
LOG2_E = 1.4426950408889634
FORCED_IMPORTANCE = 3e38

DIFF_HEADS = 4
NSA_HEADS_PER_GROUP = 4
NSA_KV_GROUPS = 2
CMP_BLOCK = 32
CMP_STRIDE = 16
CMP_HIDDEN = 4 * HEAD_DIM
SLC_BLOCK = 64
SLC_TOPK = 16
WINDOW = 512
N_BRANCHES = 3

N_MAIN_SLABS = 20
N_AUX_SLABS = 3
N_P_SLABS = 23
VMEM_LIMIT = 56 * 1024 * 1024


def _nt_dot(a, b):
    return lax.dot_general(a, b, (((1,), (1,)), ((), ())), preferred_element_type=F32)


def _dot(a, b):
    return jnp.dot(a, b, preferred_element_type=F32)


def _rms(x, g):
    return x * lax.rsqrt(jnp.mean(x * x, axis=-1, keepdims=True) + NORM_EPS) * g


def _inproj_kernel(x_ref, g_ref, w_ref, cos_ref, sin_ref, p_ref, aux_ref, *, tm, seq_tiles):
    h = _rms(x_ref[...], g_ref[...]).astype(MXU_DTYPE)
    cos = cos_ref[...]
    sin = sin_ref[...]
    lane = lax.broadcasted_iota(jnp.int32, (tm, LANES), 1)
    low_half = (lane & (HEAD_DIM - 1)) < HEAD_DIM // 2

    def rope(y):
        fwd = pltpu.roll(y, HEAD_DIM // 2, 1)
        bwd = pltpu.roll(y, LANES - HEAD_DIM // 2, 1)
        return y * cos + jnp.where(low_half, bwd, fwd) * sin

    pos = (pl.program_id(0) % seq_tiles) * tm + lax.broadcasted_iota(jnp.int32, (tm, LANES), 0)
    blk = pos // SLC_BLOCK
    scale = HEAD_DIM ** -0.5 * LOG2_E

    n_slabs = N_MAIN_SLABS + N_AUX_SLABS
    for c0 in range(0, n_slabs, 2):
        c1 = min(c0 + 2, n_slabs)
        y2 = _dot(h, w_ref[:, c0 * LANES:c1 * LANES])
        for s in range(c0, c1):
            y = y2[:, (s - c0) * LANES:(s - c0 + 1) * LANES]
            if s < 4 or 12 <= s < 16:
                p_ref[:, s * LANES:(s + 1) * LANES] = (rope(y) * scale).astype(p_ref.dtype)
            elif s < 8:
                p_ref[:, s * LANES:(s + 1) * LANES] = rope(y).astype(p_ref.dtype)
            elif s < 12:
                p_ref[:, s * LANES:(s + 1) * LANES] = y.astype(p_ref.dtype)
            elif s == 16:
                r = rope(y)
                ind_hi = jnp.where(lane - HEAD_DIM == blk, 1.0, 0.0)
                ind_lo = jnp.where(lane == blk, 1.0, 0.0)
                p_ref[:, 16 * LANES:17 * LANES] = jnp.where(lane < HEAD_DIM, r, ind_hi).astype(p_ref.dtype)
                p_ref[:, 17 * LANES:18 * LANES] = jnp.where(lane >= HEAD_DIM, r, ind_lo).astype(p_ref.dtype)
            elif s == 17 or s == 19:
                o = 18 if s == 17 else 21
                p_ref[:, o * LANES:(o + 1) * LANES] = jnp.where(lane < HEAD_DIM, y, 1.0).astype(p_ref.dtype)
                p_ref[:, (o + 1) * LANES:(o + 2) * LANES] = jnp.where(lane >= HEAD_DIM, y, 1.0).astype(p_ref.dtype)
            elif s == 18:
                p_ref[:, 20 * LANES:21 * LANES] = rope(y).astype(p_ref.dtype)
            elif s == 20:
                aux_ref[:, 0:LANES] = rope(y)
            elif s == 21:
                aux_ref[:, LANES:2 * LANES] = y
            else:
                aux_ref[:, 2 * LANES:3 * LANES] = jax.nn.sigmoid(y)


def _compress_kernel(tk_ref, tv_ref, kpl_ref, kph_ref, vpl_ref, vph_ref,
                     kwl_ref, kwh_ref, kw2_ref, vwl_ref, vwh_ref, vw2_ref, ko_ref, vo_ref, *, nc):
    def compress(t_ref, plo_ref, phi_ref, wlo_ref, whi_ref, w2_ref):
        x = jnp.concatenate([t_ref[pl.ds(l, nc, stride=CMP_STRIDE), :] for l in range(CMP_STRIDE)], axis=1)
        a = _dot((x + plo_ref[...]).astype(MXU_DTYPE), wlo_ref[...])
        b = _dot((x + phi_ref[...]).astype(MXU_DTYPE), whi_ref[...])
        hid = a + pltpu.roll(b, nc - 1, 0)
        act = hid * jax.nn.sigmoid(hid)
        return _dot(act.astype(MXU_DTYPE), w2_ref[...])

    ko_ref[...] = compress(tk_ref, kpl_ref, kph_ref, kwl_ref, kwh_ref, kw2_ref).astype(ko_ref.dtype)
    vo_ref[...] = compress(tv_ref, vpl_ref, vph_ref, vwl_ref, vwh_ref, vw2_ref).astype(vo_ref.dtype)


def _flash_init(m_ref, l_ref, acc_ref):
    m_ref[...] = jnp.full(m_ref.shape, NEG_INF, F32)
    if l_ref is not None:
        l_ref[...] = jnp.zeros(l_ref.shape, F32)
    acc_ref[...] = jnp.zeros(acc_ref.shape, F32)


def _lane_tiles(x):
    return [x[:, c:c + LANES] for c in range(0, x.shape[1], LANES)]


def _softmax_pv(s, v, m_ref, l_ref, acc_ref, mask=None):
    if mask is not None:
        s = jnp.where(mask, s, NEG_INF)
    n_tiles = s.shape[1] // LANES
    m_prev = m_ref[...]
    m_new = jnp.maximum(m_prev, jnp.max(s, axis=-1, keepdims=True))
    alpha = jnp.exp2(m_prev - m_new)
    x = s - jnp.concatenate([m_new] * n_tiles, axis=1)
    if l_ref is None:
        p = jnp.exp2(x.astype(MXU_DTYPE))
    else:
        p = jnp.exp2(x)
        l_ref[...] = alpha * l_ref[...] + functools.reduce(lambda a, b: a + b, _lane_tiles(p))
        p = p.astype(MXU_DTYPE)
    acc_ref[...] = alpha * acc_ref[...] + _dot(p, v)
    m_ref[...] = m_new


def _flash_finish(l_ref, acc_ref):
    if l_ref is None:
        return acc_ref[...]
    return acc_ref[...] / jnp.sum(l_ref[...], axis=-1, keepdims=True)


def _causal_flash(q, k_ref, v_ref, m_ref, l_ref, acc_ref, s0, tq, tk):
    assert tq <= tk and tk % tq == 0
    rows = q.shape[0]
    _flash_init(m_ref, l_ref, acc_ref)

    def step(j, mask=None):
        k0 = pl.multiple_of(j * tk, tk)
        _softmax_pv(_nt_dot(q, k_ref[pl.ds(k0, tk), :]), v_ref[pl.ds(k0, tk), :], m_ref, l_ref, acc_ref, mask)

    def pair(jj, carry):
        step(2 * jj)
        step(2 * jj + 1)
        return carry

    n_full = s0 // tk
    lax.fori_loop(0, n_full // 2, pair, 0)

    @pl.when(n_full % 2 == 1)
    def _():
        step(n_full - 1)

    qpos = s0 + (lax.broadcasted_iota(jnp.int32, (rows, tk), 0) & (tq - 1))
    step(n_full, (n_full * tk + lax.broadcasted_iota(jnp.int32, (rows, tk), 1)) <= qpos)
    return _flash_finish(l_ref, acc_ref)


def _diff_kernel(lq1_ref, lk1_ref, lq2_ref, lk2_ref, subln_ref, q_ref, k_ref, v_ref, o_ref,
                 m_ref, l_ref, acc_ref, *, tq, tk, lam_init):
    qi = pl.program_id(2)
    s0 = qi * tq
    lane = lax.broadcasted_iota(jnp.int32, (tq, LANES), 1)
    qf = q_ref[...].astype(F32)
    zero = jnp.zeros_like(qf)
    q2 = jnp.concatenate([jnp.where(lane < HEAD_DIM, qf, zero),
                          jnp.where(lane >= HEAD_DIM, qf, zero)], axis=0).astype(MXU_DTYPE)

    o12 = _causal_flash(q2, k_ref, v_ref, m_ref, l_ref, acc_ref, s0, tq, tk)
    lam = (jnp.exp(jnp.sum(lq1_ref[...] * lk1_ref[...], axis=-1, keepdims=True))
           - jnp.exp(jnp.sum(lq2_ref[...] * lk2_ref[...], axis=-1, keepdims=True)) + lam_init)
    o = o12[:tq] - lam * o12[tq:]
    o_ref[...] = (_rms(o, subln_ref[...]) * (1.0 - lam_init)).astype(o_ref.dtype)


def _nsa_kernel(q_ref, gate_ref, kc_ref, vc_ref, ovt_ref, ka0_ref, ka1_ref, vs0_ref, vs1_ref,
                kw_ref, vw0_ref, vw1_ref, o_ref, m_ref, acc_ref, *, tq, tk, nc, n_sel, top_n):
    hg = NSA_HEADS_PER_GROUP
    rows = hg * tq
    qi = pl.program_id(1)
    s0 = qi * tq
    lane = lax.broadcasted_iota(jnp.int32, (tq, LANES), 1)
    lane_rows = lax.broadcasted_iota(jnp.int32, (rows, LANES), 1)
    qf = [q_ref[:, h * LANES:(h + 1) * LANES].astype(F32) for h in range(hg)]
    gates = gate_ref[...]
    zero = jnp.zeros((tq, LANES), F32)

    wk = WINDOW + tq
    w0 = pl.multiple_of(jnp.maximum(s0 - WINDOW, 0), tq)
    back = (s0 - w0) + (lax.broadcasted_iota(jnp.int32, (rows, wk), 0) & (tq - 1)) \
        - lax.broadcasted_iota(jnp.int32, (rows, wk), 1)
    win_mask = (back >= 0) & (back < WINDOW)

    outs = []
    for g in range(NSA_KV_GROUPS):
        in_half = (lane >= HEAD_DIM) if g else (lane < HEAD_DIM)
        q_plain = jnp.concatenate([jnp.where(in_half, qf[h], zero) for h in range(hg)],
                                  axis=0).astype(MXU_DTYPE)

        sc = _nt_dot(q_plain, kc_ref[...])
        n_idx = lax.broadcasted_iota(jnp.int32, (rows, nc), 1)
        qpos_c = s0 + (lax.broadcasted_iota(jnp.int32, (rows, nc), 0) & (tq - 1))
        cmask = n_idx * CMP_STRIDE + (CMP_BLOCK - 1) <= qpos_c
        sc = jnp.where(cmask, sc, NEG_INF)
        pc = jnp.where(cmask, jnp.exp2(sc - jnp.max(sc, axis=-1, keepdims=True)), 0.0)
        lc = jnp.sum(pc, axis=-1, keepdims=True)
        pc = (pc / jnp.where(lc > 0.0, lc, 1.0)).astype(MXU_DTYPE)
        o_cmp = _dot(pc, vc_ref[...])

        imp_t = None
        for h in range(hg):
            part = _nt_dot(ovt_ref[...], pc[h * tq:(h + 1) * tq])
            imp_t = part if imp_t is None else imp_t + part
        imp_t = imp_t[:SLC_BLOCK]
        jb = lax.broadcasted_iota(jnp.int32, (SLC_BLOCK, tq), 0)
        qp = s0 + lax.broadcasted_iota(jnp.int32, (SLC_BLOCK, tq), 1)
        cur = qp // SLC_BLOCK
        valid = jb * SLC_BLOCK <= qp
        forced = (jb == 0) | (jb == cur) | (jb == cur - 1)
        key = jnp.where(forced, FORCED_IMPORTANCE, jnp.where(valid, imp_t, -1.0))
        n_grp = SLC_BLOCK // SUBLANES
        key_g = [key[a * SUBLANES:(a + 1) * SUBLANES] for a in range(n_grp)]
        jb_g = lax.broadcasted_iota(jnp.int32, (SUBLANES, tq), 0)
        rank_g = [jnp.zeros((SUBLANES, tq), jnp.int32) for _ in range(n_grp)]
        for i in range(n_sel):
            ri = key[i:i + 1, :]
            for a in range(n_grp):
                if a < i // SUBLANES:
                    ahead = jnp.where(ri > key_g[a], 1, 0)
                elif a > i // SUBLANES:
                    ahead = jnp.where(ri >= key_g[a], 1, 0)
                else:
                    ahead = (jnp.where(ri > key_g[a], 1, 0)
                             + jnp.where((ri == key_g[a]) & (jb_g > i % SUBLANES), 1, 0))
                rank_g[a] = rank_g[a] + ahead
        selected = (jnp.concatenate(rank_g, axis=0) < top_n) & valid
        bias_t = jnp.where(selected, 0.0, NEG_INF)
        pad_t = jnp.zeros((SLC_BLOCK, tq), F32)
        bias = jnp.transpose(jnp.concatenate([pad_t, bias_t] if g == 0 else [bias_t, pad_t], axis=0))
        q_sel = jnp.concatenate([jnp.where(in_half, qf[h], bias) for h in range(hg)],
                                axis=0).astype(MXU_DTYPE)

        in_half_rows = (lane_rows >= HEAD_DIM) if g else (lane_rows < HEAD_DIM)

        def normalise(raw):
            den = pltpu.roll(raw, HEAD_DIM, 1)
            return raw / jnp.where(in_half_rows, den, 1.0)

        o_slc = normalise(_causal_flash(q_sel, ka1_ref if g else ka0_ref, vs1_ref if g else vs0_ref,
                                        m_ref, None, acc_ref, s0, tq, tk))

        sw = _nt_dot(q_plain, kw_ref[pl.ds(w0, wk), :])
        sw = jnp.where(win_mask, sw, NEG_INF)
        pw = jnp.exp2((sw - jnp.max(sw, axis=-1, keepdims=True)).astype(MXU_DTYPE))
        o_win = normalise(_dot(pw, (vw1_ref if g else vw0_ref)[pl.ds(w0, wk), :]))

        per_head = []
        for h in range(hg):
            c = (g * hg + h) * N_BRANCHES
            r = slice(h * tq, (h + 1) * tq)
            per_head.append(gates[:, c:c + 1] * o_cmp[r] + gates[:, c + 1:c + 2] * o_slc[r]
                            + gates[:, c + 2:c + 3] * o_win[r])
        outs.append(per_head)

    for h in range(hg):
        o_ref[:, h * LANES:(h + 1) * LANES] = jnp.where(lane < HEAD_DIM, outs[0][h], outs[1][h]).astype(o_ref.dtype)


def _post_kernel(x_ref, od_ref, on_ref, wod_ref, won_ref, gpost_ref, gpre_ref, wg_ref, wu_ref, wd_ref,
                 gffn_ref, o_ref, act_ref, *, d_ff, ff_chunk):
    mix = _dot(od_ref[...], wod_ref[...]) + _dot(on_ref[...], won_ref[...])
    x1 = x_ref[...] + _rms(mix, gpost_ref[...])
    h = _rms(x1, gpre_ref[...]).astype(MXU_DTYPE)
    for c in range(0, d_ff, ff_chunk):
        gate = _dot(h, wg_ref[:, c:c + ff_chunk])
        up = _dot(h, wu_ref[:, c:c + ff_chunk])
        act_ref[:, c:c + ff_chunk] = (gate * jax.nn.sigmoid(gate) * up).astype(act_ref.dtype)
    f = _dot(act_ref[...], wd_ref[...])
    o_ref[...] = x1 + _rms(f, gffn_ref[...])


def _resident(shape):
    nd = len(shape)
    return pl.BlockSpec(shape, lambda *_: (0,) * nd, pipeline_mode=pl.Buffered(1))


def _params(sem):
    return pltpu.CompilerParams(dimension_semantics=sem, vmem_limit_bytes=VMEM_LIMIT)


def _rope_tables(S):
    inv = 1.0 / (ROPE_THETA ** (jnp.arange(0, HEAD_DIM, 2, dtype=F32) / HEAD_DIM))
    ang = jnp.arange(S, dtype=F32)[:, None] * inv[None, :]
    cos, sin = jnp.cos(ang), jnp.sin(ang)
    return jnp.tile(cos, (1, 4)), jnp.concatenate([-sin, sin, -sin, sin], axis=1)


def _selection_overlap_t(nc, n_cmp, n_sel):
    c0 = np.arange(n_cmp)[:, None] * CMP_STRIDE
    b0 = np.arange(n_sel)[None, :] * SLC_BLOCK
    ov = np.clip(np.minimum(c0 + CMP_BLOCK, b0 + SLC_BLOCK) - np.maximum(c0, b0), 0, None) / CMP_BLOCK
    full = np.zeros((LANES, nc), np.float32)
    full[:n_sel, :n_cmp] = ov.T
    return full


def _layer(x, layer, attn_pre_norm, w_in, lq1, lk1, lq2, lk2, diff_subln, k_pos, k_w1, k_w2,
           v_pos, v_w1, v_w2, w_out, attn_post_norm, ffn_pre_norm, w_gate, w_up, w_down, ffn_post_norm):
    B, S, D = x.shape
    N = B * S
    d_ff = w_gate.shape[1]
    nc = S // CMP_STRIDE
    n_cmp = (S - CMP_BLOCK) // CMP_STRIDE + 1
    n_sel = S // SLC_BLOCK
    top_n = min(SLC_TOPK, n_sel)
    assert n_sel <= SLC_BLOCK and S % 512 == 0
    lam_init = 0.8 - 0.6 * math.exp(-0.3 * layer)
    dt = MXU_DTYPE

    hg, G, d = NSA_HEADS_PER_GROUP, NSA_KV_GROUPS, HEAD_DIM
    nq_perm = np.array([1536 + (g * hg + h) * d + e for h in range(hg) for g in range(G) for e in range(d)])
    cols = np.concatenate([np.arange(0, 1536), nq_perm,
                           np.arange(2304, 2816),
                           np.arange(2048, 2304),
                           np.arange(2816, 2840)])
    w_cat = jnp.pad(w_in[:, cols], ((0, 0), (0, LANES - hg * G * N_BRANCHES))).astype(dt)
    w_out_d = w_out[:512].astype(dt)
    w_out_n = w_out[nq_perm - 1536 + 512].astype(dt)
    cos_t, sin_t = _rope_tables(S)

    tm = 512
    seq_tiles = S // tm
    n_cols = (N_MAIN_SLABS + N_AUX_SLABS) * LANES
    p, aux = pl.pallas_call(
        functools.partial(_inproj_kernel, tm=tm, seq_tiles=seq_tiles),
        grid=(N // tm,),
        in_specs=[pl.BlockSpec((tm, D), lambda i: (i, 0)),
                  _resident((1, D)),
                  _resident((D, n_cols)),
                  pl.BlockSpec((tm, LANES), lambda i: (i % seq_tiles, 0)),
                  pl.BlockSpec((tm, LANES), lambda i: (i % seq_tiles, 0))],
        out_specs=[pl.BlockSpec((tm, N_P_SLABS * LANES), lambda i: (i, 0)),
                   pl.BlockSpec((tm, N_AUX_SLABS * LANES), lambda i: (i, 0))],
        out_shape=[jax.ShapeDtypeStruct((N, N_P_SLABS * LANES), dt),
                   jax.ShapeDtypeStruct((N, N_AUX_SLABS * LANES), F32)],
        compiler_params=_params(("parallel",)),
        name="inproj",
    )(x.reshape(N, D), attn_pre_norm.reshape(1, D), w_cat, cos_t, sin_t)
    p = p.reshape(B, S, N_P_SLABS * LANES)
    aux = aux.reshape(B, S, N_AUX_SLABS * LANES)

    eye_g = jnp.eye(G, dtype=F32)

    def pos_rows(pos):
        tiled = jnp.broadcast_to(pos.reshape(2, CMP_STRIDE, 1, d), (2, CMP_STRIDE, G, d))
        return tiled[0].reshape(1, -1), tiled[1].reshape(1, -1)

    def w1_blocks(w1):
        w = w1.reshape(2, CMP_STRIDE, d, CMP_HIDDEN)
        blk = jnp.einsum('pldj,gh->plgdhj', w, eye_g).reshape(2, CMP_STRIDE * G * d, G * CMP_HIDDEN)
        return blk[0].astype(dt), blk[1].astype(dt)

    def w2_blocks(w2):
        return jnp.einsum('jd,gh->gjhd', w2, eye_g).reshape(G * CMP_HIDDEN, G * d).astype(dt)

    kpl, kph = pos_rows(k_pos)
    vpl, vph = pos_rows(v_pos)
    kwl, kwh = w1_blocks(k_w1)
    vwl, vwh = w1_blocks(v_w1)
    chunk_w = CMP_STRIDE * G * d
    w1_spec = _resident((chunk_w, G * CMP_HIDDEN))
    w2_spec = _resident((G * CMP_HIDDEN, LANES))
    kcmp, vcmp = pl.pallas_call(
        functools.partial(_compress_kernel, nc=nc),
        grid=(B,),
        in_specs=[pl.BlockSpec((None, S, LANES), lambda b: (b, 0, 0)),
                  pl.BlockSpec((None, S, LANES), lambda b: (b, 0, 1)),
                  _resident((1, chunk_w)), _resident((1, chunk_w)), _resident((1, chunk_w)), _resident((1, chunk_w)),
                  w1_spec, w1_spec, w2_spec, w1_spec, w1_spec, w2_spec],
        out_specs=[pl.BlockSpec((None, nc, LANES), lambda b: (b, 0, 0)),
                   pl.BlockSpec((None, nc, LANES), lambda b: (b, 0, 0))],
        out_shape=[jax.ShapeDtypeStruct((B, nc, LANES), dt), jax.ShapeDtypeStruct((B, nc, LANES), dt)],
        compiler_params=_params(("parallel",)),
        name="compress",
    )(aux, aux, kpl, kph, vpl, vph, kwl, kwh, w2_blocks(k_w2), vwl, vwh, w2_blocks(v_w2))

    tq_d, tk_d = 512, 512
    lam_vec = [v.reshape(1, d) for v in (lq1, lk1, lq2, lk2)]
    o_diff = pl.pallas_call(
        functools.partial(_diff_kernel, tq=tq_d, tk=tk_d, lam_init=lam_init),
        grid=(B, DIFF_HEADS, S // tq_d),
        in_specs=[_resident((1, d))] * 4 + [_resident((1, LANES)),
                  pl.BlockSpec((None, tq_d, LANES), lambda b, h, i: (b, i, h)),
                  pl.BlockSpec((None, S, LANES), lambda b, h, i: (b, 0, 4 + h)),
                  pl.BlockSpec((None, S, LANES), lambda b, h, i: (b, 0, 8 + h))],
        out_specs=pl.BlockSpec((None, tq_d, LANES), lambda b, h, i: (b, i, h)),
        out_shape=jax.ShapeDtypeStruct((B, S, DIFF_HEADS * LANES), dt),
        scratch_shapes=[pltpu.VMEM((2 * tq_d, LANES), F32)] * 3,
        compiler_params=_params(("parallel", "parallel", "arbitrary")),
        name="diff_attn",
    )(*lam_vec, diff_subln.reshape(1, LANES), p, p, p)

    tq_n, tk_n = 256, 512
    ovt = jnp.asarray(_selection_overlap_t(nc, n_cmp, n_sel)).astype(dt)
    rows = NSA_HEADS_PER_GROUP * tq_n

    def seq_slab(c):
        return pl.BlockSpec((None, S, LANES), lambda b, i: (b, 0, c))

    o_nsa = pl.pallas_call(
        functools.partial(_nsa_kernel, tq=tq_n, tk=tk_n, nc=nc, n_sel=n_sel, top_n=top_n),
        grid=(B, S // tq_n),
        in_specs=[pl.BlockSpec((None, tq_n, 4 * LANES), lambda b, i: (b, i, 3)),
                  pl.BlockSpec((None, tq_n, LANES), lambda b, i: (b, i, 2)),
                  pl.BlockSpec((None, nc, LANES), lambda b, i: (b, 0, 0)),
                  pl.BlockSpec((None, nc, LANES), lambda b, i: (b, 0, 0)),
                  _resident((LANES, nc)),
                  seq_slab(16), seq_slab(17), seq_slab(18), seq_slab(19), seq_slab(20), seq_slab(21),
                  seq_slab(22)],
        out_specs=pl.BlockSpec((None, tq_n, 4 * LANES), lambda b, i: (b, i, 0)),
        out_shape=jax.ShapeDtypeStruct((B, S, 4 * LANES), dt),
        scratch_shapes=[pltpu.VMEM((rows, LANES), F32)] * 2,
        compiler_params=_params(("parallel", "arbitrary")),
        name="nsa_attn",
    )(p, aux, kcmp, vcmp, ovt, p, p, p, p, p, p, p)

    tm2 = 512
    row = lambda i: (i, 0)
    out = pl.pallas_call(
        functools.partial(_post_kernel, d_ff=d_ff, ff_chunk=256),
        grid=(N // tm2,),
        in_specs=[pl.BlockSpec((tm2, D), row),
                  pl.BlockSpec((tm2, 512), row), pl.BlockSpec((tm2, 512), row),
                  _resident((512, D)), _resident((512, D)), _resident((1, D)), _resident((1, D)),
                  _resident((D, d_ff)), _resident((D, d_ff)), _resident((d_ff, D)), _resident((1, D))],
        out_specs=pl.BlockSpec((tm2, D), row),
        out_shape=jax.ShapeDtypeStruct((N, D), F32),
        scratch_shapes=[pltpu.VMEM((tm2, d_ff), dt)],
        compiler_params=_params(("parallel",)),
        name="post",
    )(x.reshape(N, D), o_diff.reshape(N, 512), o_nsa.reshape(N, 512), w_out_d, w_out_n,
      attn_post_norm.reshape(1, D), ffn_pre_norm.reshape(1, D),
      w_gate.astype(dt), w_up.astype(dt), w_down.astype(dt), ffn_post_norm.reshape(1, D))
    return out.reshape(B, S, D)


def kernel(x, attn_pre_norm, w_in, lambda_q1, lambda_k1, lambda_q2, lambda_k2, diff_subln, k_cmp_pos, k_cmp_w1, k_cmp_w2, v_cmp_pos, v_cmp_w1, v_cmp_w2, w_out, attn_post_norm, ffn_pre_norm, w_gate, w_up, w_down, ffn_post_norm):
    for l in range(w_in.shape[0]):
        x = _layer(x, l, attn_pre_norm[l], w_in[l], lambda_q1[l], lambda_k1[l], lambda_q2[l], lambda_k2[l],
                   diff_subln[l], k_cmp_pos[l], k_cmp_w1[l], k_cmp_w2[l], v_cmp_pos[l], v_cmp_w1[l], v_cmp_w2[l],
                   w_out[l], attn_post_norm[l], ffn_pre_norm[l], w_gate[l], w_up[l], w_down[l], ffn_post_norm[l])
    return x
```

```python
import functools
import math

import numpy as np
import jax
import jax.numpy as jnp
from jax import lax
from jax.experimental import pallas as pl
from jax.experimental.pallas import tpu as pltpu

F32 = jnp.float32
MXU_DTYPE = jnp.bfloat16

LANES = 128
SUBLANES = 8
HEAD_DIM = 64
ROPE_THETA = 10000.0
NORM_EPS = 1e-6
NEG_INF = -1e30
LOG2_E = 1.4426950408889634
FORCED_IMPORTANCE = 3e38

DIFF_HEADS = 4
NSA_HEADS_PER_GROUP = 4
NSA_KV_GROUPS = 2
CMP_BLOCK = 32
CMP_STRIDE = 16
CMP_HIDDEN = 4 * HEAD_DIM
SLC_BLOCK = 64
SLC_TOPK = 16
WINDOW = 512
N_BRANCHES = 3

N_MAIN_SLABS = 20
N_AUX_SLABS = 3
N_P_SLABS = 23
VMEM_LIMIT = 56 * 1024 * 1024


def _nt_dot(a, b):
    return lax.dot_general(a, b, (((1,), (1,)), ((), ())), preferred_element_type=F32)


def _dot(a, b):
    return jnp.dot(a, b, preferred_element_type=F32)


def _rms(x, g):
    return x * lax.rsqrt(jnp.mean(x * x, axis=-1, keepdims=True) + NORM_EPS) * g


def _inproj_kernel(x_ref, g_ref, w_ref, cos_ref, sin_ref, p_ref, aux_ref, *, tm, seq_tiles):
    h = _rms(x_ref[...], g_ref[...]).astype(MXU_DTYPE)
    cos = cos_ref[...]
    sin = sin_ref[...]
    lane = lax.broadcasted_iota(jnp.int32, (tm, LANES), 1)
    low_half = (lane & (HEAD_DIM - 1)) < HEAD_DIM // 2

    def rope(y):
        fwd = pltpu.roll(y, HEAD_DIM // 2, 1)
        bwd = pltpu.roll(y, LANES - HEAD_DIM // 2, 1)
        return y * cos + jnp.where(low_half, bwd, fwd) * sin

    pos = (pl.program_id(0) % seq_tiles) * tm + lax.broadcasted_iota(jnp.int32, (tm, LANES), 0)
    blk = pos // SLC_BLOCK
    scale = HEAD_DIM ** -0.5 * LOG2_E

    n_slabs = N_MAIN_SLABS + N_AUX_SLABS
    for c0 in range(0, n_slabs, 2):
        c1 = min(c0 + 2, n_slabs)
        y2 = _dot(h, w_ref[:, c0 * LANES:c1 * LANES])
        for s in range(c0, c1):
            y = y2[:, (s - c0) * LANES:(s - c0 + 1) * LANES]
            if s < 4 or 12 <= s < 16:
                p_ref[:, s * LANES:(s + 1) * LANES] = (rope(y) * scale).astype(p_ref.dtype)
            elif s < 8:
                p_ref[:, s * LANES:(s + 1) * LANES] = rope(y).astype(p_ref.dtype)
            elif s < 12:
                p_ref[:, s * LANES:(s + 1) * LANES] = y.astype(p_ref.dtype)
            elif s == 16:
                r = rope(y)
                ind_hi = jnp.where(lane - HEAD_DIM == blk, 1.0, 0.0)
                ind_lo = jnp.where(lane == blk, 1.0, 0.0)
                p_ref[:, 16 * LANES:17 * LANES] = jnp.where(lane < HEAD_DIM, r, ind_hi).astype(p_ref.dtype)
                p_ref[:, 17 * LANES:18 * LANES] = jnp.where(lane >= HEAD_DIM, r, ind_lo).astype(p_ref.dtype)
            elif s == 17 or s == 19:
                o = 18 if s == 17 else 21
                p_ref[:, o * LANES:(o + 1) * LANES] = jnp.where(lane < HEAD_DIM, y, 1.0).astype(p_ref.dtype)
                p_ref[:, (o + 1) * LANES:(o + 2) * LANES] = jnp.where(lane >= HEAD_DIM, y, 1.0).astype(p_ref.dtype)
            elif s == 18:
                p_ref[:, 20 * LANES:21 * LANES] = rope(y).astype(p_ref.dtype)
            elif s == 20:
                aux_ref[:, 0:LANES] = rope(y)
            elif s == 21:
                aux_ref[:, LANES:2 * LANES] = y
            else:
                aux_ref[:, 2 * LANES:3 * LANES] = jax.nn.sigmoid(y)


def _compress_kernel(tk_ref, tv_ref, kpl_ref, kph_ref, vpl_ref, vph_ref,
                     kwl_ref, kwh_ref, kw2_ref, vwl_ref, vwh_ref, vw2_ref, ko_ref, vo_ref, *, nc):
    def compress(t_ref, plo_ref, phi_ref, wlo_ref, whi_ref, w2_ref):
        x = jnp.concatenate([t_ref[pl.ds(l, nc, stride=CMP_STRIDE), :] for l in range(CMP_STRIDE)], axis=1)
        a = _dot((x + plo_ref[...]).astype(MXU_DTYPE), wlo_ref[...])
        b = _dot((x + phi_ref[...]).astype(MXU_DTYPE), whi_ref[...])
        hid = a + pltpu.roll(b, nc - 1, 0)
        act = hid * jax.nn.sigmoid(hid)
        return _dot(act.astype(MXU_DTYPE), w2_ref[...])

    ko_ref[...] = compress(tk_ref, kpl_ref, kph_ref, kwl_ref, kwh_ref, kw2_ref).astype(ko_ref.dtype)
    vo_ref[...] = compress(tv_ref, vpl_ref, vph_ref, vwl_ref, vwh_ref, vw2_ref).astype(vo_ref.dtype)


def _flash_init(m_ref, l_ref, acc_ref):
    m_ref[...] = jnp.full(m_ref.shape, NEG_INF, F32)
    if l_ref is not None:
        l_ref[...] = jnp.zeros(l_ref.shape, F32)
    acc_ref[...] = jnp.zeros(acc_ref.shape, F32)


def _lane_tiles(x):
    return [x[:, c:c + LANES] for c in range(0, x.shape[1], LANES)]


def _stack_rows(x, n):
    return jnp.concatenate([x] * n, axis=0)


def _softmax_pv(s, v, m_ref, l_ref, acc_ref, bias=None):
    if bias is not None:
        s = s + bias
    n_tiles = s.shape[1] // LANES
    m_prev = m_ref[...]
    m_new = jnp.maximum(m_prev, jnp.max(s, axis=-1, keepdims=True))
    alpha = jnp.exp2(m_prev - m_new)
    x = s - jnp.concatenate([m_new] * n_tiles, axis=1)
    if l_ref is None:
        p = jnp.exp2(x.astype(MXU_DTYPE))
    else:
        p = jnp.exp2(x)
        l_ref[...] = alpha * l_ref[...] + functools.reduce(lambda a, b: a + b, _lane_tiles(p))
        p = p.astype(MXU_DTYPE)
    acc_ref[...] = alpha * acc_ref[...] + _dot(p, v)
    m_ref[...] = m_new


def _flash_finish(l_ref, acc_ref):
    if l_ref is None:
        return acc_ref[...]
    return acc_ref[...] / jnp.sum(l_ref[...], axis=-1, keepdims=True)


def _causal_flash(q, k_ref, v_ref, m_ref, l_ref, acc_ref, s0, tq, tk):
    assert tq <= tk and tk % tq == 0
    rows = q.shape[0]
    _flash_init(m_ref, l_ref, acc_ref)

    def step(j, bias=None):
        k0 = pl.multiple_of(j * tk, tk)
        _softmax_pv(_nt_dot(q, k_ref[pl.ds(k0, tk), :]), v_ref[pl.ds(k0, tk), :], m_ref, l_ref, acc_ref, bias)

    def pair(jj, carry):
        step(2 * jj)
        step(2 * jj + 1)
        return carry

    n_full = s0 // tk
    lax.fori_loop(0, n_full // 2, pair, 0)

    @pl.when(n_full % 2 == 1)
    def _():
        step(n_full - 1)

    qpos = s0 + lax.broadcasted_iota(jnp.int32, (tq, tk), 0)
    causal = (n_full * tk + lax.broadcasted_iota(jnp.int32, (tq, tk), 1)) <= qpos
    step(n_full, _stack_rows(jnp.where(causal, 0.0, NEG_INF), rows // tq))
    return _flash_finish(l_ref, acc_ref)


def _diff_kernel(lq1_ref, lk1_ref, lq2_ref, lk2_ref, subln_ref, q_ref, k_ref, v_ref, o_ref,
                 m_ref, l_ref, acc_ref, *, tq, tk, lam_init):
    qi = pl.program_id(2)
    s0 = qi * tq
    lane = lax.broadcasted_iota(jnp.int32, (tq, LANES), 1)
    qf = q_ref[...].astype(F32)
    zero = jnp.zeros_like(qf)
    q2 = jnp.concatenate([jnp.where(lane < HEAD_DIM, qf, zero),
                          jnp.where(lane >= HEAD_DIM, qf, zero)], axis=0).astype(MXU_DTYPE)

    o12 = _causal_flash(q2, k_ref, v_ref, m_ref, l_ref, acc_ref, s0, tq, tk)
    lam = (jnp.exp(jnp.sum(lq1_ref[...] * lk1_ref[...], axis=-1, keepdims=True))
           - jnp.exp(jnp.sum(lq2_ref[...] * lk2_ref[...], axis=-1, keepdims=True)) + lam_init)
    o = o12[:tq] - lam * o12[tq:]
    o_ref[...] = (_rms(o, subln_ref[...]) * (1.0 - lam_init)).astype(o_ref.dtype)


def _nsa_kernel(q_ref, gate_ref, kc_ref, vc_ref, ovt_ref, ka0_ref, ka1_ref, vs0_ref, vs1_ref,
                kw_ref, vw0_ref, vw1_ref, o_ref, m_ref, acc_ref, *, tq, tk, nc, n_sel, top_n):
    hg = NSA_HEADS_PER_GROUP
    rows = hg * tq
    qi = pl.program_id(1)
    s0 = qi * tq
    lane = lax.broadcasted_iota(jnp.int32, (tq, LANES), 1)
    lane_rows = lax.broadcasted_iota(jnp.int32, (rows, LANES), 1)
    qf = [q_ref[:, h * LANES:(h + 1) * LANES].astype(F32) for h in range(hg)]
    gates = gate_ref[...]
    zero = jnp.zeros((tq, LANES), F32)

    wk = WINDOW + tq
    w0 = pl.multiple_of(jnp.maximum(s0 - WINDOW, 0), tq)
    back = (s0 - w0) + lax.broadcasted_iota(jnp.int32, (tq, wk), 0) \
        - lax.broadcasted_iota(jnp.int32, (tq, wk), 1)
    win_bias = _stack_rows(jnp.where((back >= 0) & (back < WINDOW), 0.0, NEG_INF), hg)

    n_idx = lax.broadcasted_iota(jnp.int32, (tq, nc), 1)
    qpos_c = s0 + lax.broadcasted_iota(jnp.int32, (tq, nc), 0)
    cmp_ok = n_idx * CMP_STRIDE + (CMP_BLOCK - 1) <= qpos_c
    cmp_bias = _stack_rows(jnp.where(cmp_ok, 0.0, NEG_INF), hg)
    cmp_keep = _stack_rows(jnp.where(cmp_ok, 1.0, 0.0), hg)

    def normalise(raw, g):
        in_half_rows = (lane_rows >= HEAD_DIM) if g else (lane_rows < HEAD_DIM)
        return raw / jnp.where(in_half_rows, pltpu.roll(raw, HEAD_DIM, 1), 1.0)

    def gate(g, h, branch):
        c = (g * hg + h) * N_BRANCHES + branch
        return gates[:, c:c + 1]

    q_sels, partial = [], []
    for g in range(NSA_KV_GROUPS):
        in_half = (lane >= HEAD_DIM) if g else (lane < HEAD_DIM)
        q_plain = jnp.concatenate([jnp.where(in_half, qf[h], zero) for h in range(hg)],
                                  axis=0).astype(MXU_DTYPE)

        sc = _nt_dot(q_plain, kc_ref[...])
        sc = sc + cmp_bias
        pc = jnp.exp2(sc - jnp.max(sc, axis=-1, keepdims=True)) * cmp_keep
        lc = jnp.sum(pc, axis=-1, keepdims=True)
        pc = (pc / jnp.where(lc > 0.0, lc, 1.0)).astype(MXU_DTYPE)
        o_cmp = _dot(pc, vc_ref[...])

        imp_t = None
        for h in range(hg):
            part = _nt_dot(ovt_ref[...], pc[h * tq:(h + 1) * tq])
            imp_t = part if imp_t is None else imp_t + part
        imp_t = imp_t[:SLC_BLOCK]
        jb = lax.broadcasted_iota(jnp.int32, (SLC_BLOCK, tq), 0)
        qp = s0 + lax.broadcasted_iota(jnp.int32, (SLC_BLOCK, tq), 1)
        cur = qp // SLC_BLOCK
        valid = jb * SLC_BLOCK <= qp
        forced = (jb == 0) | (jb == cur) | (jb == cur - 1)
        key = jnp.where(forced, FORCED_IMPORTANCE, jnp.where(valid, imp_t, -1.0))
        n_grp = SLC_BLOCK // SUBLANES
        key_g = [key[a * SUBLANES:(a + 1) * SUBLANES] for a in range(n_grp)]
        jb_g = lax.broadcasted_iota(jnp.int32, (SUBLANES, tq), 0)
        rank_g = [jnp.zeros((SUBLANES, tq), jnp.int32) for _ in range(n_grp)]
        for i in range(n_sel):
            ri = key[i:i + 1, :]
            for a in range(n_grp):
                if a < i // SUBLANES:
                    ahead = jnp.where(ri > key_g[a], 1, 0)
                elif a > i // SUBLANES:
                    ahead = jnp.where(ri >= key_g[a], 1, 0)
                else:
                    ahead = (jnp.where(ri > key_g[a], 1, 0)
                             + jnp.where((ri == key_g[a]) & (jb_g > i % SUBLANES), 1, 0))
                rank_g[a] = rank_g[a] + ahead
        selected = (jnp.concatenate(rank_g, axis=0) < top_n) & valid
        bias_t = jnp.where(selected, 0.0, NEG_INF)
        pad_t = jnp.zeros((SLC_BLOCK, tq), F32)
        bias = jnp.transpose(jnp.concatenate([pad_t, bias_t] if g == 0 else [bias_t, pad_t], axis=0))
        q_sels.append(jnp.concatenate([jnp.where(in_half, qf[h], bias) for h in range(hg)],
                                      axis=0).astype(MXU_DTYPE))

        sw = _nt_dot(q_plain, kw_ref[pl.ds(w0, wk), :])
        sw = sw + win_bias
        pw = jnp.exp2((sw - jnp.max(sw, axis=-1, keepdims=True)).astype(MXU_DTYPE))
        o_win = normalise(_dot(pw, (vw1_ref if g else vw0_ref)[pl.ds(w0, wk), :]), g)
        partial.append([gate(g, h, 0) * o_cmp[h * tq:(h + 1) * tq] + gate(g, h, 2) * o_win[h * tq:(h + 1) * tq]
                        for h in range(hg)])

    outs = []
    for g in range(NSA_KV_GROUPS):
        o_slc = normalise(_causal_flash(q_sels[g], ka1_ref if g else ka0_ref, vs1_ref if g else vs0_ref,
                                        m_ref, None, acc_ref, s0, tq, tk), g)
        outs.append([partial[g][h] + gate(g, h, 1) * o_slc[h * tq:(h + 1) * tq] for h in range(hg)])

    for h in range(hg):
        o_ref[:, h * LANES:(h + 1) * LANES] = jnp.where(lane < HEAD_DIM, outs[0][h], outs[1][h]).astype(o_ref.dtype)


def _post_kernel(x_ref, od_ref, on_ref, wod_ref, won_ref, gpost_ref, gpre_ref, wg_ref, wu_ref, wd_ref,
                 gffn_ref, o_ref, act_ref, *, d_ff, ff_chunk):
    mix = _dot(od_ref[...], wod_ref[...]) + _dot(on_ref[...], won_ref[...])
    x1 = x_ref[...] + _rms(mix, gpost_ref[...])
    h = _rms(x1, gpre_ref[...]).astype(MXU_DTYPE)
    for c in range(0, d_ff, ff_chunk):
        gate = _dot(h, wg_ref[:, c:c + ff_chunk])
        up = _dot(h, wu_ref[:, c:c + ff_chunk])
        act_ref[:, c:c + ff_chunk] = (gate * jax.nn.sigmoid(gate) * up).astype(act_ref.dtype)
    f = _dot(act_ref[...], wd_ref[...])
    o_ref[...] = x1 + _rms(f, gffn_ref[...])


def _resident(shape):
    nd = len(shape)
    return pl.BlockSpec(shape, lambda *_: (0,) * nd, pipeline_mode=pl.Buffered(1))


def _params(sem):
    return pltpu.CompilerParams(dimension_semantics=sem, vmem_limit_bytes=VMEM_LIMIT)


def _rope_tables(S):
    inv = 1.0 / (ROPE_THETA ** (jnp.arange(0, HEAD_DIM, 2, dtype=F32) / HEAD_DIM))
    ang = jnp.arange(S, dtype=F32)[:, None] * inv[None, :]
    cos, sin = jnp.cos(ang), jnp.sin(ang)
    return jnp.tile(cos, (1, 4)), jnp.concatenate([-sin, sin, -sin, sin], axis=1)


def _selection_overlap_t(nc, n_cmp, n_sel):
    c0 = np.arange(n_cmp)[:, None] * CMP_STRIDE
    b0 = np.arange(n_sel)[None, :] * SLC_BLOCK
    ov = np.clip(np.minimum(c0 + CMP_BLOCK, b0 + SLC_BLOCK) - np.maximum(c0, b0), 0, None) / CMP_BLOCK
    full = np.zeros((LANES, nc), np.float32)
    full[:n_sel, :n_cmp] = ov.T
    return full


def _layer(x, layer, attn_pre_norm, w_in, lq1, lk1, lq2, lk2, diff_subln, k_pos, k_w1, k_w2,
           v_pos, v_w1, v_w2, w_out, attn_post_norm, ffn_pre_norm, w_gate, w_up, w_down, ffn_post_norm):
    B, S, D = x.shape
    N = B * S
    d_ff = w_gate.shape[1]
    nc = S // CMP_STRIDE
    n_cmp = (S - CMP_BLOCK) // CMP_STRIDE + 1
    n_sel = S // SLC_BLOCK
    top_n = min(SLC_TOPK, n_sel)
    assert n_sel <= SLC_BLOCK and S % 512 == 0
    lam_init = 0.8 - 0.6 * math.exp(-0.3 * layer)
    dt = MXU_DTYPE

    hg, G, d = NSA_HEADS_PER_GROUP, NSA_KV_GROUPS, HEAD_DIM
    nq_perm = np.array([1536 + (g * hg + h) * d + e for h in range(hg) for g in range(G) for e in range(d)])
    cols = np.concatenate([np.arange(0, 1536), nq_perm,
                           np.arange(2304, 2816),
                           np.arange(2048, 2304),
                           np.arange(2816, 2840)])
    w_cat = jnp.pad(w_in[:, cols], ((0, 0), (0, LANES - hg * G * N_BRANCHES))).astype(dt)
    w_out_d = w_out[:512].astype(dt)
    w_out_n = w_out[nq_perm - 1536 + 512].astype(dt)
    cos_t, sin_t = _rope_tables(S)

    tm = 512
    seq_tiles = S // tm
    n_cols = (N_MAIN_SLABS + N_AUX_SLABS) * LANES
    p, aux = pl.pallas_call(
        functools.partial(_inproj_kernel, tm=tm, seq_tiles=seq_tiles),
        grid=(N // tm,),
        in_specs=[pl.BlockSpec((tm, D), lambda i: (i, 0)),
                  _resident((1, D)),
                  _resident((D, n_cols)),
                  pl.BlockSpec((tm, LANES), lambda i: (i % seq_tiles, 0)),
                  pl.BlockSpec((tm, LANES), lambda i: (i % seq_tiles, 0))],
        out_specs=[pl.BlockSpec((tm, N_P_SLABS * LANES), lambda i: (i, 0)),
                   pl.BlockSpec((tm, N_AUX_SLABS * LANES), lambda i: (i, 0))],
        out_shape=[jax.ShapeDtypeStruct((N, N_P_SLABS * LANES), dt),
                   jax.ShapeDtypeStruct((N, N_AUX_SLABS * LANES), F32)],
        compiler_params=_params(("parallel",)),
        name="inproj",
    )(x.reshape(N, D), attn_pre_norm.reshape(1, D), w_cat, cos_t, sin_t)
    p = p.reshape(B, S, N_P_SLABS * LANES)
    aux = aux.reshape(B, S, N_AUX_SLABS * LANES)

    eye_g = jnp.eye(G, dtype=F32)

    def pos_rows(pos):
        tiled = jnp.broadcast_to(pos.reshape(2, CMP_STRIDE, 1, d), (2, CMP_STRIDE, G, d))
        return tiled[0].reshape(1, -1), tiled[1].reshape(1, -1)

    def w1_blocks(w1):
        w = w1.reshape(2, CMP_STRIDE, d, CMP_HIDDEN)
        blk = jnp.einsum('pldj,gh->plgdhj', w, eye_g).reshape(2, CMP_STRIDE * G * d, G * CMP_HIDDEN)
        return blk[0].astype(dt), blk[1].astype(dt)

    def w2_blocks(w2):
        return jnp.einsum('jd,gh->gjhd', w2, eye_g).reshape(G * CMP_HIDDEN, G * d).astype(dt)

    kpl, kph = pos_rows(k_pos)
    vpl, vph = pos_rows(v_pos)
    kwl, kwh = w1_blocks(k_w1)
    vwl, vwh = w1_blocks(v_w1)
    chunk_w = CMP_STRIDE * G * d
    w1_spec = _resident((chunk_w, G * CMP_HIDDEN))
    w2_spec = _resident((G * CMP_HIDDEN, LANES))
    kcmp, vcmp = pl.pallas_call(
        functools.partial(_compress_kernel, nc=nc),
        grid=(B,),
        in_specs=[pl.BlockSpec((None, S, LANES), lambda b: (b, 0, 0)),
                  pl.BlockSpec((None, S, LANES), lambda b: (b, 0, 1)),
                  _resident((1, chunk_w)), _resident((1, chunk_w)), _resident((1, chunk_w)), _resident((1, chunk_w)),
                  w1_spec, w1_spec, w2_spec, w1_spec, w1_spec, w2_spec],
        out_specs=[pl.BlockSpec((None, nc, LANES), lambda b: (b, 0, 0)),
                   pl.BlockSpec((None, nc, LANES), lambda b: (b, 0, 0))],
        out_shape=[jax.ShapeDtypeStruct((B, nc, LANES), dt), jax.ShapeDtypeStruct((B, nc, LANES), dt)],
        compiler_params=_params(("parallel",)),
        name="compress",
    )(aux, aux, kpl, kph, vpl, vph, kwl, kwh, w2_blocks(k_w2), vwl, vwh, w2_blocks(v_w2))

    tq_d, tk_d = 512, 512
    lam_vec = [v.reshape(1, d) for v in (lq1, lk1, lq2, lk2)]
    o_diff = pl.pallas_call(
        functools.partial(_diff_kernel, tq=tq_d, tk=tk_d, lam_init=lam_init),
        grid=(B, DIFF_HEADS, S // tq_d),
        in_specs=[_resident((1, d))] * 4 + [_resident((1, LANES)),
                  pl.BlockSpec((None, tq_d, LANES), lambda b, h, i: (b, i, h)),
                  pl.BlockSpec((None, S, LANES), lambda b, h, i: (b, 0, 4 + h)),
                  pl.BlockSpec((None, S, LANES), lambda b, h, i: (b, 0, 8 + h))],
        out_specs=pl.BlockSpec((None, tq_d, LANES), lambda b, h, i: (b, i, h)),
        out_shape=jax.ShapeDtypeStruct((B, S, DIFF_HEADS * LANES), dt),
        scratch_shapes=[pltpu.VMEM((2 * tq_d, LANES), F32)] * 3,
        compiler_params=_params(("parallel", "parallel", "arbitrary")),
        name="diff_attn",
    )(*lam_vec, diff_subln.reshape(1, LANES), p, p, p)

    tq_n, tk_n = 256, 512
    ovt = jnp.asarray(_selection_overlap_t(nc, n_cmp, n_sel)).astype(dt)
    rows = NSA_HEADS_PER_GROUP * tq_n

    def seq_slab(c):
        return pl.BlockSpec((None, S, LANES), lambda b, i: (b, 0, c))

    o_nsa = pl.pallas_call(
        functools.partial(_nsa_kernel, tq=tq_n, tk=tk_n, nc=nc, n_sel=n_sel, top_n=top_n),
        grid=(B, S // tq_n),
        in_specs=[pl.BlockSpec((None, tq_n, 4 * LANES), lambda b, i: (b, i, 3)),
                  pl.BlockSpec((None, tq_n, LANES), lambda b, i: (b, i, 2)),
                  pl.BlockSpec((None, nc, LANES), lambda b, i: (b, 0, 0)),
                  pl.BlockSpec((None, nc, LANES), lambda b, i: (b, 0, 0)),
                  _resident((LANES, nc)),
                  seq_slab(16), seq_slab(17), seq_slab(18), seq_slab(19), seq_slab(20), seq_slab(21),
                  seq_slab(22)],
        out_specs=pl.BlockSpec((None, tq_n, 4 * LANES), lambda b, i: (b, i, 0)),
        out_shape=jax.ShapeDtypeStruct((B, S, 4 * LANES), dt),
        scratch_shapes=[pltpu.VMEM((rows, LANES), F32)] * 2,
        compiler_params=_params(("parallel", "arbitrary")),
        name="nsa_attn",
    )(p, aux, kcmp, vcmp, ovt, p, p, p, p, p, p, p)

    tm2 = 512
    row = lambda i: (i, 0)
    out = pl.pallas_call(
        functools.partial(_post_kernel, d_ff=d_ff, ff_chunk=256),
        grid=(N // tm2,),
        in_specs=[pl.BlockSpec((tm2, D), row),
                  pl.BlockSpec((tm2, 512), row), pl.BlockSpec((tm2, 512), row),
                  _resident((512, D)), _resident((512, D)), _resident((1, D)), _resident((1, D)),
                  _resident((D, d_ff)), _resident((D, d_ff)), _resident((d_ff, D)), _resident((1, D))],
        out_specs=pl.BlockSpec((tm2, D), row),
        out_shape=jax.ShapeDtypeStruct((N, D), F32),
        scratch_shapes=[pltpu.VMEM((tm2, d_ff), dt)],
        compiler_params=_params(("parallel",)),
        name="post",
    )(x.reshape(N, D), o_diff.reshape(N, 512), o_nsa.reshape(N, 512), w_out_d, w_out_n,
      attn_post_norm.reshape(1, D), ffn_pre_norm.reshape(1, D),
      w_gate.astype(dt), w_up.astype(dt), w_down.astype(dt), ffn_post_norm.reshape(1, D))
    return out.reshape(B, S, D)


def kernel(x, attn_pre_norm, w_in, lambda_q1, lambda_k1, lambda_q2, lambda_k2, diff_subln, k_cmp_pos, k_cmp_w1, k_cmp_w2, v_cmp_pos, v_cmp_w1, v_cmp_w2, w_out, attn_post_norm, ffn_pre_norm, w_gate, w_up, w_down, ffn_post_norm):
    for l in range(w_in.shape[0]):
        x = _layer(x, l, attn_pre_norm[l], w_in[l], lambda_q1[l], lambda_k1[l], lambda_q2[l], lambda_k2[l],
                   diff_subln[l], k_cmp_pos[l], k_cmp_w1[l], k_cmp_w2[l], v_cmp_pos[l], v_cmp_w1[l], v_cmp_w2[l],
                   w_out[l], attn_post_norm[l], ffn_pre_norm[l], w_gate[l], w_up[l], w_down[l], ffn_post_norm[l])
    return x
```

```python
import functools
import math

import numpy as np
import jax
import jax.numpy as jnp
from jax import lax
from jax.experimental import pallas as pl
from jax.experimental.pallas import tpu as pltpu

F32 = jnp.float32
MXU_DTYPE = jnp.bfloat16

LANES = 128
SUBLANES = 8
HEAD_DIM = 64
ROPE_THETA = 10000.0
NORM_EPS = 1e-6
NEG_INF = -1e30
LOG2_E = 1.4426950408889634
FORCED_IMPORTANCE = 3e38

DIFF_HEADS = 4
NSA_HEADS_PER_GROUP = 4
NSA_KV_GROUPS = 2
CMP_BLOCK = 32
CMP_STRIDE = 16
CMP_HIDDEN = 4 * HEAD_DIM
SLC_BLOCK = 64
SLC_TOPK = 16
WINDOW = 512
N_BRANCHES = 3
SEQ_PARTS = 4

N_MAIN_SLABS = 20
N_AUX_SLABS = 3
N_P_SLABS = 23
VMEM_LIMIT = 56 * 1024 * 1024


def _nt_dot(a, b):
    return lax.dot_general(a, b, (((1,), (1,)), ((), ())), preferred_element_type=F32)


def _dot(a, b):
    return jnp.dot(a, b, preferred_element_type=F32)


def _rms(x, g):
    return x * lax.rsqrt(jnp.mean(x * x, axis=-1, keepdims=True) + NORM_EPS) * g


def _inproj_kernel(x_ref, g_ref, w_ref, cos_ref, sin_ref, p_ref, aux_ref, *, tm, seq_tiles):
    h = _rms(x_ref[...], g_ref[...]).astype(MXU_DTYPE)
    cos = cos_ref[...]
    sin = sin_ref[...]
    lane = lax.broadcasted_iota(jnp.int32, (tm, LANES), 1)
    low_half = (lane & (HEAD_DIM - 1)) < HEAD_DIM // 2

    def rope(y):
        fwd = pltpu.roll(y, HEAD_DIM // 2, 1)
        bwd = pltpu.roll(y, LANES - HEAD_DIM // 2, 1)
        return y * cos + jnp.where(low_half, bwd, fwd) * sin

    pos = (pl.program_id(0) % seq_tiles) * tm + lax.broadcasted_iota(jnp.int32, (tm, LANES), 0)
    blk = pos // SLC_BLOCK
    scale = HEAD_DIM ** -0.5 * LOG2_E

    n_slabs = N_MAIN_SLABS + N_AUX_SLABS
    for c0 in range(0, n_slabs, 2):
        c1 = min(c0 + 2, n_slabs)
        y2 = _dot(h, w_ref[:, c0 * LANES:c1 * LANES])
        for s in range(c0, c1):
            y = y2[:, (s - c0) * LANES:(s - c0 + 1) * LANES]
            if s < 4 or 12 <= s < 16:
                p_ref[:, s * LANES:(s + 1) * LANES] = (rope(y) * scale).astype(p_ref.dtype)
            elif s < 8:
                p_ref[:, s * LANES:(s + 1) * LANES] = rope(y).astype(p_ref.dtype)
            elif s < 12:
                p_ref[:, s * LANES:(s + 1) * LANES] = y.astype(p_ref.dtype)
            elif s == 16:
                r = rope(y)
                ind_hi = jnp.where(lane - HEAD_DIM == blk, 1.0, 0.0)
                ind_lo = jnp.where(lane == blk, 1.0, 0.0)
                p_ref[:, 16 * LANES:17 * LANES] = jnp.where(lane < HEAD_DIM, r, ind_hi).astype(p_ref.dtype)
                p_ref[:, 17 * LANES:18 * LANES] = jnp.where(lane >= HEAD_DIM, r, ind_lo).astype(p_ref.dtype)
            elif s == 17 or s == 19:
                o = 18 if s == 17 else 21
                p_ref[:, o * LANES:(o + 1) * LANES] = jnp.where(lane < HEAD_DIM, y, 1.0).astype(p_ref.dtype)
                p_ref[:, (o + 1) * LANES:(o + 2) * LANES] = jnp.where(lane >= HEAD_DIM, y, 1.0).astype(p_ref.dtype)
            elif s == 18:
                p_ref[:, 20 * LANES:21 * LANES] = rope(y).astype(p_ref.dtype)
            elif s == 20:
                aux_ref[:, 0:LANES] = rope(y)
            elif s == 21:
                aux_ref[:, LANES:2 * LANES] = y
            else:
                aux_ref[:, 2 * LANES:3 * LANES] = jax.nn.sigmoid(y)


def _compress_kernel(tk_ref, tv_ref, kpl_ref, kph_ref, vpl_ref, vph_ref,
                     kwl_ref, kwh_ref, kw2_ref, vwl_ref, vwh_ref, vw2_ref, ko_ref, vo_ref, *, nc):
    def compress(t_ref, plo_ref, phi_ref, wlo_ref, whi_ref, w2_ref):
        x = jnp.concatenate([t_ref[pl.ds(l, nc, stride=CMP_STRIDE), :] for l in range(CMP_STRIDE)], axis=1)
        a = _dot((x + plo_ref[...]).astype(MXU_DTYPE), wlo_ref[...])
        b = _dot((x + phi_ref[...]).astype(MXU_DTYPE), whi_ref[...])
        hid = a + pltpu.roll(b, nc - 1, 0)
        act = hid * jax.nn.sigmoid(hid)
        return _dot(act.astype(MXU_DTYPE), w2_ref[...])

    ko_ref[...] = compress(tk_ref, kpl_ref, kph_ref, kwl_ref, kwh_ref, kw2_ref).astype(ko_ref.dtype)
    vo_ref[...] = compress(tv_ref, vpl_ref, vph_ref, vwl_ref, vwh_ref, vw2_ref).astype(vo_ref.dtype)


def _flash_init(m_ref, l_ref, acc_ref):
    m_ref[...] = jnp.full(m_ref.shape, NEG_INF, F32)
    if l_ref is not None:
        l_ref[...] = jnp.zeros(l_ref.shape, F32)
    acc_ref[...] = jnp.zeros(acc_ref.shape, F32)


def _lane_tiles(x):
    return [x[:, c:c + LANES] for c in range(0, x.shape[1], LANES)]


def _stack_rows(x, n):
    return jnp.concatenate([x] * n, axis=0)


def _softmax_pv(s, v, m_ref, l_ref, acc_ref, bias=None):
    if bias is not None:
        s = s + bias
    n_tiles = s.shape[1] // LANES
    m_prev = m_ref[...]
    m_new = jnp.maximum(m_prev, jnp.max(s, axis=-1, keepdims=True))
    alpha = jnp.exp2(m_prev - m_new)
    x = s - jnp.concatenate([m_new] * n_tiles, axis=1)
    if l_ref is None:
        p = jnp.exp2(x.astype(MXU_DTYPE))
    else:
        p = jnp.exp2(x)
        l_ref[...] = alpha * l_ref[...] + functools.reduce(lambda a, b: a + b, _lane_tiles(p))
        p = p.astype(MXU_DTYPE)
    acc_ref[...] = alpha * acc_ref[...] + _dot(p, v)
    m_ref[...] = m_new


def _flash_finish(l_ref, acc_ref):
    if l_ref is None:
        return acc_ref[...]
    return acc_ref[...] / jnp.sum(l_ref[...], axis=-1, keepdims=True)


def _causal_flash(q, k_ref, v_ref, m_ref, l_ref, acc_ref, s0, tq, tk):
    assert tq <= tk and tk % tq == 0
    rows = q.shape[0]
    _flash_init(m_ref, l_ref, acc_ref)

    def step(j, bias=None):
        k0 = pl.multiple_of(j * tk, tk)
        _softmax_pv(_nt_dot(q, k_ref[pl.ds(k0, tk), :]), v_ref[pl.ds(k0, tk), :], m_ref, l_ref, acc_ref, bias)

    def pair(jj, carry):
        step(2 * jj)
        step(2 * jj + 1)
        return carry

    n_full = s0 // tk
    lax.fori_loop(0, n_full // 2, pair, 0)

    @pl.when(n_full % 2 == 1)
    def _():
        step(n_full - 1)

    qpos = s0 + lax.broadcasted_iota(jnp.int32, (tq, tk), 0)
    causal = (n_full * tk + lax.broadcasted_iota(jnp.int32, (tq, tk), 1)) <= qpos
    step(n_full, _stack_rows(jnp.where(causal, 0.0, NEG_INF), rows // tq))
    return _flash_finish(l_ref, acc_ref)


def _diff_kernel(lq1_ref, lk1_ref, lq2_ref, lk2_ref, subln_ref, q_ref, k_ref, v_ref, o_ref,
                 m_ref, l_ref, acc_ref, *, tq, tk, lam_init):
    qi = pl.program_id(2)
    s0 = qi * tq
    lane = lax.broadcasted_iota(jnp.int32, (tq, LANES), 1)
    qf = q_ref[...].astype(F32)
    zero = jnp.zeros_like(qf)
    q2 = jnp.concatenate([jnp.where(lane < HEAD_DIM, qf, zero),
                          jnp.where(lane >= HEAD_DIM, qf, zero)], axis=0).astype(MXU_DTYPE)

    o12 = _causal_flash(q2, k_ref, v_ref, m_ref, l_ref, acc_ref, s0, tq, tk)
    lam = (jnp.exp(jnp.sum(lq1_ref[...] * lk1_ref[...], axis=-1, keepdims=True))
           - jnp.exp(jnp.sum(lq2_ref[...] * lk2_ref[...], axis=-1, keepdims=True)) + lam_init)
    o = o12[:tq] - lam * o12[tq:]
    o_ref[...] = (_rms(o, subln_ref[...]) * (1.0 - lam_init)).astype(o_ref.dtype)


def _nsa_kernel(q_ref, gate_ref, kc_ref, vc_ref, ovt_ref, ka0_ref, ka1_ref, vs0_ref, vs1_ref,
                kw_ref, vw0_ref, vw1_ref, o_ref, m_ref, acc_ref, qsel_ref, part_ref,
                *, tq, tk, nc, n_sel, top_n, n_q_tiles):
    hg = NSA_HEADS_PER_GROUP
    rows = hg * tq
    qi = pl.program_id(1)
    s0 = qi * tq
    lane = lax.broadcasted_iota(jnp.int32, (tq, LANES), 1)
    lane_rows = lax.broadcasted_iota(jnp.int32, (rows, LANES), 1)
    qf = [q_ref[:, h * LANES:(h + 1) * LANES].astype(F32) for h in range(hg)]
    gates = gate_ref[...]
    zero = jnp.zeros((tq, LANES), F32)

    wk = WINDOW + tq
    w0 = pl.multiple_of(jnp.maximum(s0 - WINDOW, 0), tq)
    back = (s0 - w0) + lax.broadcasted_iota(jnp.int32, (tq, wk), 0) \
        - lax.broadcasted_iota(jnp.int32, (tq, wk), 1)
    win_bias = _stack_rows(jnp.where((back >= 0) & (back < WINDOW), 0.0, NEG_INF), hg)

    def normalise(raw, g):
        in_half_rows = (lane_rows >= HEAD_DIM) if g else (lane_rows < HEAD_DIM)
        return raw / jnp.where(in_half_rows, pltpu.roll(raw, HEAD_DIM, 1), 1.0)

    def gate(g, h, branch):
        c = (g * hg + h) * N_BRANCHES + branch
        return gates[:, c:c + 1]

    def phase1(n_blk, nc_eff):
        n_idx = lax.broadcasted_iota(jnp.int32, (tq, nc_eff), 1)
        qpos_c = s0 + lax.broadcasted_iota(jnp.int32, (tq, nc_eff), 0)
        cmp_ok = n_idx * CMP_STRIDE + (CMP_BLOCK - 1) <= qpos_c
        cmp_bias = _stack_rows(jnp.where(cmp_ok, 0.0, NEG_INF), hg)
        cmp_keep = _stack_rows(jnp.where(cmp_ok, 1.0, 0.0), hg)
        jb = lax.broadcasted_iota(jnp.int32, (n_blk, tq), 0)
        qp = s0 + lax.broadcasted_iota(jnp.int32, (n_blk, tq), 1)
        cur = qp // SLC_BLOCK
        valid = jb * SLC_BLOCK <= qp
        forced = (jb == 0) | (jb == cur) | (jb == cur - 1)

        for g in range(NSA_KV_GROUPS):
            in_half = (lane >= HEAD_DIM) if g else (lane < HEAD_DIM)
            q_plain = jnp.concatenate([jnp.where(in_half, qf[h], zero) for h in range(hg)],
                                      axis=0).astype(MXU_DTYPE)

            sc = _nt_dot(q_plain, kc_ref[:nc_eff, :]) + cmp_bias
            pc = jnp.exp2(sc - jnp.max(sc, axis=-1, keepdims=True)) * cmp_keep
            lc = jnp.sum(pc, axis=-1, keepdims=True)
            pc = (pc / jnp.where(lc > 0.0, lc, 1.0)).astype(MXU_DTYPE)
            o_cmp = _dot(pc, vc_ref[:nc_eff, :])

            if n_blk <= top_n:
                selected = valid
            else:
                imp_t = None
                for h in range(hg):
                    part = _nt_dot(ovt_ref[:, :nc_eff], pc[h * tq:(h + 1) * tq])
                    imp_t = part if imp_t is None else imp_t + part
                key = jnp.where(forced, FORCED_IMPORTANCE, jnp.where(valid, imp_t[:n_blk], -1.0))
                n_grp = n_blk // SUBLANES
                key_g = [key[a * SUBLANES:(a + 1) * SUBLANES] for a in range(n_grp)]
                jb_g = lax.broadcasted_iota(jnp.int32, (SUBLANES, tq), 0)
                rank_g = [jnp.zeros((SUBLANES, tq), jnp.int32) for _ in range(n_grp)]
                for i in range(n_blk):
                    ri = key[i:i + 1, :]
                    for a in range(n_grp):
                        if a < i // SUBLANES:
                            ahead = jnp.where(ri > key_g[a], 1, 0)
                        elif a > i // SUBLANES:
                            ahead = jnp.where(ri >= key_g[a], 1, 0)
                        else:
                            ahead = (jnp.where(ri > key_g[a], 1, 0)
                                     + jnp.where((ri == key_g[a]) & (jb_g > i % SUBLANES), 1, 0))
                        rank_g[a] = rank_g[a] + ahead
                selected = (jnp.concatenate(rank_g, axis=0) < top_n) & valid
            bias_t = jnp.where(selected, 0.0, NEG_INF)
            if n_blk < SLC_BLOCK:
                bias_t = jnp.concatenate([bias_t, jnp.full((SLC_BLOCK - n_blk, tq), NEG_INF, F32)], axis=0)
            pad_t = jnp.zeros((SLC_BLOCK, tq), F32)
            bias = jnp.transpose(jnp.concatenate([pad_t, bias_t] if g == 0 else [bias_t, pad_t], axis=0))
            qsel_ref[g] = jnp.concatenate([jnp.where(in_half, qf[h], bias) for h in range(hg)],
                                          axis=0).astype(qsel_ref.dtype)

            sw = _nt_dot(q_plain, kw_ref[pl.ds(w0, wk), :]) + win_bias
            pw = jnp.exp2((sw - jnp.max(sw, axis=-1, keepdims=True)).astype(MXU_DTYPE))
            o_win = normalise(_dot(pw, (vw1_ref if g else vw0_ref)[pl.ds(w0, wk), :]), g)
            part_ref[g] = jnp.concatenate(
                [gate(g, h, 0) * o_cmp[h * tq:(h + 1) * tq] + gate(g, h, 2) * o_win[h * tq:(h + 1) * tq]
                 for h in range(hg)], axis=0)

    tiles_per_part = n_q_tiles // SEQ_PARTS
    for c in range(SEQ_PARTS):
        nc_part = min(nc, -(-(nc * (c + 1) // SEQ_PARTS) // LANES) * LANES)
        pl.when(qi // tiles_per_part == c)(
            functools.partial(phase1, n_sel * (c + 1) // SEQ_PARTS, nc_part))

    outs = []
    for g in range(NSA_KV_GROUPS):
        o_slc = normalise(_causal_flash(qsel_ref[g], ka1_ref if g else ka0_ref, vs1_ref if g else vs0_ref,
                                        m_ref, None, acc_ref, s0, tq, tk), g)
        outs.append([part_ref[g, h * tq:(h + 1) * tq, :] + gate(g, h, 1) * o_slc[h * tq:(h + 1) * tq]
                     for h in range(hg)])

    for h in range(hg):
        o_ref[:, h * LANES:(h + 1) * LANES] = jnp.where(lane < HEAD_DIM, outs[0][h], outs[1][h]).astype(o_ref.dtype)


def _post_kernel(x_ref, od_ref, on_ref, wod_ref, won_ref, gpost_ref, gpre_ref, wg_ref, wu_ref, wd_ref,
                 gffn_ref, o_ref, act_ref, *, d_ff, ff_chunk):
    mix = _dot(od_ref[...], wod_ref[...]) + _dot(on_ref[...], won_ref[...])
    x1 = x_ref[...] + _rms(mix, gpost_ref[...])
    h = _rms(x1, gpre_ref[...]).astype(MXU_DTYPE)
    for c in range(0, d_ff, ff_chunk):
        gate = _dot(h, wg_ref[:, c:c + ff_chunk])
        up = _dot(h, wu_ref[:, c:c + ff_chunk])
        act_ref[:, c:c + ff_chunk] = (gate * jax.nn.sigmoid(gate) * up).astype(act_ref.dtype)
    f = _dot(act_ref[...], wd_ref[...])
    o_ref[...] = x1 + _rms(f, gffn_ref[...])


def _resident(shape):
    nd = len(shape)
    return pl.BlockSpec(shape, lambda *_: (0,) * nd, pipeline_mode=pl.Buffered(1))


def _params(sem):
    return pltpu.CompilerParams(dimension_semantics=sem, vmem_limit_bytes=VMEM_LIMIT)


def _rope_tables(S):
    inv = 1.0 / (ROPE_THETA ** (jnp.arange(0, HEAD_DIM, 2, dtype=F32) / HEAD_DIM))
    ang = jnp.arange(S, dtype=F32)[:, None] * inv[None, :]
    cos, sin = jnp.cos(ang), jnp.sin(ang)
    return jnp.tile(cos, (1, 4)), jnp.concatenate([-sin, sin, -sin, sin], axis=1)


def _selection_overlap_t(nc, n_cmp, n_sel):
    c0 = np.arange(n_cmp)[:, None] * CMP_STRIDE
    b0 = np.arange(n_sel)[None, :] * SLC_BLOCK
    ov = np.clip(np.minimum(c0 + CMP_BLOCK, b0 + SLC_BLOCK) - np.maximum(c0, b0), 0, None) / CMP_BLOCK
    full = np.zeros((LANES, nc), np.float32)
    full[:n_sel, :n_cmp] = ov.T
    return full


def _layer(x, layer, attn_pre_norm, w_in, lq1, lk1, lq2, lk2, diff_subln, k_pos, k_w1, k_w2,
           v_pos, v_w1, v_w2, w_out, attn_post_norm, ffn_pre_norm, w_gate, w_up, w_down, ffn_post_norm):
    B, S, D = x.shape
    N = B * S
    d_ff = w_gate.shape[1]
    nc = S // CMP_STRIDE
    n_cmp = (S - CMP_BLOCK) // CMP_STRIDE + 1
    n_sel = S // SLC_BLOCK
    top_n = min(SLC_TOPK, n_sel)
    assert n_sel <= SLC_BLOCK and S % 512 == 0
    lam_init = 0.8 - 0.6 * math.exp(-0.3 * layer)
    dt = MXU_DTYPE

    hg, G, d = NSA_HEADS_PER_GROUP, NSA_KV_GROUPS, HEAD_DIM
    nq_perm = np.array([1536 + (g * hg + h) * d + e for h in range(hg) for g in range(G) for e in range(d)])
    cols = np.concatenate([np.arange(0, 1536), nq_perm,
                           np.arange(2304, 2816),
                           np.arange(2048, 2304),
                           np.arange(2816, 2840)])
    w_cat = jnp.pad(w_in[:, cols], ((0, 0), (0, LANES - hg * G * N_BRANCHES))).astype(dt)
    w_out_d = w_out[:512].astype(dt)
    w_out_n = w_out[nq_perm - 1536 + 512].astype(dt)
    cos_t, sin_t = _rope_tables(S)

    tm = 512
    seq_tiles = S // tm
    n_cols = (N_MAIN_SLABS + N_AUX_SLABS) * LANES
    p, aux = pl.pallas_call(
        functools.partial(_inproj_kernel, tm=tm, seq_tiles=seq_tiles),
        grid=(N // tm,),
        in_specs=[pl.BlockSpec((tm, D), lambda i: (i, 0)),
                  _resident((1, D)),
                  _resident((D, n_cols)),
                  pl.BlockSpec((tm, LANES), lambda i: (i % seq_tiles, 0)),
                  pl.BlockSpec((tm, LANES), lambda i: (i % seq_tiles, 0))],
        out_specs=[pl.BlockSpec((tm, N_P_SLABS * LANES), lambda i: (i, 0)),
                   pl.BlockSpec((tm, N_AUX_SLABS * LANES), lambda i: (i, 0))],
        out_shape=[jax.ShapeDtypeStruct((N, N_P_SLABS * LANES), dt),
                   jax.ShapeDtypeStruct((N, N_AUX_SLABS * LANES), F32)],
        compiler_params=_params(("parallel",)),
        name="inproj",
    )(x.reshape(N, D), attn_pre_norm.reshape(1, D), w_cat, cos_t, sin_t)
    p = p.reshape(B, S, N_P_SLABS * LANES)
    aux = aux.reshape(B, S, N_AUX_SLABS * LANES)

    eye_g = jnp.eye(G, dtype=F32)

    def pos_rows(pos):
        tiled = jnp.broadcast_to(pos.reshape(2, CMP_STRIDE, 1, d), (2, CMP_STRIDE, G, d))
        return tiled[0].reshape(1, -1), tiled[1].reshape(1, -1)

    def w1_blocks(w1):
        w = w1.reshape(2, CMP_STRIDE, d, CMP_HIDDEN)
        blk = jnp.einsum('pldj,gh->plgdhj', w, eye_g).reshape(2, CMP_STRIDE * G * d, G * CMP_HIDDEN)
        return blk[0].astype(dt), blk[1].astype(dt)

    def w2_blocks(w2):
        return jnp.einsum('jd,gh->gjhd', w2, eye_g).reshape(G * CMP_HIDDEN, G * d).astype(dt)

    kpl, kph = pos_rows(k_pos)
    vpl, vph = pos_rows(v_pos)
    kwl, kwh = w1_blocks(k_w1)
    vwl, vwh = w1_blocks(v_w1)
    chunk_w = CMP_STRIDE * G * d
    w1_spec = _resident((chunk_w, G * CMP_HIDDEN))
    w2_spec = _resident((G * CMP_HIDDEN, LANES))
    kcmp, vcmp = pl.pallas_call(
        functools.partial(_compress_kernel, nc=nc),
        grid=(B,),
        in_specs=[pl.BlockSpec((None, S, LANES), lambda b: (b, 0, 0)),
                  pl.BlockSpec((None, S, LANES), lambda b: (b, 0, 1)),
                  _resident((1, chunk_w)), _resident((1, chunk_w)), _resident((1, chunk_w)), _resident((1, chunk_w)),
                  w1_spec, w1_spec, w2_spec, w1_spec, w1_spec, w2_spec],
        out_specs=[pl.BlockSpec((None, nc, LANES), lambda b: (b, 0, 0)),
                   pl.BlockSpec((None, nc, LANES), lambda b: (b, 0, 0))],
        out_shape=[jax.ShapeDtypeStruct((B, nc, LANES), dt), jax.ShapeDtypeStruct((B, nc, LANES), dt)],
        compiler_params=_params(("parallel",)),
        name="compress",
    )(aux, aux, kpl, kph, vpl, vph, kwl, kwh, w2_blocks(k_w2), vwl, vwh, w2_blocks(v_w2))

    tq_d, tk_d = 512, 512
    lam_vec = [v.reshape(1, d) for v in (lq1, lk1, lq2, lk2)]
    o_diff = pl.pallas_call(
        functools.partial(_diff_kernel, tq=tq_d, tk=tk_d, lam_init=lam_init),
        grid=(B, DIFF_HEADS, S // tq_d),
        in_specs=[_resident((1, d))] * 4 + [_resident((1, LANES)),
                  pl.BlockSpec((None, tq_d, LANES), lambda b, h, i: (b, i, h)),
                  pl.BlockSpec((None, S, LANES), lambda b, h, i: (b, 0, 4 + h)),
                  pl.BlockSpec((None, S, LANES), lambda b, h, i: (b, 0, 8 + h))],
        out_specs=pl.BlockSpec((None, tq_d, LANES), lambda b, h, i: (b, i, h)),
        out_shape=jax.ShapeDtypeStruct((B, S, DIFF_HEADS * LANES), dt),
        scratch_shapes=[pltpu.VMEM((2 * tq_d, LANES), F32)] * 3,
        compiler_params=_params(("parallel", "parallel", "arbitrary")),
        name="diff_attn",
    )(*lam_vec, diff_subln.reshape(1, LANES), p, p, p)

    tq_n, tk_n = 256, 512
    ovt = jnp.asarray(_selection_overlap_t(nc, n_cmp, n_sel)).astype(dt)
    rows = NSA_HEADS_PER_GROUP * tq_n

    def seq_slab(c):
        return pl.BlockSpec((None, S, LANES), lambda b, i: (b, 0, c))

    o_nsa = pl.pallas_call(
        functools.partial(_nsa_kernel, tq=tq_n, tk=tk_n, nc=nc, n_sel=n_sel, top_n=top_n, n_q_tiles=S // tq_n),
        grid=(B, S // tq_n),
        in_specs=[pl.BlockSpec((None, tq_n, 4 * LANES), lambda b, i: (b, i, 3)),
                  pl.BlockSpec((None, tq_n, LANES), lambda b, i: (b, i, 2)),
                  pl.BlockSpec((None, nc, LANES), lambda b, i: (b, 0, 0)),
                  pl.BlockSpec((None, nc, LANES), lambda b, i: (b, 0, 0)),
                  _resident((LANES, nc)),
                  seq_slab(16), seq_slab(17), seq_slab(18), seq_slab(19), seq_slab(20), seq_slab(21),
                  seq_slab(22)],
        out_specs=pl.BlockSpec((None, tq_n, 4 * LANES), lambda b, i: (b, i, 0)),
        out_shape=jax.ShapeDtypeStruct((B, S, 4 * LANES), dt),
        scratch_shapes=[pltpu.VMEM((rows, LANES), F32)] * 2
        + [pltpu.VMEM((G, rows, LANES), dt), pltpu.VMEM((G, rows, LANES), F32)],
        compiler_params=_params(("parallel", "arbitrary")),
        name="nsa_attn",
    )(p, aux, kcmp, vcmp, ovt, p, p, p, p, p, p, p)

    tm2 = 512
    row = lambda i: (i, 0)
    out = pl.pallas_call(
        functools.partial(_post_kernel, d_ff=d_ff, ff_chunk=256),
        grid=(N // tm2,),
        in_specs=[pl.BlockSpec((tm2, D), row),
                  pl.BlockSpec((tm2, 512), row), pl.BlockSpec((tm2, 512), row),
                  _resident((512, D)), _resident((512, D)), _resident((1, D)), _resident((1, D)),
                  _resident((D, d_ff)), _resident((D, d_ff)), _resident((d_ff, D)), _resident((1, D))],
        out_specs=pl.BlockSpec((tm2, D), row),
        out_shape=jax.ShapeDtypeStruct((N, D), F32),
        scratch_shapes=[pltpu.VMEM((tm2, d_ff), dt)],
        compiler_params=_params(("parallel",)),
        name="post",
    )(x.reshape(N, D), o_diff.reshape(N, 512), o_nsa.reshape(N, 512), w_out_d, w_out_n,
      attn_post_norm.reshape(1, D), ffn_pre_norm.reshape(1, D),
      w_gate.astype(dt), w_up.astype(dt), w_down.astype(dt), ffn_post_norm.reshape(1, D))
    return out.reshape(B, S, D)


def kernel(x, attn_pre_norm, w_in, lambda_q1, lambda_k1, lambda_q2, lambda_k2, diff_subln, k_cmp_pos, k_cmp_w1, k_cmp_w2, v_cmp_pos, v_cmp_w1, v_cmp_w2, w_out, attn_post_norm, ffn_pre_norm, w_gate, w_up, w_down, ffn_post_norm):
    for l in range(w_in.shape[0]):
        x = _layer(x, l, attn_pre_norm[l], w_in[l], lambda_q1[l], lambda_k1[l], lambda_q2[l], lambda_k2[l],
                   diff_subln[l], k_cmp_pos[l], k_cmp_w1[l], k_cmp_w2[l], v_cmp_pos[l], v_cmp_w1[l], v_cmp_w2[l],
                   w_out[l], attn_post_norm[l], ffn_pre_norm[l], w_gate[l], w_up[l], w_down[l], ffn_post_norm[l])
    return x
```

```python
import functools
import math

import numpy as np
import jax
import jax.numpy as jnp
from jax import lax
from jax.experimental import pallas as pl
from jax.experimental.pallas import tpu as pltpu

F32 = jnp.float32
MXU_DTYPE = jnp.bfloat16

LANES = 128
SUBLANES = 8
HEAD_DIM = 64
ROPE_THETA = 10000.0
NORM_EPS = 1e-6
NEG_INF = -1e30
LOG2_E = 1.4426950408889634
FORCED_IMPORTANCE = 3e38

DIFF_HEADS = 4
NSA_HEADS_PER_GROUP = 4
NSA_KV_GROUPS = 2
CMP_BLOCK = 32
CMP_STRIDE = 16
CMP_HIDDEN = 4 * HEAD_DIM
SLC_BLOCK = 64
SLC_TOPK = 16
WINDOW = 512
N_BRANCHES = 3
SEQ_PARTS = 4

N_MAIN_SLABS = 20
N_AUX_SLABS = 3
N_P_SLABS = 23
VMEM_LIMIT = 56 * 1024 * 1024


def _nt_dot(a, b):
    return lax.dot_general(a, b, (((1,), (1,)), ((), ())), preferred_element_type=F32)


def _dot(a, b):
    return jnp.dot(a, b, preferred_element_type=F32)


def _rms(x, g):
    return x * lax.rsqrt(jnp.mean(x * x, axis=-1, keepdims=True) + NORM_EPS) * g


def _inproj_kernel(x_ref, g_ref, w_ref, cos_ref, sin_ref, p_ref, aux_ref, *, tm, seq_tiles):
    h = _rms(x_ref[...], g_ref[...]).astype(MXU_DTYPE)
    cos = cos_ref[...]
    sin = sin_ref[...]
    lane = lax.broadcasted_iota(jnp.int32, (tm, LANES), 1)
    low_half = (lane & (HEAD_DIM - 1)) < HEAD_DIM // 2

    def rope(y):
        fwd = pltpu.roll(y, HEAD_DIM // 2, 1)
        bwd = pltpu.roll(y, LANES - HEAD_DIM // 2, 1)
        return y * cos + jnp.where(low_half, bwd, fwd) * sin

    pos = (pl.program_id(0) % seq_tiles) * tm + lax.broadcasted_iota(jnp.int32, (tm, LANES), 0)
    blk = pos // SLC_BLOCK
    scale = HEAD_DIM ** -0.5 * LOG2_E

    n_slabs = N_MAIN_SLABS + N_AUX_SLABS
    for c0 in range(0, n_slabs, 2):
        c1 = min(c0 + 2, n_slabs)
        y2 = _dot(h, w_ref[:, c0 * LANES:c1 * LANES])
        for s in range(c0, c1):
            y = y2[:, (s - c0) * LANES:(s - c0 + 1) * LANES]
            if s < 4 or 12 <= s < 16:
                p_ref[:, s * LANES:(s + 1) * LANES] = (rope(y) * scale).astype(p_ref.dtype)
            elif s < 8:
                p_ref[:, s * LANES:(s + 1) * LANES] = rope(y).astype(p_ref.dtype)
            elif s < 12:
                p_ref[:, s * LANES:(s + 1) * LANES] = y.astype(p_ref.dtype)
            elif s == 16:
                r = rope(y)
                ind_hi = jnp.where(lane - HEAD_DIM == blk, 1.0, 0.0)
                ind_lo = jnp.where(lane == blk, 1.0, 0.0)
                p_ref[:, 16 * LANES:17 * LANES] = jnp.where(lane < HEAD_DIM, r, ind_hi).astype(p_ref.dtype)
                p_ref[:, 17 * LANES:18 * LANES] = jnp.where(lane >= HEAD_DIM, r, ind_lo).astype(p_ref.dtype)
            elif s == 17 or s == 19:
                o = 18 if s == 17 else 21
                p_ref[:, o * LANES:(o + 1) * LANES] = jnp.where(lane < HEAD_DIM, y, 1.0).astype(p_ref.dtype)
                p_ref[:, (o + 1) * LANES:(o + 2) * LANES] = jnp.where(lane >= HEAD_DIM, y, 1.0).astype(p_ref.dtype)
            elif s == 18:
                p_ref[:, 20 * LANES:21 * LANES] = rope(y).astype(p_ref.dtype)
            elif s == 20:
                aux_ref[:, 0:LANES] = rope(y)
            elif s == 21:
                aux_ref[:, LANES:2 * LANES] = y
            else:
                aux_ref[:, 2 * LANES:3 * LANES] = jax.nn.sigmoid(y)


def _compress_kernel(tk_ref, tv_ref, kpl_ref, kph_ref, vpl_ref, vph_ref,
                     kwl_ref, kwh_ref, kw2_ref, vwl_ref, vwh_ref, vw2_ref, ko_ref, vo_ref, *, nc):
    def compress(t_ref, plo_ref, phi_ref, wlo_ref, whi_ref, w2_ref):
        x = jnp.concatenate([t_ref[pl.ds(l, nc, stride=CMP_STRIDE), :] for l in range(CMP_STRIDE)], axis=1)
        a = _dot((x + plo_ref[...]).astype(MXU_DTYPE), wlo_ref[...])
        b = _dot((x + phi_ref[...]).astype(MXU_DTYPE), whi_ref[...])
        hid = a + pltpu.roll(b, nc - 1, 0)
        act = hid * jax.nn.sigmoid(hid)
        return _dot(act.astype(MXU_DTYPE), w2_ref[...])

    ko_ref[...] = compress(tk_ref, kpl_ref, kph_ref, kwl_ref, kwh_ref, kw2_ref).astype(ko_ref.dtype)
    vo_ref[...] = compress(tv_ref, vpl_ref, vph_ref, vwl_ref, vwh_ref, vw2_ref).astype(vo_ref.dtype)


def _flash_init(m_ref, l_ref, acc_ref):
    m_ref[...] = jnp.full(m_ref.shape, NEG_INF, F32)
    if l_ref is not None:
        l_ref[...] = jnp.zeros(l_ref.shape, F32)
    acc_ref[...] = jnp.zeros(acc_ref.shape, F32)


def _lane_tiles(x):
    return [x[:, c:c + LANES] for c in range(0, x.shape[1], LANES)]


def _stack_rows(x, n):
    return jnp.concatenate([x] * n, axis=0)


def _softmax_pv(s, v, m_ref, l_ref, acc_ref, bias=None):
    if bias is not None:
        s = s + bias
    n_tiles = s.shape[1] // LANES
    m_prev = m_ref[...]
    m_new = jnp.maximum(m_prev, jnp.max(s, axis=-1, keepdims=True))
    alpha = jnp.exp2(m_prev - m_new)
    x = s - jnp.concatenate([m_new] * n_tiles, axis=1)
    if l_ref is None:
        p = jnp.exp2(x.astype(MXU_DTYPE))
    else:
        p = jnp.exp2(x)
        l_ref[...] = alpha * l_ref[...] + functools.reduce(lambda a, b: a + b, _lane_tiles(p))
        p = p.astype(MXU_DTYPE)
    acc_ref[...] = alpha * acc_ref[...] + _dot(p, v)
    m_ref[...] = m_new


def _flash_finish(l_ref, acc_ref):
    if l_ref is None:
        return acc_ref[...]
    return acc_ref[...] / jnp.sum(l_ref[...], axis=-1, keepdims=True)


def _causal_flash(q, k_ref, v_ref, m_ref, l_ref, acc_ref, s0, tq, tk):
    assert tk in (tq, 2 * tq)
    rows = q.shape[0]
    _flash_init(m_ref, l_ref, acc_ref)

    def step(k0, width, bias=None):
        k0 = pl.multiple_of(k0, width)
        _softmax_pv(_nt_dot(q, k_ref[pl.ds(k0, width), :]), v_ref[pl.ds(k0, width), :],
                    m_ref, l_ref, acc_ref, bias)

    def pair(jj, carry):
        step(2 * jj * tk, tk)
        step((2 * jj + 1) * tk, tk)
        return carry

    n_full = s0 // tk
    lax.fori_loop(0, n_full // 2, pair, 0)

    def tail_bias(width):
        qpos = (width - tq) + lax.broadcasted_iota(jnp.int32, (tq, width), 0)
        causal = lax.broadcasted_iota(jnp.int32, (tq, width), 1) <= qpos
        return _stack_rows(jnp.where(causal, 0.0, NEG_INF), rows // tq)

    def tail(odd_tile, width):
        def body():
            if odd_tile:
                step((n_full - 1) * tk, tk)
            step(s0 + tq - width, width, tail_bias(width))
        return body

    aligned = (s0 - n_full * tk) == 0
    for odd_tile in (False, True):
        parity = (n_full % 2 == 1) if odd_tile else (n_full % 2 == 0)
        if tk == tq:
            pl.when(parity)(tail(odd_tile, tq))
        else:
            pl.when(parity & aligned)(tail(odd_tile, tq))
            pl.when(parity & jnp.logical_not(aligned))(tail(odd_tile, tk))

    return _flash_finish(l_ref, acc_ref)


def _diff_kernel(lq1_ref, lk1_ref, lq2_ref, lk2_ref, subln_ref, q_ref, k_ref, v_ref, o_ref,
                 m_ref, l_ref, acc_ref, *, tq, tk, lam_init):
    qi = pl.program_id(2)
    s0 = qi * tq
    lane = lax.broadcasted_iota(jnp.int32, (tq, LANES), 1)
    qf = q_ref[...].astype(F32)
    zero = jnp.zeros_like(qf)
    q2 = jnp.concatenate([jnp.where(lane < HEAD_DIM, qf, zero),
                          jnp.where(lane >= HEAD_DIM, qf, zero)], axis=0).astype(MXU_DTYPE)

    o12 = _causal_flash(q2, k_ref, v_ref, m_ref, l_ref, acc_ref, s0, tq, tk)
    lam = (jnp.exp(jnp.sum(lq1_ref[...] * lk1_ref[...], axis=-1, keepdims=True))
           - jnp.exp(jnp.sum(lq2_ref[...] * lk2_ref[...], axis=-1, keepdims=True)) + lam_init)
    o = o12[:tq] - lam * o12[tq:]
    o_ref[...] = (_rms(o, subln_ref[...]) * (1.0 - lam_init)).astype(o_ref.dtype)


def _nsa_kernel(q_ref, gate_ref, kc_ref, vc_ref, ovt_ref, ka0_ref, ka1_ref, vs0_ref, vs1_ref,
                kw_ref, vw0_ref, vw1_ref, o_ref, m_ref, acc_ref, qsel_ref, part_ref,
                *, tq, tk, nc, n_sel, top_n, n_q_tiles):
    hg = NSA_HEADS_PER_GROUP
    rows = hg * tq
    qi = pl.program_id(1)
    s0 = qi * tq
    lane = lax.broadcasted_iota(jnp.int32, (tq, LANES), 1)
    lane_rows = lax.broadcasted_iota(jnp.int32, (rows, LANES), 1)
    qf = [q_ref[:, h * LANES:(h + 1) * LANES].astype(F32) for h in range(hg)]
    gates = gate_ref[...]
    zero = jnp.zeros((tq, LANES), F32)

    wk = WINDOW + tq
    w0 = pl.multiple_of(jnp.maximum(s0 - WINDOW, 0), tq)
    back = (s0 - w0) + lax.broadcasted_iota(jnp.int32, (tq, wk), 0) \
        - lax.broadcasted_iota(jnp.int32, (tq, wk), 1)
    win_bias = _stack_rows(jnp.where((back >= 0) & (back < WINDOW), 0.0, NEG_INF), hg)

    def normalise(raw, g):
        in_half_rows = (lane_rows >= HEAD_DIM) if g else (lane_rows < HEAD_DIM)
        return raw / jnp.where(in_half_rows, pltpu.roll(raw, HEAD_DIM, 1), 1.0)

    def gate(g, h, branch):
        c = (g * hg + h) * N_BRANCHES + branch
        return gates[:, c:c + 1]

    def phase1(n_blk, nc_eff):
        n_idx = lax.broadcasted_iota(jnp.int32, (tq, nc_eff), 1)
        qpos_c = s0 + lax.broadcasted_iota(jnp.int32, (tq, nc_eff), 0)
        cmp_ok = n_idx * CMP_STRIDE + (CMP_BLOCK - 1) <= qpos_c
        cmp_bias = _stack_rows(jnp.where(cmp_ok, 0.0, NEG_INF), hg)
        cmp_keep = _stack_rows(jnp.where(cmp_ok, 1.0, 0.0), hg)
        jb = lax.broadcasted_iota(jnp.int32, (n_blk, tq), 0)
        qp = s0 + lax.broadcasted_iota(jnp.int32, (n_blk, tq), 1)
        cur = qp // SLC_BLOCK
        valid = jb * SLC_BLOCK <= qp
        forced = (jb == 0) | (jb == cur) | (jb == cur - 1)

        for g in range(NSA_KV_GROUPS):
            in_half = (lane >= HEAD_DIM) if g else (lane < HEAD_DIM)
            q_plain = jnp.concatenate([jnp.where(in_half, qf[h], zero) for h in range(hg)],
                                      axis=0).astype(MXU_DTYPE)

            sc = _nt_dot(q_plain, kc_ref[:nc_eff, :]) + cmp_bias
            pc = jnp.exp2(sc - jnp.max(sc, axis=-1, keepdims=True)) * cmp_keep
            lc = jnp.sum(pc, axis=-1, keepdims=True)
            pc = (pc / jnp.where(lc > 0.0, lc, 1.0)).astype(MXU_DTYPE)
            o_cmp = _dot(pc, vc_ref[:nc_eff, :])

            if n_blk <= top_n:
                selected = valid
            else:
                imp_t = None
                for h in range(hg):
                    part = _nt_dot(ovt_ref[:, :nc_eff], pc[h * tq:(h + 1) * tq])
                    imp_t = part if imp_t is None else imp_t + part
                key = jnp.where(forced, FORCED_IMPORTANCE, jnp.where(valid, imp_t[:n_blk], -1.0))
                n_grp = n_blk // SUBLANES
                key_g = [key[a * SUBLANES:(a + 1) * SUBLANES] for a in range(n_grp)]
                jb_g = lax.broadcasted_iota(jnp.int32, (SUBLANES, tq), 0)
                rank_g = [jnp.zeros((SUBLANES, tq), jnp.int32) for _ in range(n_grp)]
                for i in range(n_blk):
                    ri = key[i:i + 1, :]
                    for a in range(n_grp):
                        if a < i // SUBLANES:
                            ahead = jnp.where(ri > key_g[a], 1, 0)
                        elif a > i // SUBLANES:
                            ahead = jnp.where(ri >= key_g[a], 1, 0)
                        else:
                            ahead = (jnp.where(ri > key_g[a], 1, 0)
                                     + jnp.where((ri == key_g[a]) & (jb_g > i % SUBLANES), 1, 0))
                        rank_g[a] = rank_g[a] + ahead
                selected = (jnp.concatenate(rank_g, axis=0) < top_n) & valid
            bias_t = jnp.where(selected, 0.0, NEG_INF)
            if n_blk < SLC_BLOCK:
                bias_t = jnp.concatenate([bias_t, jnp.full((SLC_BLOCK - n_blk, tq), NEG_INF, F32)], axis=0)
            pad_t = jnp.zeros((SLC_BLOCK, tq), F32)
            bias = jnp.transpose(jnp.concatenate([pad_t, bias_t] if g == 0 else [bias_t, pad_t], axis=0))
            qsel_ref[g] = jnp.concatenate([jnp.where(in_half, qf[h], bias) for h in range(hg)],
                                          axis=0).astype(qsel_ref.dtype)

            sw = _nt_dot(q_plain, kw_ref[pl.ds(w0, wk), :]) + win_bias
            pw = jnp.exp2((sw - jnp.max(sw, axis=-1, keepdims=True)).astype(MXU_DTYPE))
            o_win = normalise(_dot(pw, (vw1_ref if g else vw0_ref)[pl.ds(w0, wk), :]), g)
            part_ref[g] = jnp.concatenate(
                [gate(g, h, 0) * o_cmp[h * tq:(h + 1) * tq] + gate(g, h, 2) * o_win[h * tq:(h + 1) * tq]
                 for h in range(hg)], axis=0)

    tiles_per_part = n_q_tiles // SEQ_PARTS
    for c in range(SEQ_PARTS):
        nc_part = min(nc, -(-(nc * (c + 1) // SEQ_PARTS) // LANES) * LANES)
        pl.when(qi // tiles_per_part == c)(
            functools.partial(phase1, n_sel * (c + 1) // SEQ_PARTS, nc_part))

    outs = []
    for g in range(NSA_KV_GROUPS):
        o_slc = normalise(_causal_flash(qsel_ref[g], ka1_ref if g else ka0_ref, vs1_ref if g else vs0_ref,
                                        m_ref, None, acc_ref, s0, tq, tk), g)
        outs.append([part_ref[g, h * tq:(h + 1) * tq, :] + gate(g, h, 1) * o_slc[h * tq:(h + 1) * tq]
                     for h in range(hg)])

    for h in range(hg):
        o_ref[:, h * LANES:(h + 1) * LANES] = jnp.where(lane < HEAD_DIM, outs[0][h], outs[1][h]).astype(o_ref.dtype)


def _post_kernel(x_ref, od_ref, on_ref, wod_ref, won_ref, gpost_ref, gpre_ref, wg_ref, wu_ref, wd_ref,
                 gffn_ref, o_ref, act_ref, *, d_ff, ff_chunk):
    mix = _dot(od_ref[...], wod_ref[...]) + _dot(on_ref[...], won_ref[...])
    x1 = x_ref[...] + _rms(mix, gpost_ref[...])
    h = _rms(x1, gpre_ref[...]).astype(MXU_DTYPE)
    for c in range(0, d_ff, ff_chunk):
        gate = _dot(h, wg_ref[:, c:c + ff_chunk])
        up = _dot(h, wu_ref[:, c:c + ff_chunk])
        act_ref[:, c:c + ff_chunk] = (gate * jax.nn.sigmoid(gate) * up).astype(act_ref.dtype)
    f = _dot(act_ref[...], wd_ref[...])
    o_ref[...] = x1 + _rms(f, gffn_ref[...])


def _resident(shape):
    nd = len(shape)
    return pl.BlockSpec(shape, lambda *_: (0,) * nd, pipeline_mode=pl.Buffered(1))


def _params(sem):
    return pltpu.CompilerParams(dimension_semantics=sem, vmem_limit_bytes=VMEM_LIMIT)


def _rope_tables(S):
    inv = 1.0 / (ROPE_THETA ** (jnp.arange(0, HEAD_DIM, 2, dtype=F32) / HEAD_DIM))
    ang = jnp.arange(S, dtype=F32)[:, None] * inv[None, :]
    cos, sin = jnp.cos(ang), jnp.sin(ang)
    return jnp.tile(cos, (1, 4)), jnp.concatenate([-sin, sin, -sin, sin], axis=1)


def _selection_overlap_t(nc, n_cmp, n_sel):
    c0 = np.arange(n_cmp)[:, None] * CMP_STRIDE
    b0 = np.arange(n_sel)[None, :] * SLC_BLOCK
    ov = np.clip(np.minimum(c0 + CMP_BLOCK, b0 + SLC_BLOCK) - np.maximum(c0, b0), 0, None) / CMP_BLOCK
    full = np.zeros((LANES, nc), np.float32)
    full[:n_sel, :n_cmp] = ov.T
    return full


def _layer(x, layer, attn_pre_norm, w_in, lq1, lk1, lq2, lk2, diff_subln, k_pos, k_w1, k_w2,
           v_pos, v_w1, v_w2, w_out, attn_post_norm, ffn_pre_norm, w_gate, w_up, w_down, ffn_post_norm):
    B, S, D = x.shape
    N = B * S
    d_ff = w_gate.shape[1]
    nc = S // CMP_STRIDE
    n_cmp = (S - CMP_BLOCK) // CMP_STRIDE + 1
    n_sel = S // SLC_BLOCK
    top_n = min(SLC_TOPK, n_sel)
    assert n_sel <= SLC_BLOCK and S % 512 == 0
    lam_init = 0.8 - 0.6 * math.exp(-0.3 * layer)
    dt = MXU_DTYPE

    hg, G, d = NSA_HEADS_PER_GROUP, NSA_KV_GROUPS, HEAD_DIM
    head_starts = [(g * hg + h) * d for h in range(hg) for g in range(G)]
    w_cat = jnp.concatenate(
        [w_in[:, :1536]] + [w_in[:, 1536 + c:1536 + c + d] for c in head_starts]
        + [w_in[:, 2304:2816],
           w_in[:, 2048:2304],
           w_in[:, 2816:2840],
           jnp.zeros((D, LANES - hg * G * N_BRANCHES), w_in.dtype)], axis=1).astype(dt)
    w_out_d = w_out[:512].astype(dt)
    w_out_n = jnp.concatenate([w_out[512 + c:512 + c + d] for c in head_starts], axis=0).astype(dt)
    cos_t, sin_t = _rope_tables(S)

    tm = 512
    seq_tiles = S // tm
    n_cols = (N_MAIN_SLABS + N_AUX_SLABS) * LANES
    p, aux = pl.pallas_call(
        functools.partial(_inproj_kernel, tm=tm, seq_tiles=seq_tiles),
        grid=(N // tm,),
        in_specs=[pl.BlockSpec((tm, D), lambda i: (i, 0)),
                  _resident((1, D)),
                  _resident((D, n_cols)),
                  pl.BlockSpec((tm, LANES), lambda i: (i % seq_tiles, 0)),
                  pl.BlockSpec((tm, LANES), lambda i: (i % seq_tiles, 0))],
        out_specs=[pl.BlockSpec((tm, N_P_SLABS * LANES), lambda i: (i, 0)),
                   pl.BlockSpec((tm, N_AUX_SLABS * LANES), lambda i: (i, 0))],
        out_shape=[jax.ShapeDtypeStruct((N, N_P_SLABS * LANES), dt),
                   jax.ShapeDtypeStruct((N, N_AUX_SLABS * LANES), F32)],
        compiler_params=_params(("parallel",)),
        name="inproj",
    )(x.reshape(N, D), attn_pre_norm.reshape(1, D), w_cat, cos_t, sin_t)
    p = p.reshape(B, S, N_P_SLABS * LANES)
    aux = aux.reshape(B, S, N_AUX_SLABS * LANES)

    eye_g = jnp.eye(G, dtype=F32)

    def pos_rows(pos):
        tiled = jnp.broadcast_to(pos.reshape(2, CMP_STRIDE, 1, d), (2, CMP_STRIDE, G, d))
        return tiled[0].reshape(1, -1), tiled[1].reshape(1, -1)

    def w1_blocks(w1):
        w = w1.reshape(2, CMP_STRIDE, d, CMP_HIDDEN)
        blk = jnp.einsum('pldj,gh->plgdhj', w, eye_g).reshape(2, CMP_STRIDE * G * d, G * CMP_HIDDEN)
        return blk[0].astype(dt), blk[1].astype(dt)

    def w2_blocks(w2):
        return jnp.einsum('jd,gh->gjhd', w2, eye_g).reshape(G * CMP_HIDDEN, G * d).astype(dt)

    kpl, kph = pos_rows(k_pos)
    vpl, vph = pos_rows(v_pos)
    kwl, kwh = w1_blocks(k_w1)
    vwl, vwh = w1_blocks(v_w1)
    chunk_w = CMP_STRIDE * G * d
    w1_spec = _resident((chunk_w, G * CMP_HIDDEN))
    w2_spec = _resident((G * CMP_HIDDEN, LANES))
    kcmp, vcmp = pl.pallas_call(
        functools.partial(_compress_kernel, nc=nc),
        grid=(B,),
        in_specs=[pl.BlockSpec((None, S, LANES), lambda b: (b, 0, 0)),
                  pl.BlockSpec((None, S, LANES), lambda b: (b, 0, 1)),
                  _resident((1, chunk_w)), _resident((1, chunk_w)), _resident((1, chunk_w)), _resident((1, chunk_w)),
                  w1_spec, w1_spec, w2_spec, w1_spec, w1_spec, w2_spec],
        out_specs=[pl.BlockSpec((None, nc, LANES), lambda b: (b, 0, 0)),
                   pl.BlockSpec((None, nc, LANES), lambda b: (b, 0, 0))],
        out_shape=[jax.ShapeDtypeStruct((B, nc, LANES), dt), jax.ShapeDtypeStruct((B, nc, LANES), dt)],
        compiler_params=_params(("parallel",)),
        name="compress",
    )(aux, aux, kpl, kph, vpl, vph, kwl, kwh, w2_blocks(k_w2), vwl, vwh, w2_blocks(v_w2))

    tq_d, tk_d = 512, 512
    lam_vec = [v.reshape(1, d) for v in (lq1, lk1, lq2, lk2)]
    o_diff = pl.pallas_call(
        functools.partial(_diff_kernel, tq=tq_d, tk=tk_d, lam_init=lam_init),
        grid=(B, DIFF_HEADS, S // tq_d),
        in_specs=[_resident((1, d))] * 4 + [_resident((1, LANES)),
                  pl.BlockSpec((None, tq_d, LANES), lambda b, h, i: (b, i, h)),
                  pl.BlockSpec((None, S, LANES), lambda b, h, i: (b, 0, 4 + h)),
                  pl.BlockSpec((None, S, LANES), lambda b, h, i: (b, 0, 8 + h))],
        out_specs=pl.BlockSpec((None, tq_d, LANES), lambda b, h, i: (b, i, h)),
        out_shape=jax.ShapeDtypeStruct((B, S, DIFF_HEADS * LANES), dt),
        scratch_shapes=[pltpu.VMEM((2 * tq_d, LANES), F32)] * 3,
        compiler_params=_params(("parallel", "parallel", "arbitrary")),
        name="diff_attn",
    )(*lam_vec, diff_subln.reshape(1, LANES), p, p, p)

    tq_n, tk_n = 256, 512
    ovt = jnp.asarray(_selection_overlap_t(nc, n_cmp, n_sel)).astype(dt)
    rows = NSA_HEADS_PER_GROUP * tq_n

    def seq_slab(c):
        return pl.BlockSpec((None, S, LANES), lambda b, i: (b, 0, c))

    o_nsa = pl.pallas_call(
        functools.partial(_nsa_kernel, tq=tq_n, tk=tk_n, nc=nc, n_sel=n_sel, top_n=top_n, n_q_tiles=S // tq_n),
        grid=(B, S // tq_n),
        in_specs=[pl.BlockSpec((None, tq_n, 4 * LANES), lambda b, i: (b, i, 3)),
                  pl.BlockSpec((None, tq_n, LANES), lambda b, i: (b, i, 2)),
                  pl.BlockSpec((None, nc, LANES), lambda b, i: (b, 0, 0)),
                  pl.BlockSpec((None, nc, LANES), lambda b, i: (b, 0, 0)),
                  _resident((LANES, nc)),
                  seq_slab(16), seq_slab(17), seq_slab(18), seq_slab(19), seq_slab(20), seq_slab(21),
                  seq_slab(22)],
        out_specs=pl.BlockSpec((None, tq_n, 4 * LANES), lambda b, i: (b, i, 0)),
        out_shape=jax.ShapeDtypeStruct((B, S, 4 * LANES), dt),
        scratch_shapes=[pltpu.VMEM((rows, LANES), F32)] * 2
        + [pltpu.VMEM((G, rows, LANES), dt), pltpu.VMEM((G, rows, LANES), F32)],
        compiler_params=_params(("parallel", "arbitrary")),
        name="nsa_attn",
    )(p, aux, kcmp, vcmp, ovt, p, p, p, p, p, p, p)

    tm2 = 512
    row = lambda i: (i, 0)
    out = pl.pallas_call(
        functools.partial(_post_kernel, d_ff=d_ff, ff_chunk=256),
        grid=(N // tm2,),
        in_specs=[pl.BlockSpec((tm2, D), row),
                  pl.BlockSpec((tm2, 512), row), pl.BlockSpec((tm2, 512), row),
                  _resident((512, D)), _resident((512, D)), _resident((1, D)), _resident((1, D)),
                  _resident((D, d_ff)), _resident((D, d_ff)), _resident((d_ff, D)), _resident((1, D))],
        out_specs=pl.BlockSpec((tm2, D), row),
        out_shape=jax.ShapeDtypeStruct((N, D), F32),
        scratch_shapes=[pltpu.VMEM((tm2, d_ff), dt)],
        compiler_params=_params(("parallel",)),
        name="post",
    )(x.reshape(N, D), o_diff.reshape(N, 512), o_nsa.reshape(N, 512), w_out_d, w_out_n,
      attn_post_norm.reshape(1, D), ffn_pre_norm.reshape(1, D),
      w_gate.astype(dt), w_up.astype(dt), w_down.astype(dt), ffn_post_norm.reshape(1, D))
    return out.reshape(B, S, D)


def kernel(x, attn_pre_norm, w_in, lambda_q1, lambda_k1, lambda_q2, lambda_k2, diff_subln, k_cmp_pos, k_cmp_w1, k_cmp_w2, v_cmp_pos, v_cmp_w1, v_cmp_w2, w_out, attn_post_norm, ffn_pre_norm, w_gate, w_up, w_down, ffn_post_norm):
    for l in range(w_in.shape[0]):
        x = _layer(x, l, attn_pre_norm[l], w_in[l], lambda_q1[l], lambda_k1[l], lambda_q2[l], lambda_k2[l],
                   diff_subln[l], k_cmp_pos[l], k_cmp_w1[l], k_cmp_w2[l], v_cmp_pos[l], v_cmp_w1[l], v_cmp_w2[l],
                   w_out[l], attn_post_norm[l], ffn_pre_norm[l], w_gate[l], w_up[l], w_down[l], ffn_post_norm[l])
    return x
```

```python
import functools
import math

import numpy as np
import jax
import jax.numpy as jnp
from jax import lax
from jax.experimental import pallas as pl
from jax.experimental.pallas import tpu as pltpu

F32 = jnp.float32
MXU_DTYPE = jnp.bfloat16

LANES = 128
SUBLANES = 8
HEAD_DIM = 64
ROPE_THETA = 10000.0
NORM_EPS = 1e-6
NEG_INF = -1e30
LOG2_E = 1.4426950408889634
FORCED_IMPORTANCE = 3e38

DIFF_HEADS = 4
NSA_HEADS_PER_GROUP = 4
NSA_KV_GROUPS = 2
CMP_BLOCK = 32
CMP_STRIDE = 16
CMP_HIDDEN = 4 * HEAD_DIM
SLC_BLOCK = 64
SLC_TOPK = 16
WINDOW = 512
N_BRANCHES = 3
SEQ_PARTS = 4

N_MAIN_SLABS = 20
N_AUX_SLABS = 3
N_P_SLABS = 23
VMEM_LIMIT = 56 * 1024 * 1024


def _nt_dot(a, b):
    return lax.dot_general(a, b, (((1,), (1,)), ((), ())), preferred_element_type=F32)


def _dot(a, b):
    return jnp.dot(a, b, preferred_element_type=F32)


def _rms(x, g):
    return x * lax.rsqrt(jnp.mean(x * x, axis=-1, keepdims=True) + NORM_EPS) * g


def _inproj_kernel(x_ref, g_ref, w_ref, cos_ref, sin_ref, p_ref, aux_ref, *, tm, seq_tiles):
    h = _rms(x_ref[...], g_ref[...]).astype(MXU_DTYPE)
    cos = cos_ref[...]
    sin = sin_ref[...]
    lane = lax.broadcasted_iota(jnp.int32, (tm, LANES), 1)
    low_half = (lane & (HEAD_DIM - 1)) < HEAD_DIM // 2

    def rope(y):
        fwd = pltpu.roll(y, HEAD_DIM // 2, 1)
        bwd = pltpu.roll(y, LANES - HEAD_DIM // 2, 1)
        return y * cos + jnp.where(low_half, bwd, fwd) * sin

    pos = (pl.program_id(0) % seq_tiles) * tm + lax.broadcasted_iota(jnp.int32, (tm, LANES), 0)
    blk = pos // SLC_BLOCK
    scale = HEAD_DIM ** -0.5 * LOG2_E

    n_slabs = N_MAIN_SLABS + N_AUX_SLABS
    for c0 in range(0, n_slabs, 2):
        c1 = min(c0 + 2, n_slabs)
        y2 = _dot(h, w_ref[:, c0 * LANES:c1 * LANES])
        for s in range(c0, c1):
            y = y2[:, (s - c0) * LANES:(s - c0 + 1) * LANES]
            if s < 4 or 12 <= s < 16:
                p_ref[:, s * LANES:(s + 1) * LANES] = (rope(y) * scale).astype(p_ref.dtype)
            elif s < 8:
                p_ref[:, s * LANES:(s + 1) * LANES] = rope(y).astype(p_ref.dtype)
            elif s < 12:
                p_ref[:, s * LANES:(s + 1) * LANES] = y.astype(p_ref.dtype)
            elif s == 16:
                r = rope(y)
                ind_hi = jnp.where(lane - HEAD_DIM == blk, 1.0, 0.0)
                ind_lo = jnp.where(lane == blk, 1.0, 0.0)
                p_ref[:, 16 * LANES:17 * LANES] = jnp.where(lane < HEAD_DIM, r, ind_hi).astype(p_ref.dtype)
                p_ref[:, 17 * LANES:18 * LANES] = jnp.where(lane >= HEAD_DIM, r, ind_lo).astype(p_ref.dtype)
            elif s == 17 or s == 19:
                o = 18 if s == 17 else 21
                p_ref[:, o * LANES:(o + 1) * LANES] = jnp.where(lane < HEAD_DIM, y, 1.0).astype(p_ref.dtype)
                p_ref[:, (o + 1) * LANES:(o + 2) * LANES] = jnp.where(lane >= HEAD_DIM, y, 1.0).astype(p_ref.dtype)
            elif s == 18:
                p_ref[:, 20 * LANES:21 * LANES] = rope(y).astype(p_ref.dtype)
            elif s == 20:
                aux_ref[:, 0:LANES] = rope(y)
            elif s == 21:
                aux_ref[:, LANES:2 * LANES] = y
            else:
                aux_ref[:, 2 * LANES:3 * LANES] = jax.nn.sigmoid(y)


def _compress_kernel(tk_ref, tv_ref, kpl_ref, kph_ref, vpl_ref, vph_ref,
                     kwl_ref, kwh_ref, kw2_ref, vwl_ref, vwh_ref, vw2_ref, ko_ref, vo_ref, *, nc):
    def compress(t_ref, plo_ref, phi_ref, wlo_ref, whi_ref, w2_ref):
        x = jnp.concatenate([t_ref[pl.ds(l, nc, stride=CMP_STRIDE), :] for l in range(CMP_STRIDE)], axis=1)
        a = _dot((x + plo_ref[...]).astype(MXU_DTYPE), wlo_ref[...])
        b = _dot((x + phi_ref[...]).astype(MXU_DTYPE), whi_ref[...])
        hid = a + pltpu.roll(b, nc - 1, 0)
        act = hid * jax.nn.sigmoid(hid)
        return _dot(act.astype(MXU_DTYPE), w2_ref[...])

    ko_ref[...] = compress(tk_ref, kpl_ref, kph_ref, kwl_ref, kwh_ref, kw2_ref).astype(ko_ref.dtype)
    vo_ref[...] = compress(tv_ref, vpl_ref, vph_ref, vwl_ref, vwh_ref, vw2_ref).astype(vo_ref.dtype)


def _flash_init(m_ref, l_ref, acc_ref):
    m_ref[...] = jnp.full(m_ref.shape, NEG_INF, F32)
    if l_ref is not None:
        l_ref[...] = jnp.zeros(l_ref.shape, F32)
    acc_ref[...] = jnp.zeros(acc_ref.shape, F32)


def _lane_tiles(x):
    return [x[:, c:c + LANES] for c in range(0, x.shape[1], LANES)]


def _stack_rows(x, n):
    return jnp.concatenate([x] * n, axis=0)


def _softmax_pv(s, v, m_ref, l_ref, acc_ref, bias=None):
    if bias is not None:
        s = s + bias
    n_tiles = s.shape[1] // LANES
    m_prev = m_ref[...]
    m_new = jnp.maximum(m_prev, jnp.max(s, axis=-1, keepdims=True))
    alpha = jnp.exp2(m_prev - m_new)
    x = s - jnp.concatenate([m_new] * n_tiles, axis=1)
    if l_ref is None:
        p = jnp.exp2(x.astype(MXU_DTYPE))
    else:
        p = jnp.exp2(x)
        l_ref[...] = alpha * l_ref[...] + functools.reduce(lambda a, b: a + b, _lane_tiles(p))
        p = p.astype(MXU_DTYPE)
    acc_ref[...] = alpha * acc_ref[...] + _dot(p, v)
    m_ref[...] = m_new


def _flash_finish(l_ref, acc_ref):
    if l_ref is None:
        return acc_ref[...]
    return acc_ref[...] / jnp.sum(l_ref[...], axis=-1, keepdims=True)


def _causal_flash(q, k_ref, v_ref, m_ref, l_ref, acc_ref, s0, tq, tk):
    assert tk in (tq, 2 * tq)
    rows = q.shape[0]
    _flash_init(m_ref, l_ref, acc_ref)

    def step(k0, width, bias=None):
        k0 = pl.multiple_of(k0, width)
        _softmax_pv(_nt_dot(q, k_ref[pl.ds(k0, width), :]), v_ref[pl.ds(k0, width), :],
                    m_ref, l_ref, acc_ref, bias)

    def pair(jj, carry):
        step(2 * jj * tk, tk)
        step((2 * jj + 1) * tk, tk)
        return carry

    n_full = s0 // tk
    lax.fori_loop(0, n_full // 2, pair, 0)

    def tail_bias(width):
        qpos = (width - tq) + lax.broadcasted_iota(jnp.int32, (tq, width), 0)
        causal = lax.broadcasted_iota(jnp.int32, (tq, width), 1) <= qpos
        return _stack_rows(jnp.where(causal, 0.0, NEG_INF), rows // tq)

    def tail(odd_tile, width):
        def body():
            if odd_tile:
                step((n_full - 1) * tk, tk)
            step(s0 + tq - width, width, tail_bias(width))
        return body

    if tk == tq:
        pl.when(n_full % 2 == 1)(lambda: step((n_full - 1) * tk, tk))
        step(s0, tq, tail_bias(tq))
        return _flash_finish(l_ref, acc_ref)

    aligned = (s0 - n_full * tk) == 0
    for odd_tile in (False, True):
        parity = (n_full % 2 == 1) if odd_tile else (n_full % 2 == 0)
        pl.when(parity & aligned)(tail(odd_tile, tq))
        pl.when(parity & jnp.logical_not(aligned))(tail(odd_tile, tk))
    return _flash_finish(l_ref, acc_ref)


def _diff_kernel(lq1_ref, lk1_ref, lq2_ref, lk2_ref, subln_ref, q_ref, k_ref, v_ref, o_ref,
                 m_ref, l_ref, acc_ref, *, tq, tk, lam_init):
    qi = pl.program_id(2)
    s0 = qi * tq
    lane = lax.broadcasted_iota(jnp.int32, (tq, LANES), 1)
    qf = q_ref[...].astype(F32)
    zero = jnp.zeros_like(qf)
    q2 = jnp.concatenate([jnp.where(lane < HEAD_DIM, qf, zero),
                          jnp.where(lane >= HEAD_DIM, qf, zero)], axis=0).astype(MXU_DTYPE)

    o12 = _causal_flash(q2, k_ref, v_ref, m_ref, l_ref, acc_ref, s0, tq, tk)
    lam = (jnp.exp(jnp.sum(lq1_ref[...] * lk1_ref[...], axis=-1, keepdims=True))
           - jnp.exp(jnp.sum(lq2_ref[...] * lk2_ref[...], axis=-1, keepdims=True)) + lam_init)
    o = o12[:tq] - lam * o12[tq:]
    o_ref[...] = (_rms(o, subln_ref[...]) * (1.0 - lam_init)).astype(o_ref.dtype)


def _nsa_kernel(q_ref, gate_ref, kc_ref, vc_ref, ovt_ref, ka0_ref, ka1_ref, vs0_ref, vs1_ref,
                kw_ref, vw0_ref, vw1_ref, o_ref, m_ref, acc_ref, qsel_ref, part_ref,
                *, tq, tk, nc, n_sel, top_n, n_q_tiles):
    hg = NSA_HEADS_PER_GROUP
    rows = hg * tq
    qi = pl.program_id(1)
    s0 = qi * tq
    lane = lax.broadcasted_iota(jnp.int32, (tq, LANES), 1)
    lane_rows = lax.broadcasted_iota(jnp.int32, (rows, LANES), 1)
    qf = [q_ref[:, h * LANES:(h + 1) * LANES].astype(F32) for h in range(hg)]
    gates = gate_ref[...]
    zero = jnp.zeros((tq, LANES), F32)

    wk = WINDOW + tq
    w0 = pl.multiple_of(jnp.maximum(s0 - WINDOW, 0), tq)
    back = (s0 - w0) + lax.broadcasted_iota(jnp.int32, (tq, wk), 0) \
        - lax.broadcasted_iota(jnp.int32, (tq, wk), 1)
    win_bias = _stack_rows(jnp.where((back >= 0) & (back < WINDOW), 0.0, NEG_INF), hg)

    def normalise(raw, g):
        in_half_rows = (lane_rows >= HEAD_DIM) if g else (lane_rows < HEAD_DIM)
        return raw / jnp.where(in_half_rows, pltpu.roll(raw, HEAD_DIM, 1), 1.0)

    def gate(g, h, branch):
        c = (g * hg + h) * N_BRANCHES + branch
        return gates[:, c:c + 1]

    def phase1(n_blk, nc_eff):
        n_idx = lax.broadcasted_iota(jnp.int32, (tq, nc_eff), 1)
        qpos_c = s0 + lax.broadcasted_iota(jnp.int32, (tq, nc_eff), 0)
        cmp_ok = n_idx * CMP_STRIDE + (CMP_BLOCK - 1) <= qpos_c
        cmp_bias = _stack_rows(jnp.where(cmp_ok, 0.0, NEG_INF), hg)
        cmp_keep = _stack_rows(jnp.where(cmp_ok, 1.0, 0.0), hg)
        jb = lax.broadcasted_iota(jnp.int32, (n_blk, tq), 0)
        qp = s0 + lax.broadcasted_iota(jnp.int32, (n_blk, tq), 1)
        cur = qp // SLC_BLOCK
        valid = jb * SLC_BLOCK <= qp
        forced = (jb == 0) | (jb == cur) | (jb == cur - 1)

        for g in range(NSA_KV_GROUPS):
            in_half = (lane >= HEAD_DIM) if g else (lane < HEAD_DIM)
            q_plain = jnp.concatenate([jnp.where(in_half, qf[h], zero) for h in range(hg)],
                                      axis=0).astype(MXU_DTYPE)

            sc = _nt_dot(q_plain, kc_ref[:nc_eff, :]) + cmp_bias
            pc = jnp.exp2(sc - jnp.max(sc, axis=-1, keepdims=True)) * cmp_keep
            lc = jnp.sum(pc, axis=-1, keepdims=True)
            pc = (pc / jnp.where(lc > 0.0, lc, 1.0)).astype(MXU_DTYPE)
            o_cmp = _dot(pc, vc_ref[:nc_eff, :])

            if n_blk <= top_n:
                selected = valid
            else:
                imp_t = None
                for h in range(hg):
                    part = _nt_dot(ovt_ref[:, :nc_eff], pc[h * tq:(h + 1) * tq])
                    imp_t = part if imp_t is None else imp_t + part
                key = jnp.where(forced, FORCED_IMPORTANCE, jnp.where(valid, imp_t[:n_blk], -1.0))
                n_grp = n_blk // SUBLANES
                key_g = [key[a * SUBLANES:(a + 1) * SUBLANES] for a in range(n_grp)]
                jb_g = lax.broadcasted_iota(jnp.int32, (SUBLANES, tq), 0)
                rank_g = [jnp.zeros((SUBLANES, tq), jnp.int32) for _ in range(n_grp)]
                for i in range(n_blk):
                    ri = key[i:i + 1, :]
                    for a in range(n_grp):
                        if a < i // SUBLANES:
                            ahead = jnp.where(ri > key_g[a], 1, 0)
                        elif a > i // SUBLANES:
                            ahead = jnp.where(ri >= key_g[a], 1, 0)
                        else:
                            ahead = (jnp.where(ri > key_g[a], 1, 0)
                                     + jnp.where((ri == key_g[a]) & (jb_g > i % SUBLANES), 1, 0))
                        rank_g[a] = rank_g[a] + ahead
                selected = (jnp.concatenate(rank_g, axis=0) < top_n) & valid
            bias_t = jnp.where(selected, 0.0, NEG_INF)
            if n_blk < SLC_BLOCK:
                bias_t = jnp.concatenate([bias_t, jnp.full((SLC_BLOCK - n_blk, tq), NEG_INF, F32)], axis=0)
            pad_t = jnp.zeros((SLC_BLOCK, tq), F32)
            bias = jnp.transpose(jnp.concatenate([pad_t, bias_t] if g == 0 else [bias_t, pad_t], axis=0))
            qsel_ref[g] = jnp.concatenate([jnp.where(in_half, qf[h], bias) for h in range(hg)],
                                          axis=0).astype(qsel_ref.dtype)

            sw = _nt_dot(q_plain, kw_ref[pl.ds(w0, wk), :]) + win_bias
            pw = jnp.exp2((sw - jnp.max(sw, axis=-1, keepdims=True)).astype(MXU_DTYPE))
            o_win = normalise(_dot(pw, (vw1_ref if g else vw0_ref)[pl.ds(w0, wk), :]), g)
            part_ref[g] = jnp.concatenate(
                [gate(g, h, 0) * o_cmp[h * tq:(h + 1) * tq] + gate(g, h, 2) * o_win[h * tq:(h + 1) * tq]
                 for h in range(hg)], axis=0)

    tiles_per_part = n_q_tiles // SEQ_PARTS
    for c in range(SEQ_PARTS):
        nc_part = min(nc, -(-(nc * (c + 1) // SEQ_PARTS) // LANES) * LANES)
        pl.when(qi // tiles_per_part == c)(
            functools.partial(phase1, n_sel * (c + 1) // SEQ_PARTS, nc_part))

    outs = []
    for g in range(NSA_KV_GROUPS):
        o_slc = normalise(_causal_flash(qsel_ref[g], ka1_ref if g else ka0_ref, vs1_ref if g else vs0_ref,
                                        m_ref, None, acc_ref, s0, tq, tk), g)
        outs.append([part_ref[g, h * tq:(h + 1) * tq, :] + gate(g, h, 1) * o_slc[h * tq:(h + 1) * tq]
                     for h in range(hg)])

    for h in range(hg):
        o_ref[:, h * LANES:(h + 1) * LANES] = jnp.where(lane < HEAD_DIM, outs[0][h], outs[1][h]).astype(o_ref.dtype)


def _post_kernel(x_ref, od_ref, on_ref, wod_ref, won_ref, gpost_ref, gpre_ref, wg_ref, wu_ref, wd_ref,
                 gffn_ref, o_ref, act_ref, *, d_ff, ff_chunk, row_splits):
    tm = x_ref.shape[0] // row_splits
    for r in range(row_splits):
        rs = slice(r * tm, (r + 1) * tm)
        mix = _dot(od_ref[rs, :], wod_ref[...]) + _dot(on_ref[rs, :], won_ref[...])
        x1 = x_ref[rs, :] + _rms(mix, gpost_ref[...])
        h = _rms(x1, gpre_ref[...]).astype(MXU_DTYPE)
        for c in range(0, d_ff, ff_chunk):
            gate = _dot(h, wg_ref[:, c:c + ff_chunk])
            up = _dot(h, wu_ref[:, c:c + ff_chunk])
            act_ref[rs, c:c + ff_chunk] = (gate * jax.nn.sigmoid(gate) * up).astype(act_ref.dtype)
        f = _dot(act_ref[rs, :], wd_ref[...])
        o_ref[rs, :] = x1 + _rms(f, gffn_ref[...])


def _resident(shape):
    nd = len(shape)
    return pl.BlockSpec(shape, lambda *_: (0,) * nd, pipeline_mode=pl.Buffered(1))


def _params(sem):
    return pltpu.CompilerParams(dimension_semantics=sem, vmem_limit_bytes=VMEM_LIMIT)


def _rope_tables(S):
    inv = 1.0 / (ROPE_THETA ** (jnp.arange(0, HEAD_DIM, 2, dtype=F32) / HEAD_DIM))
    ang = jnp.arange(S, dtype=F32)[:, None] * inv[None, :]
    cos, sin = jnp.cos(ang), jnp.sin(ang)
    return jnp.tile(cos, (1, 4)), jnp.concatenate([-sin, sin, -sin, sin], axis=1)


def _selection_overlap_t(nc, n_cmp, n_sel):
    c0 = np.arange(n_cmp)[:, None] * CMP_STRIDE
    b0 = np.arange(n_sel)[None, :] * SLC_BLOCK
    ov = np.clip(np.minimum(c0 + CMP_BLOCK, b0 + SLC_BLOCK) - np.maximum(c0, b0), 0, None) / CMP_BLOCK
    full = np.zeros((LANES, nc), np.float32)
    full[:n_sel, :n_cmp] = ov.T
    return full


def _layer(x, layer, attn_pre_norm, w_in, lq1, lk1, lq2, lk2, diff_subln, k_pos, k_w1, k_w2,
           v_pos, v_w1, v_w2, w_out, attn_post_norm, ffn_pre_norm, w_gate, w_up, w_down, ffn_post_norm):
    B, S, D = x.shape
    N = B * S
    d_ff = w_gate.shape[1]
    nc = S // CMP_STRIDE
    n_cmp = (S - CMP_BLOCK) // CMP_STRIDE + 1
    n_sel = S // SLC_BLOCK
    top_n = min(SLC_TOPK, n_sel)
    assert n_sel <= SLC_BLOCK and S % 512 == 0
    lam_init = 0.8 - 0.6 * math.exp(-0.3 * layer)
    dt = MXU_DTYPE

    hg, G, d = NSA_HEADS_PER_GROUP, NSA_KV_GROUPS, HEAD_DIM
    head_starts = [(g * hg + h) * d for h in range(hg) for g in range(G)]
    w_cat = jnp.concatenate(
        [w_in[:, :1536]] + [w_in[:, 1536 + c:1536 + c + d] for c in head_starts]
        + [w_in[:, 2304:2816],
           w_in[:, 2048:2304],
           w_in[:, 2816:2840],
           jnp.zeros((D, LANES - hg * G * N_BRANCHES), w_in.dtype)], axis=1).astype(dt)
    w_out_d = w_out[:512].astype(dt)
    w_out_n = jnp.concatenate([w_out[512 + c:512 + c + d] for c in head_starts], axis=0).astype(dt)
    cos_t, sin_t = _rope_tables(S)

    tm = 512
    seq_tiles = S // tm
    n_cols = (N_MAIN_SLABS + N_AUX_SLABS) * LANES
    p, aux = pl.pallas_call(
        functools.partial(_inproj_kernel, tm=tm, seq_tiles=seq_tiles),
        grid=(N // tm,),
        in_specs=[pl.BlockSpec((tm, D), lambda i: (i, 0)),
                  _resident((1, D)),
                  _resident((D, n_cols)),
                  pl.BlockSpec((tm, LANES), lambda i: (i % seq_tiles, 0)),
                  pl.BlockSpec((tm, LANES), lambda i: (i % seq_tiles, 0))],
        out_specs=[pl.BlockSpec((tm, N_P_SLABS * LANES), lambda i: (i, 0)),
                   pl.BlockSpec((tm, N_AUX_SLABS * LANES), lambda i: (i, 0))],
        out_shape=[jax.ShapeDtypeStruct((N, N_P_SLABS * LANES), dt),
                   jax.ShapeDtypeStruct((N, N_AUX_SLABS * LANES), F32)],
        compiler_params=_params(("parallel",)),
        name="inproj",
    )(x.reshape(N, D), attn_pre_norm.reshape(1, D), w_cat, cos_t, sin_t)
    p = p.reshape(B, S, N_P_SLABS * LANES)
    aux = aux.reshape(B, S, N_AUX_SLABS * LANES)

    eye_g = jnp.eye(G, dtype=F32)

    def pos_rows(pos):
        tiled = jnp.broadcast_to(pos.reshape(2, CMP_STRIDE, 1, d), (2, CMP_STRIDE, G, d))
        return tiled[0].reshape(1, -1), tiled[1].reshape(1, -1)

    def w1_blocks(w1):
        w = w1.reshape(2, CMP_STRIDE, d, CMP_HIDDEN)
        blk = jnp.einsum('pldj,gh->plgdhj', w, eye_g).reshape(2, CMP_STRIDE * G * d, G * CMP_HIDDEN)
        return blk[0].astype(dt), blk[1].astype(dt)

    def w2_blocks(w2):
        return jnp.einsum('jd,gh->gjhd', w2, eye_g).reshape(G * CMP_HIDDEN, G * d).astype(dt)

    kpl, kph = pos_rows(k_pos)
    vpl, vph = pos_rows(v_pos)
    kwl, kwh = w1_blocks(k_w1)
    vwl, vwh = w1_blocks(v_w1)
    chunk_w = CMP_STRIDE * G * d
    w1_spec = _resident((chunk_w, G * CMP_HIDDEN))
    w2_spec = _resident((G * CMP_HIDDEN, LANES))
    kcmp, vcmp = pl.pallas_call(
        functools.partial(_compress_kernel, nc=nc),
        grid=(B,),
        in_specs=[pl.BlockSpec((None, S, LANES), lambda b: (b, 0, 0)),
                  pl.BlockSpec((None, S, LANES), lambda b: (b, 0, 1)),
                  _resident((1, chunk_w)), _resident((1, chunk_w)), _resident((1, chunk_w)), _resident((1, chunk_w)),
                  w1_spec, w1_spec, w2_spec, w1_spec, w1_spec, w2_spec],
        out_specs=[pl.BlockSpec((None, nc, LANES), lambda b: (b, 0, 0)),
                   pl.BlockSpec((None, nc, LANES), lambda b: (b, 0, 0))],
        out_shape=[jax.ShapeDtypeStruct((B, nc, LANES), dt), jax.ShapeDtypeStruct((B, nc, LANES), dt)],
        compiler_params=_params(("parallel",)),
        name="compress",
    )(aux, aux, kpl, kph, vpl, vph, kwl, kwh, w2_blocks(k_w2), vwl, vwh, w2_blocks(v_w2))

    tq_d, tk_d = 512, 1024
    lam_vec = [v.reshape(1, d) for v in (lq1, lk1, lq2, lk2)]
    o_diff = pl.pallas_call(
        functools.partial(_diff_kernel, tq=tq_d, tk=tk_d, lam_init=lam_init),
        grid=(B, DIFF_HEADS, S // tq_d),
        in_specs=[_resident((1, d))] * 4 + [_resident((1, LANES)),
                  pl.BlockSpec((None, tq_d, LANES), lambda b, h, i: (b, i, h)),
                  pl.BlockSpec((None, S, LANES), lambda b, h, i: (b, 0, 4 + h)),
                  pl.BlockSpec((None, S, LANES), lambda b, h, i: (b, 0, 8 + h))],
        out_specs=pl.BlockSpec((None, tq_d, LANES), lambda b, h, i: (b, i, h)),
        out_shape=jax.ShapeDtypeStruct((B, S, DIFF_HEADS * LANES), dt),
        scratch_shapes=[pltpu.VMEM((2 * tq_d, LANES), F32)] * 3,
        compiler_params=_params(("parallel", "parallel", "arbitrary")),
        name="diff_attn",
    )(*lam_vec, diff_subln.reshape(1, LANES), p, p, p)

    tq_n, tk_n = 256, 512
    ovt = jnp.asarray(_selection_overlap_t(nc, n_cmp, n_sel)).astype(dt)
    rows = NSA_HEADS_PER_GROUP * tq_n

    def seq_slab(c):
        return pl.BlockSpec((None, S, LANES), lambda b, i: (b, 0, c))

    o_nsa = pl.pallas_call(
        functools.partial(_nsa_kernel, tq=tq_n, tk=tk_n, nc=nc, n_sel=n_sel, top_n=top_n, n_q_tiles=S // tq_n),
        grid=(B, S // tq_n),
        in_specs=[pl.BlockSpec((None, tq_n, 4 * LANES), lambda b, i: (b, i, 3)),
                  pl.BlockSpec((None, tq_n, LANES), lambda b, i: (b, i, 2)),
                  pl.BlockSpec((None, nc, LANES), lambda b, i: (b, 0, 0)),
                  pl.BlockSpec((None, nc, LANES), lambda b, i: (b, 0, 0)),
                  _resident((LANES, nc)),
                  seq_slab(16), seq_slab(17), seq_slab(18), seq_slab(19), seq_slab(20), seq_slab(21),
                  seq_slab(22)],
        out_specs=pl.BlockSpec((None, tq_n, 4 * LANES), lambda b, i: (b, i, 0)),
        out_shape=jax.ShapeDtypeStruct((B, S, 4 * LANES), dt),
        scratch_shapes=[pltpu.VMEM((rows, LANES), F32)] * 2
        + [pltpu.VMEM((G, rows, LANES), dt), pltpu.VMEM((G, rows, LANES), F32)],
        compiler_params=_params(("parallel", "arbitrary")),
        name="nsa_attn",
    )(p, aux, kcmp, vcmp, ovt, p, p, p, p, p, p, p)

    tm2 = 512
    row = lambda i: (i, 0)
    out = pl.pallas_call(
        functools.partial(_post_kernel, d_ff=d_ff, ff_chunk=256, row_splits=1),
        grid=(N // tm2,),
        in_specs=[pl.BlockSpec((tm2, D), row),
                  pl.BlockSpec((tm2, 512), row), pl.BlockSpec((tm2, 512), row),
                  _resident((512, D)), _resident((512, D)), _resident((1, D)), _resident((1, D)),
                  _resident((D, d_ff)), _resident((D, d_ff)), _resident((d_ff, D)), _resident((1, D))],
        out_specs=pl.BlockSpec((tm2, D), row),
        out_shape=jax.ShapeDtypeStruct((N, D), F32),
        scratch_shapes=[pltpu.VMEM((tm2, d_ff), dt)],
        compiler_params=_params(("parallel",)),
        name="post",
    )(x.reshape(N, D), o_diff.reshape(N, 512), o_nsa.reshape(N, 512), w_out_d, w_out_n,
      attn_post_norm.reshape(1, D), ffn_pre_norm.reshape(1, D),
      w_gate.astype(dt), w_up.astype(dt), w_down.astype(dt), ffn_post_norm.reshape(1, D))
    return out.reshape(B, S, D)


def kernel(x, attn_pre_norm, w_in, lambda_q1, lambda_k1, lambda_q2, lambda_k2, diff_subln, k_cmp_pos, k_cmp_w1, k_cmp_w2, v_cmp_pos, v_cmp_w1, v_cmp_w2, w_out, attn_post_norm, ffn_pre_norm, w_gate, w_up, w_down, ffn_post_norm):
    for l in range(w_in.shape[0]):
        x = _layer(x, l, attn_pre_norm[l], w_in[l], lambda_q1[l], lambda_k1[l], lambda_q2[l], lambda_k2[l],
                   diff_subln[l], k_cmp_pos[l], k_cmp_w1[l], k_cmp_w2[l], v_cmp_pos[l], v_cmp_w1[l], v_cmp_w2[l],
                   w_out[l], attn_post_norm[l], ffn_pre_norm[l], w_gate[l], w_up[l], w_down[l], ffn_post_norm[l])
    return x
```

```python
import functools
import math

import numpy as np
import jax
import jax.numpy as jnp
from jax import lax
from jax.experimental import pallas as pl
from jax.experimental.pallas import tpu as pltpu

F32 = jnp.float32
MXU_DTYPE = jnp.bfloat16

LANES = 128
SUBLANES = 8
HEAD_DIM = 64
ROPE_THETA = 10000.0
NORM_EPS = 1e-6
NEG_INF = -1e30
LOG2_E = 1.4426950408889634
FORCED_IMPORTANCE = 3e38

DIFF_HEADS = 4
NSA_HEADS_PER_GROUP = 4
NSA_KV_GROUPS = 2
CMP_BLOCK = 32
CMP_STRIDE = 16
CMP_HIDDEN = 4 * HEAD_DIM
SLC_BLOCK = 64
SLC_TOPK = 16
WINDOW = 512
N_BRANCHES = 3
SEQ_PARTS = 4

N_MAIN_SLABS = 20
N_AUX_SLABS = 3
N_P_SLABS = 23
VMEM_LIMIT = 56 * 1024 * 1024


def _nt_dot(a, b):
    return lax.dot_general(a, b, (((1,), (1,)), ((), ())), preferred_element_type=F32)


def _dot(a, b):
    return jnp.dot(a, b, preferred_element_type=F32)


def _rms(x, g):
    return x * lax.rsqrt(jnp.mean(x * x, axis=-1, keepdims=True) + NORM_EPS) * g


def _inproj_kernel(x_ref, g_ref, w_ref, cos_ref, sin_ref, p_ref, aux_ref, *, tm, seq_tiles):
    h = _rms(x_ref[...], g_ref[...]).astype(MXU_DTYPE)
    cos = cos_ref[...]
    sin = sin_ref[...]
    lane = lax.broadcasted_iota(jnp.int32, (tm, LANES), 1)
    low_half = (lane & (HEAD_DIM - 1)) < HEAD_DIM // 2

    def rope(y):
        fwd = pltpu.roll(y, HEAD_DIM // 2, 1)
        bwd = pltpu.roll(y, LANES - HEAD_DIM // 2, 1)
        return y * cos + jnp.where(low_half, bwd, fwd) * sin

    pos = (pl.program_id(0) % seq_tiles) * tm + lax.broadcasted_iota(jnp.int32, (tm, LANES), 0)
    blk = pos // SLC_BLOCK
    scale = HEAD_DIM ** -0.5 * LOG2_E

    n_slabs = N_MAIN_SLABS + N_AUX_SLABS
    for c0 in range(0, n_slabs, 2):
        c1 = min(c0 + 2, n_slabs)
        y2 = _dot(h, w_ref[:, c0 * LANES:c1 * LANES])
        for s in range(c0, c1):
            y = y2[:, (s - c0) * LANES:(s - c0 + 1) * LANES]
            if s < 4 or 12 <= s < 16:
                p_ref[:, s * LANES:(s + 1) * LANES] = (rope(y) * scale).astype(p_ref.dtype)
            elif s < 8:
                p_ref[:, s * LANES:(s + 1) * LANES] = rope(y).astype(p_ref.dtype)
            elif s < 12:
                p_ref[:, s * LANES:(s + 1) * LANES] = y.astype(p_ref.dtype)
            elif s == 16:
                r = rope(y)
                ind_hi = jnp.where(lane - HEAD_DIM == blk, 1.0, 0.0)
                ind_lo = jnp.where(lane == blk, 1.0, 0.0)
                p_ref[:, 16 * LANES:17 * LANES] = jnp.where(lane < HEAD_DIM, r, ind_hi).astype(p_ref.dtype)
                p_ref[:, 17 * LANES:18 * LANES] = jnp.where(lane >= HEAD_DIM, r, ind_lo).astype(p_ref.dtype)
            elif s == 17 or s == 19:
                o = 18 if s == 17 else 21
                p_ref[:, o * LANES:(o + 1) * LANES] = jnp.where(lane < HEAD_DIM, y, 1.0).astype(p_ref.dtype)
                p_ref[:, (o + 1) * LANES:(o + 2) * LANES] = jnp.where(lane >= HEAD_DIM, y, 1.0).astype(p_ref.dtype)
            elif s == 18:
                p_ref[:, 20 * LANES:21 * LANES] = rope(y).astype(p_ref.dtype)
            elif s == 20:
                aux_ref[:, 0:LANES] = rope(y)
            elif s == 21:
                aux_ref[:, LANES:2 * LANES] = y
            else:
                aux_ref[:, 2 * LANES:3 * LANES] = jax.nn.sigmoid(y)


def _compress_kernel(tk_ref, tv_ref, kpl_ref, kph_ref, vpl_ref, vph_ref,
                     kwl_ref, kwh_ref, kw2_ref, vwl_ref, vwh_ref, vw2_ref, ko_ref, vo_ref, *, nc):
    lane = lax.broadcasted_iota(jnp.int32, (nc, CMP_STRIDE * LANES), 1)
    group_lanes = [(lane & HEAD_DIM) == 0, (lane & HEAD_DIM) != 0]

    def compress(t_ref, plo_ref, phi_ref, wlo_ref, whi_ref, w2_ref):
        x = jnp.concatenate([t_ref[pl.ds(l, nc, stride=CMP_STRIDE), :] for l in range(CMP_STRIDE)], axis=1)
        x_lo = x + plo_ref[...]
        x_hi = x + phi_ref[...]
        out = None
        for g in range(NSA_KV_GROUPS):
            a = _dot(jnp.where(group_lanes[g], x_lo, 0.0).astype(MXU_DTYPE), wlo_ref[...])
            b = _dot(jnp.where(group_lanes[g], x_hi, 0.0).astype(MXU_DTYPE), whi_ref[...])
            hid = a + pltpu.roll(b, nc - 1, 0)
            act = hid * jax.nn.sigmoid(hid)
            o = _dot(act.astype(MXU_DTYPE), w2_ref[g])
            out = o if out is None else out + o
        return out

    ko_ref[...] = compress(tk_ref, kpl_ref, kph_ref, kwl_ref, kwh_ref, kw2_ref).astype(ko_ref.dtype)
    vo_ref[...] = compress(tv_ref, vpl_ref, vph_ref, vwl_ref, vwh_ref, vw2_ref).astype(vo_ref.dtype)


def _flash_init(m_ref, l_ref, acc_ref):
    m_ref[...] = jnp.full(m_ref.shape, NEG_INF, F32)
    if l_ref is not None:
        l_ref[...] = jnp.zeros(l_ref.shape, F32)
    acc_ref[...] = jnp.zeros(acc_ref.shape, F32)


def _lane_tiles(x):
    return [x[:, c:c + LANES] for c in range(0, x.shape[1], LANES)]


def _stack_rows(x, n):
    return jnp.concatenate([x] * n, axis=0)


def _softmax_pv(s, v, m_ref, l_ref, acc_ref, bias=None):
    if bias is not None:
        s = s + bias
    n_tiles = s.shape[1] // LANES
    m_prev = m_ref[...]
    m_new = jnp.maximum(m_prev, jnp.max(s, axis=-1, keepdims=True))
    alpha = jnp.exp2(m_prev - m_new)
    x = s - jnp.concatenate([m_new] * n_tiles, axis=1)
    if l_ref is None:
        p = jnp.exp2(x.astype(MXU_DTYPE))
    else:
        p = jnp.exp2(x)
        l_ref[...] = alpha * l_ref[...] + functools.reduce(lambda a, b: a + b, _lane_tiles(p))
        p = p.astype(MXU_DTYPE)
    acc_ref[...] = alpha * acc_ref[...] + _dot(p, v)
    m_ref[...] = m_new


def _flash_finish(l_ref, acc_ref):
    if l_ref is None:
        return acc_ref[...]
    return acc_ref[...] / jnp.sum(l_ref[...], axis=-1, keepdims=True)


def _causal_flash(q, k_ref, v_ref, m_ref, l_ref, acc_ref, s0, tq, tk, merge_odd_tile=True):
    assert tk in (tq, 2 * tq)
    rows = q.shape[0]
    _flash_init(m_ref, l_ref, acc_ref)

    def step(k0, width, bias=None):
        k0 = pl.multiple_of(k0, width)
        _softmax_pv(_nt_dot(q, k_ref[pl.ds(k0, width), :]), v_ref[pl.ds(k0, width), :],
                    m_ref, l_ref, acc_ref, bias)

    def pair(jj, carry):
        step(2 * jj * tk, tk)
        step((2 * jj + 1) * tk, tk)
        return carry

    n_full = s0 // tk
    lax.fori_loop(0, n_full // 2, pair, 0)

    def tail_bias(width):
        qpos = (width - tq) + lax.broadcasted_iota(jnp.int32, (tq, width), 0)
        causal = lax.broadcasted_iota(jnp.int32, (tq, width), 1) <= qpos
        return _stack_rows(jnp.where(causal, 0.0, NEG_INF), rows // tq)

    def tail(odd_tile, width):
        def body():
            if odd_tile:
                step((n_full - 1) * tk, tk)
            step(s0 + tq - width, width, tail_bias(width))
        return body

    aligned = (s0 - n_full * tk) == 0
    if not merge_odd_tile:
        pl.when(n_full % 2 == 1)(lambda: step((n_full - 1) * tk, tk))
        if tk == tq:
            step(s0, tq, tail_bias(tq))
        else:
            pl.when(aligned)(tail(False, tq))
            pl.when(jnp.logical_not(aligned))(tail(False, tk))
        return _flash_finish(l_ref, acc_ref)

    for odd_tile in (False, True):
        parity = (n_full % 2 == 1) if odd_tile else (n_full % 2 == 0)
        pl.when(parity & aligned)(tail(odd_tile, tq))
        pl.when(parity & jnp.logical_not(aligned))(tail(odd_tile, tk))
    return _flash_finish(l_ref, acc_ref)


def _diff_kernel(lq1_ref, lk1_ref, lq2_ref, lk2_ref, subln_ref, q_ref, k_ref, v_ref, o_ref,
                 m_ref, l_ref, acc_ref, *, tq, tk, lam_init):
    qi = pl.program_id(2)
    s0 = qi * tq
    lane = lax.broadcasted_iota(jnp.int32, (tq, LANES), 1)
    qf = q_ref[...].astype(F32)
    zero = jnp.zeros_like(qf)
    q2 = jnp.concatenate([jnp.where(lane < HEAD_DIM, qf, zero),
                          jnp.where(lane >= HEAD_DIM, qf, zero)], axis=0).astype(MXU_DTYPE)

    o12 = _causal_flash(q2, k_ref, v_ref, m_ref, l_ref, acc_ref, s0, tq, tk, merge_odd_tile=False)
    lam = (jnp.exp(jnp.sum(lq1_ref[...] * lk1_ref[...], axis=-1, keepdims=True))
           - jnp.exp(jnp.sum(lq2_ref[...] * lk2_ref[...], axis=-1, keepdims=True)) + lam_init)
    o = o12[:tq] - lam * o12[tq:]
    o_ref[...] = (_rms(o, subln_ref[...]) * (1.0 - lam_init)).astype(o_ref.dtype)


def _nsa_kernel(q_ref, gate_ref, kc_ref, vc_ref, ovt_ref, ka0_ref, ka1_ref, vs0_ref, vs1_ref,
                kw_ref, vw0_ref, vw1_ref, o_ref, m_ref, acc_ref, qsel_ref, part_ref,
                *, tq, tk, nc, n_sel, top_n, n_q_tiles):
    hg = NSA_HEADS_PER_GROUP
    rows = hg * tq
    qi = pl.program_id(1)
    s0 = qi * tq
    lane = lax.broadcasted_iota(jnp.int32, (tq, LANES), 1)
    lane_rows = lax.broadcasted_iota(jnp.int32, (rows, LANES), 1)
    qf = [q_ref[:, h * LANES:(h + 1) * LANES].astype(F32) for h in range(hg)]
    gates = gate_ref[...]
    zero = jnp.zeros((tq, LANES), F32)

    wk = WINDOW + tq
    w0 = pl.multiple_of(jnp.maximum(s0 - WINDOW, 0), tq)
    back = (s0 - w0) + lax.broadcasted_iota(jnp.int32, (tq, wk), 0) \
        - lax.broadcasted_iota(jnp.int32, (tq, wk), 1)
    win_bias = _stack_rows(jnp.where((back >= 0) & (back < WINDOW), 0.0, NEG_INF), hg)

    def normalise(raw, g):
        in_half_rows = (lane_rows >= HEAD_DIM) if g else (lane_rows < HEAD_DIM)
        return raw / jnp.where(in_half_rows, pltpu.roll(raw, HEAD_DIM, 1), 1.0)

    def gate(g, h, branch):
        c = (g * hg + h) * N_BRANCHES + branch
        return gates[:, c:c + 1]

    def phase1(n_blk, nc_eff):
        n_idx = lax.broadcasted_iota(jnp.int32, (tq, nc_eff), 1)
        qpos_c = s0 + lax.broadcasted_iota(jnp.int32, (tq, nc_eff), 0)
        cmp_ok = n_idx * CMP_STRIDE + (CMP_BLOCK - 1) <= qpos_c
        cmp_bias = _stack_rows(jnp.where(cmp_ok, 0.0, NEG_INF), hg)
        cmp_keep = _stack_rows(jnp.where(cmp_ok, 1.0, 0.0), hg)
        jb = lax.broadcasted_iota(jnp.int32, (n_blk, tq), 0)
        qp = s0 + lax.broadcasted_iota(jnp.int32, (n_blk, tq), 1)
        cur = qp // SLC_BLOCK
        valid = jb * SLC_BLOCK <= qp
        forced = (jb == 0) | (jb == cur) | (jb == cur - 1)

        for g in range(NSA_KV_GROUPS):
            in_half = (lane >= HEAD_DIM) if g else (lane < HEAD_DIM)
            q_plain = jnp.concatenate([jnp.where(in_half, qf[h], zero) for h in range(hg)],
                                      axis=0).astype(MXU_DTYPE)

            sc = _nt_dot(q_plain, kc_ref[:nc_eff, :]) + cmp_bias
            pc = jnp.exp2(sc - jnp.max(sc, axis=-1, keepdims=True)) * cmp_keep
            lc = jnp.sum(pc, axis=-1, keepdims=True)
            pc = (pc / jnp.where(lc > 0.0, lc, 1.0)).astype(MXU_DTYPE)
            o_cmp = _dot(pc, vc_ref[:nc_eff, :])

            if n_blk <= top_n:
                selected = valid
            else:
                imp_t = None
                for h in range(hg):
                    part = _nt_dot(ovt_ref[:, :nc_eff], pc[h * tq:(h + 1) * tq])
                    imp_t = part if imp_t is None else imp_t + part
                key = jnp.where(forced, FORCED_IMPORTANCE, jnp.where(valid, imp_t[:n_blk], -1.0))
                n_grp = n_blk // SUBLANES
                key_g = [key[a * SUBLANES:(a + 1) * SUBLANES] for a in range(n_grp)]
                jb_g = lax.broadcasted_iota(jnp.int32, (SUBLANES, tq), 0)
                rank_g = [jnp.zeros((SUBLANES, tq), jnp.int32) for _ in range(n_grp)]
                for i in range(n_blk):
                    ri = key[i:i + 1, :]
                    for a in range(n_grp):
                        if a < i // SUBLANES:
                            ahead = jnp.where(ri > key_g[a], 1, 0)
                        elif a > i // SUBLANES:
                            ahead = jnp.where(ri >= key_g[a], 1, 0)
                        else:
                            ahead = (jnp.where(ri > key_g[a], 1, 0)
                                     + jnp.where((ri == key_g[a]) & (jb_g > i % SUBLANES), 1, 0))
                        rank_g[a] = rank_g[a] + ahead
                selected = (jnp.concatenate(rank_g, axis=0) < top_n) & valid
            bias_t = jnp.where(selected, 0.0, NEG_INF)
            if n_blk < SLC_BLOCK:
                bias_t = jnp.concatenate([bias_t, jnp.full((SLC_BLOCK - n_blk, tq), NEG_INF, F32)], axis=0)
            pad_t = jnp.zeros((SLC_BLOCK, tq), F32)
            bias = jnp.transpose(jnp.concatenate([pad_t, bias_t] if g == 0 else [bias_t, pad_t], axis=0))
            qsel_ref[g] = jnp.concatenate([jnp.where(in_half, qf[h], bias) for h in range(hg)],
                                          axis=0).astype(qsel_ref.dtype)

            sw = _nt_dot(q_plain, kw_ref[pl.ds(w0, wk), :]) + win_bias
            pw = jnp.exp2((sw - jnp.max(sw, axis=-1, keepdims=True)).astype(MXU_DTYPE))
            o_win = normalise(_dot(pw, (vw1_ref if g else vw0_ref)[pl.ds(w0, wk), :]), g)
            part_ref[g] = jnp.concatenate(
                [gate(g, h, 0) * o_cmp[h * tq:(h + 1) * tq] + gate(g, h, 2) * o_win[h * tq:(h + 1) * tq]
                 for h in range(hg)], axis=0)

    tiles_per_part = n_q_tiles // SEQ_PARTS
    for c in range(SEQ_PARTS):
        nc_part = min(nc, -(-(nc * (c + 1) // SEQ_PARTS) // LANES) * LANES)
        pl.when(qi // tiles_per_part == c)(
            functools.partial(phase1, n_sel * (c + 1) // SEQ_PARTS, nc_part))

    outs = []
    for g in range(NSA_KV_GROUPS):
        o_slc = normalise(_causal_flash(qsel_ref[g], ka1_ref if g else ka0_ref, vs1_ref if g else vs0_ref,
                                        m_ref, None, acc_ref, s0, tq, tk), g)
        outs.append([part_ref[g, h * tq:(h + 1) * tq, :] + gate(g, h, 1) * o_slc[h * tq:(h + 1) * tq]
                     for h in range(hg)])

    for h in range(hg):
        o_ref[:, h * LANES:(h + 1) * LANES] = jnp.where(lane < HEAD_DIM, outs[0][h], outs[1][h]).astype(o_ref.dtype)


def _post_kernel(x_ref, od_ref, on_ref, wod_ref, won_ref, gpost_ref, gpre_ref, wg_ref, wu_ref, wd_ref,
                 gffn_ref, o_ref, act_ref, *, d_ff, ff_chunk, row_splits):
    tm = x_ref.shape[0] // row_splits
    for r in range(row_splits):
        rs = slice(r * tm, (r + 1) * tm)
        mix = _dot(od_ref[rs, :], wod_ref[...]) + _dot(on_ref[rs, :], won_ref[...])
        x1 = x_ref[rs, :] + _rms(mix, gpost_ref[...])
        h = _rms(x1, gpre_ref[...]).astype(MXU_DTYPE)
        for c in range(0, d_ff, ff_chunk):
            gate = _dot(h, wg_ref[:, c:c + ff_chunk])
            up = _dot(h, wu_ref[:, c:c + ff_chunk])
            act_ref[rs, c:c + ff_chunk] = (gate * jax.nn.sigmoid(gate) * up).astype(act_ref.dtype)
        f = _dot(act_ref[rs, :], wd_ref[...])
        o_ref[rs, :] = x1 + _rms(f, gffn_ref[...])


def _resident(shape):
    nd = len(shape)
    return pl.BlockSpec(shape, lambda *_: (0,) * nd, pipeline_mode=pl.Buffered(1))


def _params(sem):
    return pltpu.CompilerParams(dimension_semantics=sem, vmem_limit_bytes=VMEM_LIMIT)


def _rope_tables(S):
    inv = 1.0 / (ROPE_THETA ** (jnp.arange(0, HEAD_DIM, 2, dtype=F32) / HEAD_DIM))
    ang = jnp.arange(S, dtype=F32)[:, None] * inv[None, :]
    cos, sin = jnp.cos(ang), jnp.sin(ang)
    return jnp.tile(cos, (1, 4)), jnp.concatenate([-sin, sin, -sin, sin], axis=1)


def _selection_overlap_t(nc, n_cmp, n_sel):
    c0 = np.arange(n_cmp)[:, None] * CMP_STRIDE
    b0 = np.arange(n_sel)[None, :] * SLC_BLOCK
    ov = np.clip(np.minimum(c0 + CMP_BLOCK, b0 + SLC_BLOCK) - np.maximum(c0, b0), 0, None) / CMP_BLOCK
    full = np.zeros((LANES, nc), np.float32)
    full[:n_sel, :n_cmp] = ov.T
    return full


def _layer(x, layer, attn_pre_norm, w_in, lq1, lk1, lq2, lk2, diff_subln, k_pos, k_w1, k_w2,
           v_pos, v_w1, v_w2, w_out, attn_post_norm, ffn_pre_norm, w_gate, w_up, w_down, ffn_post_norm):
    B, S, D = x.shape
    N = B * S
    d_ff = w_gate.shape[1]
    nc = S // CMP_STRIDE
    n_cmp = (S - CMP_BLOCK) // CMP_STRIDE + 1
    n_sel = S // SLC_BLOCK
    top_n = min(SLC_TOPK, n_sel)
    assert n_sel <= SLC_BLOCK and S % 512 == 0
    lam_init = 0.8 - 0.6 * math.exp(-0.3 * layer)
    dt = MXU_DTYPE

    hg, G, d = NSA_HEADS_PER_GROUP, NSA_KV_GROUPS, HEAD_DIM
    head_starts = [(g * hg + h) * d for h in range(hg) for g in range(G)]
    w_cat = jnp.concatenate(
        [w_in[:, :1536]] + [w_in[:, 1536 + c:1536 + c + d] for c in head_starts]
        + [w_in[:, 2304:2816],
           w_in[:, 2048:2304],
           w_in[:, 2816:2840],
           jnp.zeros((D, LANES - hg * G * N_BRANCHES), w_in.dtype)], axis=1).astype(dt)
    w_out_d = w_out[:512].astype(dt)
    w_out_n = jnp.concatenate([w_out[512 + c:512 + c + d] for c in head_starts], axis=0).astype(dt)
    cos_t, sin_t = _rope_tables(S)

    tm = 1024
    seq_tiles = S // tm
    n_cols = (N_MAIN_SLABS + N_AUX_SLABS) * LANES
    p, aux = pl.pallas_call(
        functools.partial(_inproj_kernel, tm=tm, seq_tiles=seq_tiles),
        grid=(N // tm,),
        in_specs=[pl.BlockSpec((tm, D), lambda i: (i, 0)),
                  _resident((1, D)),
                  _resident((D, n_cols)),
                  pl.BlockSpec((tm, LANES), lambda i: (i % seq_tiles, 0)),
                  pl.BlockSpec((tm, LANES), lambda i: (i % seq_tiles, 0))],
        out_specs=[pl.BlockSpec((tm, N_P_SLABS * LANES), lambda i: (i, 0)),
                   pl.BlockSpec((tm, N_AUX_SLABS * LANES), lambda i: (i, 0))],
        out_shape=[jax.ShapeDtypeStruct((N, N_P_SLABS * LANES), dt),
                   jax.ShapeDtypeStruct((N, N_AUX_SLABS * LANES), F32)],
        compiler_params=_params(("parallel",)),
        name="inproj",
    )(x.reshape(N, D), attn_pre_norm.reshape(1, D), w_cat, cos_t, sin_t)
    p = p.reshape(B, S, N_P_SLABS * LANES)
    aux = aux.reshape(B, S, N_AUX_SLABS * LANES)

    def pos_rows(pos):
        tiled = jnp.broadcast_to(pos.reshape(2, CMP_STRIDE, 1, d), (2, CMP_STRIDE, G, d))
        return tiled[0].reshape(1, -1), tiled[1].reshape(1, -1)

    def w1_rows(w1):
        w = jnp.broadcast_to(w1.astype(dt).reshape(2, CMP_STRIDE, 1, d, CMP_HIDDEN),
                             (2, CMP_STRIDE, G, d, CMP_HIDDEN)).reshape(2, CMP_STRIDE * G * d, CMP_HIDDEN)
        return w[0], w[1]

    def w2_halves(w2):
        return jnp.stack([jnp.pad(w2, ((0, 0), (g * d, (G - 1 - g) * d))) for g in range(G)]).astype(dt)

    kpl, kph = pos_rows(k_pos)
    vpl, vph = pos_rows(v_pos)
    kwl, kwh = w1_rows(k_w1)
    vwl, vwh = w1_rows(v_w1)
    chunk_w = CMP_STRIDE * G * d
    w1_spec = _resident((chunk_w, CMP_HIDDEN))
    w2_spec = _resident((G, CMP_HIDDEN, LANES))
    kcmp, vcmp = pl.pallas_call(
        functools.partial(_compress_kernel, nc=nc),
        grid=(B,),
        in_specs=[pl.BlockSpec((None, S, LANES), lambda b: (b, 0, 0)),
                  pl.BlockSpec((None, S, LANES), lambda b: (b, 0, 1)),
                  _resident((1, chunk_w)), _resident((1, chunk_w)), _resident((1, chunk_w)), _resident((1, chunk_w)),
                  w1_spec, w1_spec, w2_spec, w1_spec, w1_spec, w2_spec],
        out_specs=[pl.BlockSpec((None, nc, LANES), lambda b: (b, 0, 0)),
                   pl.BlockSpec((None, nc, LANES), lambda b: (b, 0, 0))],
        out_shape=[jax.ShapeDtypeStruct((B, nc, LANES), dt), jax.ShapeDtypeStruct((B, nc, LANES), dt)],
        compiler_params=_params(("parallel",)),
        name="compress",
    )(aux, aux, kpl, kph, vpl, vph, kwl, kwh, w2_halves(k_w2), vwl, vwh, w2_halves(v_w2))

    tq_d, tk_d = 512, 1024
    lam_vec = [v.reshape(1, d) for v in (lq1, lk1, lq2, lk2)]
    o_diff = pl.pallas_call(
        functools.partial(_diff_kernel, tq=tq_d, tk=tk_d, lam_init=lam_init),
        grid=(B, DIFF_HEADS, S // tq_d),
        in_specs=[_resident((1, d))] * 4 + [_resident((1, LANES)),
                  pl.BlockSpec((None, tq_d, LANES), lambda b, h, i: (b, i, h)),
                  pl.BlockSpec((None, S, LANES), lambda b, h, i: (b, 0, 4 + h)),
                  pl.BlockSpec((None, S, LANES), lambda b, h, i: (b, 0, 8 + h))],
        out_specs=pl.BlockSpec((None, tq_d, LANES), lambda b, h, i: (b, i, h)),
        out_shape=jax.ShapeDtypeStruct((B, S, DIFF_HEADS * LANES), dt),
        scratch_shapes=[pltpu.VMEM((2 * tq_d, LANES), F32)] * 3,
        compiler_params=_params(("parallel", "parallel", "arbitrary")),
        name="diff_attn",
    )(*lam_vec, diff_subln.reshape(1, LANES), p, p, p)

    tq_n, tk_n = 256, 512
    ovt = jnp.asarray(_selection_overlap_t(nc, n_cmp, n_sel)).astype(dt)
    rows = NSA_HEADS_PER_GROUP * tq_n

    def seq_slab(c):
        return pl.BlockSpec((None, S, LANES), lambda b, i: (b, 0, c))

    o_nsa = pl.pallas_call(
        functools.partial(_nsa_kernel, tq=tq_n, tk=tk_n, nc=nc, n_sel=n_sel, top_n=top_n, n_q_tiles=S // tq_n),
        grid=(B, S // tq_n),
        in_specs=[pl.BlockSpec((None, tq_n, 4 * LANES), lambda b, i: (b, i, 3)),
                  pl.BlockSpec((None, tq_n, LANES), lambda b, i: (b, i, 2)),
                  pl.BlockSpec((None, nc, LANES), lambda b, i: (b, 0, 0)),
                  pl.BlockSpec((None, nc, LANES), lambda b, i: (b, 0, 0)),
                  _resident((LANES, nc)),
                  seq_slab(16), seq_slab(17), seq_slab(18), seq_slab(19), seq_slab(20), seq_slab(21),
                  seq_slab(22)],
        out_specs=pl.BlockSpec((None, tq_n, 4 * LANES), lambda b, i: (b, i, 0)),
        out_shape=jax.ShapeDtypeStruct((B, S, 4 * LANES), dt),
        scratch_shapes=[pltpu.VMEM((rows, LANES), F32)] * 2
        + [pltpu.VMEM((G, rows, LANES), dt), pltpu.VMEM((G, rows, LANES), F32)],
        compiler_params=_params(("parallel", "arbitrary")),
        name="nsa_attn",
    )(p, aux, kcmp, vcmp, ovt, p, p, p, p, p, p, p)

    tm2 = 512
    row = lambda i: (i, 0)
    out = pl.pallas_call(
        functools.partial(_post_kernel, d_ff=d_ff, ff_chunk=256, row_splits=1),
        grid=(N // tm2,),
        in_specs=[pl.BlockSpec((tm2, D), row),
                  pl.BlockSpec((tm2, 512), row), pl.BlockSpec((tm2, 512), row),
                  _resident((512, D)), _resident((512, D)), _resident((1, D)), _resident((1, D)),
                  _resident((D, d_ff)), _resident((D, d_ff)), _resident((d_ff, D)), _resident((1, D))],
        out_specs=pl.BlockSpec((tm2, D), row),
        out_shape=jax.ShapeDtypeStruct((N, D), F32),
        scratch_shapes=[pltpu.VMEM((tm2, d_ff), dt)],
        compiler_params=_params(("parallel",)),
        name="post",
    )(x.reshape(N, D), o_diff.reshape(N, 512), o_nsa.reshape(N, 512), w_out_d, w_out_n,
      attn_post_norm.reshape(1, D), ffn_pre_norm.reshape(1, D),
      w_gate.astype(dt), w_up.astype(dt), w_down.astype(dt), ffn_post_norm.reshape(1, D))
    return out.reshape(B, S, D)


def kernel(x, attn_pre_norm, w_in, lambda_q1, lambda_k1, lambda_q2, lambda_k2, diff_subln, k_cmp_pos, k_cmp_w1, k_cmp_w2, v_cmp_pos, v_cmp_w1, v_cmp_w2, w_out, attn_post_norm, ffn_pre_norm, w_gate, w_up, w_down, ffn_post_norm):
    for l in range(w_in.shape[0]):
        x = _layer(x, l, attn_pre_norm[l], w_in[l], lambda_q1[l], lambda_k1[l], lambda_q2[l], lambda_k2[l],
                   diff_subln[l], k_cmp_pos[l], k_cmp_w1[l], k_cmp_w2[l], v_cmp_pos[l], v_cmp_w1[l], v_cmp_w2[l],
                   w_out[l], attn_post_norm[l], ffn_pre_norm[l], w_gate[l], w_up[l], w_down[l], ffn_post_norm[l])
    return x
```

```python
import functools
import math

import numpy as np
import jax
import jax.numpy as jnp
from jax import lax
from jax.experimental import pallas as pl
from jax.experimental.pallas import tpu as pltpu

F32 = jnp.float32
MXU_DTYPE = jnp.bfloat16

LANES = 128
SUBLANES = 8
HEAD_DIM = 64
ROPE_THETA = 10000.0
NORM_EPS = 1e-6
NEG_INF = -1e30
LOG2_E = 1.4426950408889634
FORCED_IMPORTANCE = 3e38

DIFF_HEADS = 4
NSA_HEADS_PER_GROUP = 4
NSA_KV_GROUPS = 2
CMP_BLOCK = 32
CMP_STRIDE = 16
CMP_HIDDEN = 4 * HEAD_DIM
SLC_BLOCK = 64
SLC_TOPK = 16
WINDOW = 512
N_BRANCHES = 3
SEQ_PARTS = 4

N_MAIN_SLABS = 20
N_AUX_SLABS = 3
N_P_SLABS = 23
VMEM_LIMIT = 56 * 1024 * 1024


def _nt_dot(a, b):
    return lax.dot_general(a, b, (((1,), (1,)), ((), ())), preferred_element_type=F32)


def _dot(a, b):
    return jnp.dot(a, b, preferred_element_type=F32)


def _rms(x, g):
    return x * lax.rsqrt(jnp.mean(x * x, axis=-1, keepdims=True) + NORM_EPS) * g


def _inproj_kernel(x_ref, g_ref, w_ref, cos_ref, sin_ref, p_ref, aux_ref, *, tm, seq_tiles):
    h = _rms(x_ref[...], g_ref[...]).astype(MXU_DTYPE)
    cos = cos_ref[...]
    sin = sin_ref[...]
    lane = lax.broadcasted_iota(jnp.int32, (tm, LANES), 1)
    low_half = (lane & (HEAD_DIM - 1)) < HEAD_DIM // 2

    def rope(y):
        fwd = pltpu.roll(y, HEAD_DIM // 2, 1)
        bwd = pltpu.roll(y, LANES - HEAD_DIM // 2, 1)
        return y * cos + jnp.where(low_half, bwd, fwd) * sin

    pos = (pl.program_id(0) % seq_tiles) * tm + lax.broadcasted_iota(jnp.int32, (tm, LANES), 0)
    blk = pos // SLC_BLOCK
    scale = HEAD_DIM ** -0.5 * LOG2_E

    n_slabs = N_MAIN_SLABS + N_AUX_SLABS
    for c0 in range(0, n_slabs, 2):
        c1 = min(c0 + 2, n_slabs)
        y2 = _dot(h, w_ref[:, c0 * LANES:c1 * LANES])
        for s in range(c0, c1):
            y = y2[:, (s - c0) * LANES:(s - c0 + 1) * LANES]
            if s < 4 or 12 <= s < 16:
                p_ref[:, s * LANES:(s + 1) * LANES] = (rope(y) * scale).astype(p_ref.dtype)
            elif s < 8:
                p_ref[:, s * LANES:(s + 1) * LANES] = rope(y).astype(p_ref.dtype)
            elif s < 12:
                p_ref[:, s * LANES:(s + 1) * LANES] = y.astype(p_ref.dtype)
            elif s == 16:
                r = rope(y)
                ind_hi = jnp.where(lane - HEAD_DIM == blk, 1.0, 0.0)
                ind_lo = jnp.where(lane == blk, 1.0, 0.0)
                p_ref[:, 16 * LANES:17 * LANES] = jnp.where(lane < HEAD_DIM, r, ind_hi).astype(p_ref.dtype)
                p_ref[:, 17 * LANES:18 * LANES] = jnp.where(lane >= HEAD_DIM, r, ind_lo).astype(p_ref.dtype)
            elif s == 17 or s == 19:
                o = 18 if s == 17 else 21
                p_ref[:, o * LANES:(o + 1) * LANES] = jnp.where(lane < HEAD_DIM, y, 1.0).astype(p_ref.dtype)
                p_ref[:, (o + 1) * LANES:(o + 2) * LANES] = jnp.where(lane >= HEAD_DIM, y, 1.0).astype(p_ref.dtype)
            elif s == 18:
                p_ref[:, 20 * LANES:21 * LANES] = rope(y).astype(p_ref.dtype)
            elif s == 20:
                aux_ref[:, 0:LANES] = rope(y)
            elif s == 21:
                aux_ref[:, LANES:2 * LANES] = y
            else:
                aux_ref[:, 2 * LANES:3 * LANES] = jax.nn.sigmoid(y)


def _compress_kernel(tk_ref, tv_ref, kpl_ref, kph_ref, vpl_ref, vph_ref,
                     kwl_ref, kwh_ref, kw2_ref, vwl_ref, vwh_ref, vw2_ref, ko_ref, vo_ref, *, nc):
    lane = lax.broadcasted_iota(jnp.int32, (nc, CMP_STRIDE * LANES), 1)
    group_lanes = [(lane & HEAD_DIM) == 0, (lane & HEAD_DIM) != 0]

    def compress(t_ref, plo_ref, phi_ref, wlo_ref, whi_ref, w2_ref):
        x = jnp.concatenate([t_ref[pl.ds(l, nc, stride=CMP_STRIDE), :] for l in range(CMP_STRIDE)], axis=1)
        x_lo = x + plo_ref[...]
        x_hi = x + phi_ref[...]
        out = None
        for g in range(NSA_KV_GROUPS):
            a = _dot(jnp.where(group_lanes[g], x_lo, 0.0).astype(MXU_DTYPE), wlo_ref[...])
            b = _dot(jnp.where(group_lanes[g], x_hi, 0.0).astype(MXU_DTYPE), whi_ref[...])
            hid = a + pltpu.roll(b, nc - 1, 0)
            act = hid * jax.nn.sigmoid(hid)
            o = _dot(act.astype(MXU_DTYPE), w2_ref[g])
            out = o if out is None else out + o
        return out

    ko_ref[...] = compress(tk_ref, kpl_ref, kph_ref, kwl_ref, kwh_ref, kw2_ref).astype(ko_ref.dtype)
    vo_ref[...] = compress(tv_ref, vpl_ref, vph_ref, vwl_ref, vwh_ref, vw2_ref).astype(vo_ref.dtype)


def _flash_init(m_ref, l_ref, acc_ref):
    m_ref[...] = jnp.full(m_ref.shape, NEG_INF, F32)
    if l_ref is not None:
        l_ref[...] = jnp.zeros(l_ref.shape, F32)
    acc_ref[...] = jnp.zeros(acc_ref.shape, F32)


def _lane_tiles(x):
    return [x[:, c:c + LANES] for c in range(0, x.shape[1], LANES)]


def _stack_rows(x, n):
    return jnp.concatenate([x] * n, axis=0)


def _softmax_pv(s, v, m_ref, l_ref, acc_ref, bias=None):
    if bias is not None:
        s = s + bias
    n_tiles = s.shape[1] // LANES
    m_prev = m_ref[...]
    m_new = jnp.maximum(m_prev, jnp.max(s, axis=-1, keepdims=True))
    alpha = jnp.exp2(m_prev - m_new)
    x = s - jnp.concatenate([m_new] * n_tiles, axis=1)
    if l_ref is None:
        p = jnp.exp2(x.astype(MXU_DTYPE))
    else:
        p = jnp.exp2(x)
        l_ref[...] = alpha * l_ref[...] + functools.reduce(lambda a, b: a + b, _lane_tiles(p))
        p = p.astype(MXU_DTYPE)
    acc_ref[...] = alpha * acc_ref[...] + _dot(p, v)
    m_ref[...] = m_new


def _flash_finish(l_ref, acc_ref):
    if l_ref is None:
        return acc_ref[...]
    return acc_ref[...] / jnp.sum(l_ref[...], axis=-1, keepdims=True)


def _causal_flash(streams, s0, tq, tk):
    assert tk in (tq, 2 * tq) and len(streams) in (1, 2)
    rows = streams[0][0].shape[0]
    for _, _, _, m_ref, l_ref, acc_ref in streams:
        _flash_init(m_ref, l_ref, acc_ref)

    def step(stream, k0, width, bias=None):
        q, k_ref, v_ref, m_ref, l_ref, acc_ref = stream
        k0 = pl.multiple_of(k0, width)
        _softmax_pv(_nt_dot(q, k_ref[pl.ds(k0, width), :]), v_ref[pl.ds(k0, width), :],
                    m_ref, l_ref, acc_ref, bias)

    def tail_bias(width):
        qpos = (width - tq) + lax.broadcasted_iota(jnp.int32, (tq, width), 0)
        causal = lax.broadcasted_iota(jnp.int32, (tq, width), 1) <= qpos
        return _stack_rows(jnp.where(causal, 0.0, NEG_INF), rows // tq)

    n_full = s0 // tk
    aligned = (s0 - n_full * tk) == 0
    widths = (tq,) if tk == tq else (tq, tk)

    def tail(width, odd_tile=False):
        def body():
            bias = tail_bias(width)
            for stream in streams:
                if odd_tile:
                    step(stream, (n_full - 1) * tk, tk)
                step(stream, s0 + tq - width, width, bias)
        return body

    def when_width(width, extra=None):
        cond = aligned if width == tq else jnp.logical_not(aligned)
        if tk == tq:
            cond = extra
        elif extra is not None:
            cond = cond & extra
        return pl.when(cond) if cond is not None else (lambda f: f())

    def pair(jj, carry):
        for t in range(2):
            for stream in streams:
                step(stream, (2 * jj + t) * tk, tk)
        return carry

    lax.fori_loop(0, n_full // 2, pair, 0)
    for odd_tile in (False, True):
        parity = (n_full % 2 == 1) if odd_tile else (n_full % 2 == 0)
        for width in widths:
            when_width(width, parity)(tail(width, odd_tile))

    return [_flash_finish(l_ref, acc_ref) for _, _, _, _, l_ref, acc_ref in streams]


def _diff_kernel(lq1_ref, lk1_ref, lq2_ref, lk2_ref, subln_ref, q_ref, k_ref, v_ref, o_ref,
                 m_ref, l_ref, acc_ref, *, tq, tk, lam_init, heads):
    qi = pl.program_id(2)
    s0 = qi * tq
    lane = lax.broadcasted_iota(jnp.int32, (tq, LANES), 1)
    zero = jnp.zeros((tq, LANES), F32)
    streams = []
    for h in range(heads):
        slab = pl.ds(h * LANES, LANES)
        qf = q_ref[:, h * LANES:(h + 1) * LANES].astype(F32)
        q2 = jnp.concatenate([jnp.where(lane < HEAD_DIM, qf, zero),
                              jnp.where(lane >= HEAD_DIM, qf, zero)], axis=0).astype(MXU_DTYPE)
        streams.append((q2, k_ref.at[:, slab], v_ref.at[:, slab], m_ref.at[h], l_ref.at[h], acc_ref.at[h]))

    lam = (jnp.exp(jnp.sum(lq1_ref[...] * lk1_ref[...], axis=-1, keepdims=True))
           - jnp.exp(jnp.sum(lq2_ref[...] * lk2_ref[...], axis=-1, keepdims=True)) + lam_init)
    for h, o12 in enumerate(_causal_flash(streams, s0, tq, tk)):
        o = o12[:tq] - lam * o12[tq:]
        o_ref[:, h * LANES:(h + 1) * LANES] = (_rms(o, subln_ref[...]) * (1.0 - lam_init)).astype(o_ref.dtype)


def _nsa_kernel(q_ref, gate_ref, kc_ref, vc_ref, ovt_ref, ka0_ref, ka1_ref, vs0_ref, vs1_ref,
                kw_ref, vw0_ref, vw1_ref, o_ref, m_ref, acc_ref, qsel_ref, part_ref,
                *, tq, tk, nc, n_sel, top_n, n_q_tiles):
    hg = NSA_HEADS_PER_GROUP
    rows = hg * tq
    qi = pl.program_id(1)
    s0 = qi * tq
    lane = lax.broadcasted_iota(jnp.int32, (tq, LANES), 1)
    lane_rows = lax.broadcasted_iota(jnp.int32, (rows, LANES), 1)
    qf = [q_ref[:, h * LANES:(h + 1) * LANES].astype(F32) for h in range(hg)]
    gates = gate_ref[...]
    zero = jnp.zeros((tq, LANES), F32)

    wk = WINDOW + tq
    w0 = pl.multiple_of(jnp.maximum(s0 - WINDOW, 0), tq)
    back = (s0 - w0) + lax.broadcasted_iota(jnp.int32, (tq, wk), 0) \
        - lax.broadcasted_iota(jnp.int32, (tq, wk), 1)
    win_bias = _stack_rows(jnp.where((back >= 0) & (back < WINDOW), 0.0, NEG_INF), hg)

    def normalise(raw, g):
        in_half_rows = (lane_rows >= HEAD_DIM) if g else (lane_rows < HEAD_DIM)
        return raw / jnp.where(in_half_rows, pltpu.roll(raw, HEAD_DIM, 1), 1.0)

    def gate(g, h, branch):
        c = (g * hg + h) * N_BRANCHES + branch
        return gates[:, c:c + 1]

    def phase1(n_blk, nc_eff):
        n_idx = lax.broadcasted_iota(jnp.int32, (tq, nc_eff), 1)
        qpos_c = s0 + lax.broadcasted_iota(jnp.int32, (tq, nc_eff), 0)
        cmp_ok = n_idx * CMP_STRIDE + (CMP_BLOCK - 1) <= qpos_c
        cmp_bias = _stack_rows(jnp.where(cmp_ok, 0.0, NEG_INF), hg)
        cmp_keep = _stack_rows(jnp.where(cmp_ok, 1.0, 0.0), hg)
        jb = lax.broadcasted_iota(jnp.int32, (n_blk, tq), 0)
        qp = s0 + lax.broadcasted_iota(jnp.int32, (n_blk, tq), 1)
        cur = qp // SLC_BLOCK
        valid = jb * SLC_BLOCK <= qp
        forced = (jb == 0) | (jb == cur) | (jb == cur - 1)

        for g in range(NSA_KV_GROUPS):
            in_half = (lane >= HEAD_DIM) if g else (lane < HEAD_DIM)
            q_plain = jnp.concatenate([jnp.where(in_half, qf[h], zero) for h in range(hg)],
                                      axis=0).astype(MXU_DTYPE)

            sc = _nt_dot(q_plain, kc_ref[:nc_eff, :]) + cmp_bias
            pc = jnp.exp2(sc - jnp.max(sc, axis=-1, keepdims=True)) * cmp_keep
            lc = jnp.sum(pc, axis=-1, keepdims=True)
            pc = (pc / jnp.where(lc > 0.0, lc, 1.0)).astype(MXU_DTYPE)
            o_cmp = _dot(pc, vc_ref[:nc_eff, :])

            if n_blk <= top_n:
                selected = valid
            else:
                imp_t = None
                for h in range(hg):
                    part = _nt_dot(ovt_ref[:, :nc_eff], pc[h * tq:(h + 1) * tq])
                    imp_t = part if imp_t is None else imp_t + part
                key = jnp.where(forced, FORCED_IMPORTANCE, jnp.where(valid, imp_t[:n_blk], -1.0))
                n_grp = n_blk // SUBLANES
                key_g = [key[a * SUBLANES:(a + 1) * SUBLANES] for a in range(n_grp)]
                jb_g = lax.broadcasted_iota(jnp.int32, (SUBLANES, tq), 0)
                rank_g = [jnp.zeros((SUBLANES, tq), jnp.int32) for _ in range(n_grp)]
                for i in range(n_blk):
                    ri = key[i:i + 1, :]
                    for a in range(n_grp):
                        if a < i // SUBLANES:
                            ahead = jnp.where(ri > key_g[a], 1, 0)
                        elif a > i // SUBLANES:
                            ahead = jnp.where(ri >= key_g[a], 1, 0)
                        else:
                            ahead = (jnp.where(ri > key_g[a], 1, 0)
                                     + jnp.where((ri == key_g[a]) & (jb_g > i % SUBLANES), 1, 0))
                        rank_g[a] = rank_g[a] + ahead
                selected = (jnp.concatenate(rank_g, axis=0) < top_n) & valid
            bias_t = jnp.where(selected, 0.0, NEG_INF)
            if n_blk < SLC_BLOCK:
                bias_t = jnp.concatenate([bias_t, jnp.full((SLC_BLOCK - n_blk, tq), NEG_INF, F32)], axis=0)
            pad_t = jnp.zeros((SLC_BLOCK, tq), F32)
            bias = jnp.transpose(jnp.concatenate([pad_t, bias_t] if g == 0 else [bias_t, pad_t], axis=0))
            qsel_ref[g] = jnp.concatenate([jnp.where(in_half, qf[h], bias) for h in range(hg)],
                                          axis=0).astype(qsel_ref.dtype)

            sw = _nt_dot(q_plain, kw_ref[pl.ds(w0, wk), :]) + win_bias
            pw = jnp.exp2((sw - jnp.max(sw, axis=-1, keepdims=True)).astype(MXU_DTYPE))
            o_win = normalise(_dot(pw, (vw1_ref if g else vw0_ref)[pl.ds(w0, wk), :]), g)
            part_ref[g] = jnp.concatenate(
                [gate(g, h, 0) * o_cmp[h * tq:(h + 1) * tq] + gate(g, h, 2) * o_win[h * tq:(h + 1) * tq]
                 for h in range(hg)], axis=0)

    tiles_per_part = n_q_tiles // SEQ_PARTS
    for c in range(SEQ_PARTS):
        nc_part = min(nc, -(-(nc * (c + 1) // SEQ_PARTS) // LANES) * LANES)
        pl.when(qi // tiles_per_part == c)(
            functools.partial(phase1, n_sel * (c + 1) // SEQ_PARTS, nc_part))

    outs = []
    raw = _causal_flash([(qsel_ref[0], ka0_ref, vs0_ref, m_ref.at[0], None, acc_ref.at[0]),
                         (qsel_ref[1], ka1_ref, vs1_ref, m_ref.at[1], None, acc_ref.at[1])], s0, tq, tk)
    for g in range(NSA_KV_GROUPS):
        o_slc = normalise(raw[g], g)
        outs.append([part_ref[g, h * tq:(h + 1) * tq, :] + gate(g, h, 1) * o_slc[h * tq:(h + 1) * tq]
                     for h in range(hg)])

    for h in range(hg):
        o_ref[:, h * LANES:(h + 1) * LANES] = jnp.where(lane < HEAD_DIM, outs[0][h], outs[1][h]).astype(o_ref.dtype)


def _post_kernel(x_ref, od_ref, on_ref, wod_ref, won_ref, gpost_ref, gpre_ref, wg_ref, wu_ref, wd_ref,
                 gffn_ref, o_ref, act_ref, *, d_ff, ff_chunk, row_splits):
    tm = x_ref.shape[0] // row_splits
    for r in range(row_splits):
        rs = slice(r * tm, (r + 1) * tm)
        mix = _dot(od_ref[rs, :], wod_ref[...]) + _dot(on_ref[rs, :], won_ref[...])
        x1 = x_ref[rs, :] + _rms(mix, gpost_ref[...])
        h = _rms(x1, gpre_ref[...]).astype(MXU_DTYPE)
        for c in range(0, d_ff, ff_chunk):
            gate = _dot(h, wg_ref[:, c:c + ff_chunk])
            up = _dot(h, wu_ref[:, c:c + ff_chunk])
            act_ref[rs, c:c + ff_chunk] = (gate * jax.nn.sigmoid(gate) * up).astype(act_ref.dtype)
        f = _dot(act_ref[rs, :], wd_ref[...])
        o_ref[rs, :] = x1 + _rms(f, gffn_ref[...])


def _resident(shape):
    nd = len(shape)
    return pl.BlockSpec(shape, lambda *_: (0,) * nd, pipeline_mode=pl.Buffered(1))


def _params(sem):
    return pltpu.CompilerParams(dimension_semantics=sem, vmem_limit_bytes=VMEM_LIMIT)


def _rope_tables(S):
    inv = 1.0 / (ROPE_THETA ** (jnp.arange(0, HEAD_DIM, 2, dtype=F32) / HEAD_DIM))
    ang = jnp.arange(S, dtype=F32)[:, None] * inv[None, :]
    cos, sin = jnp.cos(ang), jnp.sin(ang)
    return jnp.tile(cos, (1, 4)), jnp.concatenate([-sin, sin, -sin, sin], axis=1)


def _selection_overlap_t(nc, n_cmp, n_sel):
    c0 = np.arange(n_cmp)[:, None] * CMP_STRIDE
    b0 = np.arange(n_sel)[None, :] * SLC_BLOCK
    ov = np.clip(np.minimum(c0 + CMP_BLOCK, b0 + SLC_BLOCK) - np.maximum(c0, b0), 0, None) / CMP_BLOCK
    full = np.zeros((LANES, nc), np.float32)
    full[:n_sel, :n_cmp] = ov.T
    return full


def _layer(x, layer, attn_pre_norm, w_in, lq1, lk1, lq2, lk2, diff_subln, k_pos, k_w1, k_w2,
           v_pos, v_w1, v_w2, w_out, attn_post_norm, ffn_pre_norm, w_gate, w_up, w_down, ffn_post_norm):
    B, S, D = x.shape
    N = B * S
    d_ff = w_gate.shape[1]
    nc = S // CMP_STRIDE
    n_cmp = (S - CMP_BLOCK) // CMP_STRIDE + 1
    n_sel = S // SLC_BLOCK
    top_n = min(SLC_TOPK, n_sel)
    assert n_sel <= SLC_BLOCK and S % 512 == 0
    lam_init = 0.8 - 0.6 * math.exp(-0.3 * layer)
    dt = MXU_DTYPE

    hg, G, d = NSA_HEADS_PER_GROUP, NSA_KV_GROUPS, HEAD_DIM
    head_starts = [(g * hg + h) * d for h in range(hg) for g in range(G)]
    w_cat = jnp.concatenate(
        [w_in[:, :1536]] + [w_in[:, 1536 + c:1536 + c + d] for c in head_starts]
        + [w_in[:, 2304:2816],
           w_in[:, 2048:2304],
           w_in[:, 2816:2840],
           jnp.zeros((D, LANES - hg * G * N_BRANCHES), w_in.dtype)], axis=1).astype(dt)
    w_out_d = w_out[:512].astype(dt)
    w_out_n = jnp.concatenate([w_out[512 + c:512 + c + d] for c in head_starts], axis=0).astype(dt)
    cos_t, sin_t = _rope_tables(S)

    tm = 1024
    seq_tiles = S // tm
    n_cols = (N_MAIN_SLABS + N_AUX_SLABS) * LANES
    p, aux = pl.pallas_call(
        functools.partial(_inproj_kernel, tm=tm, seq_tiles=seq_tiles),
        grid=(N // tm,),
        in_specs=[pl.BlockSpec((tm, D), lambda i: (i, 0)),
                  _resident((1, D)),
                  _resident((D, n_cols)),
                  pl.BlockSpec((tm, LANES), lambda i: (i % seq_tiles, 0)),
                  pl.BlockSpec((tm, LANES), lambda i: (i % seq_tiles, 0))],
        out_specs=[pl.BlockSpec((tm, N_P_SLABS * LANES), lambda i: (i, 0)),
                   pl.BlockSpec((tm, N_AUX_SLABS * LANES), lambda i: (i, 0))],
        out_shape=[jax.ShapeDtypeStruct((N, N_P_SLABS * LANES), dt),
                   jax.ShapeDtypeStruct((N, N_AUX_SLABS * LANES), F32)],
        compiler_params=_params(("parallel",)),
        name="inproj",
    )(x.reshape(N, D), attn_pre_norm.reshape(1, D), w_cat, cos_t, sin_t)
    p = p.reshape(B, S, N_P_SLABS * LANES)
    aux = aux.reshape(B, S, N_AUX_SLABS * LANES)

    def pos_rows(pos):
        tiled = jnp.broadcast_to(pos.reshape(2, CMP_STRIDE, 1, d), (2, CMP_STRIDE, G, d))
        return tiled[0].reshape(1, -1), tiled[1].reshape(1, -1)

    def w1_rows(w1):
        w = jnp.broadcast_to(w1.astype(dt).reshape(2, CMP_STRIDE, 1, d, CMP_HIDDEN),
                             (2, CMP_STRIDE, G, d, CMP_HIDDEN)).reshape(2, CMP_STRIDE * G * d, CMP_HIDDEN)
        return w[0], w[1]

    def w2_halves(w2):
        return jnp.stack([jnp.pad(w2, ((0, 0), (g * d, (G - 1 - g) * d))) for g in range(G)]).astype(dt)

    kpl, kph = pos_rows(k_pos)
    vpl, vph = pos_rows(v_pos)
    kwl, kwh = w1_rows(k_w1)
    vwl, vwh = w1_rows(v_w1)
    chunk_w = CMP_STRIDE * G * d
    w1_spec = _resident((chunk_w, CMP_HIDDEN))
    w2_spec = _resident((G, CMP_HIDDEN, LANES))
    kcmp, vcmp = pl.pallas_call(
        functools.partial(_compress_kernel, nc=nc),
        grid=(B,),
        in_specs=[pl.BlockSpec((None, S, LANES), lambda b: (b, 0, 0)),
                  pl.BlockSpec((None, S, LANES), lambda b: (b, 0, 1)),
                  _resident((1, chunk_w)), _resident((1, chunk_w)), _resident((1, chunk_w)), _resident((1, chunk_w)),
                  w1_spec, w1_spec, w2_spec, w1_spec, w1_spec, w2_spec],
        out_specs=[pl.BlockSpec((None, nc, LANES), lambda b: (b, 0, 0)),
                   pl.BlockSpec((None, nc, LANES), lambda b: (b, 0, 0))],
        out_shape=[jax.ShapeDtypeStruct((B, nc, LANES), dt), jax.ShapeDtypeStruct((B, nc, LANES), dt)],
        compiler_params=_params(("parallel",)),
        name="compress",
    )(aux, aux, kpl, kph, vpl, vph, kwl, kwh, w2_halves(k_w2), vwl, vwh, w2_halves(v_w2))

    tq_d, tk_d = 512, 1024
    lam_vec = [v.reshape(1, d) for v in (lq1, lk1, lq2, lk2)]
    hp = 2
    hw = hp * LANES
    n_hp = DIFF_HEADS // hp
    o_diff = pl.pallas_call(
        functools.partial(_diff_kernel, tq=tq_d, tk=tk_d, lam_init=lam_init, heads=hp),
        grid=(B, n_hp, S // tq_d),
        in_specs=[_resident((1, d))] * 4 + [_resident((1, LANES)),
                  pl.BlockSpec((None, tq_d, hw), lambda b, h, i: (b, i, h)),
                  pl.BlockSpec((None, S, hw), lambda b, h, i: (b, 0, n_hp + h)),
                  pl.BlockSpec((None, S, hw), lambda b, h, i: (b, 0, 2 * n_hp + h))],
        out_specs=pl.BlockSpec((None, tq_d, hw), lambda b, h, i: (b, i, h)),
        out_shape=jax.ShapeDtypeStruct((B, S, DIFF_HEADS * LANES), dt),
        scratch_shapes=[pltpu.VMEM((hp, 2 * tq_d, LANES), F32)] * 3,
        compiler_params=_params(("parallel", "parallel", "arbitrary")),
        name="diff_attn",
    )(*lam_vec, diff_subln.reshape(1, LANES), p, p, p)

    tq_n, tk_n = 256, 512
    ovt = jnp.asarray(_selection_overlap_t(nc, n_cmp, n_sel)).astype(dt)
    rows = NSA_HEADS_PER_GROUP * tq_n

    def seq_slab(c):
        return pl.BlockSpec((None, S, LANES), lambda b, i: (b, 0, c))

    o_nsa = pl.pallas_call(
        functools.partial(_nsa_kernel, tq=tq_n, tk=tk_n, nc=nc, n_sel=n_sel, top_n=top_n, n_q_tiles=S // tq_n),
        grid=(B, S // tq_n),
        in_specs=[pl.BlockSpec((None, tq_n, 4 * LANES), lambda b, i: (b, i, 3)),
                  pl.BlockSpec((None, tq_n, LANES), lambda b, i: (b, i, 2)),
                  pl.BlockSpec((None, nc, LANES), lambda b, i: (b, 0, 0)),
                  pl.BlockSpec((None, nc, LANES), lambda b, i: (b, 0, 0)),
                  _resident((LANES, nc)),
                  seq_slab(16), seq_slab(17), seq_slab(18), seq_slab(19), seq_slab(20), seq_slab(21),
                  seq_slab(22)],
        out_specs=pl.BlockSpec((None, tq_n, 4 * LANES), lambda b, i: (b, i, 0)),
        out_shape=jax.ShapeDtypeStruct((B, S, 4 * LANES), dt),
        scratch_shapes=[pltpu.VMEM((G, rows, LANES), F32)] * 2
        + [pltpu.VMEM((G, rows, LANES), dt), pltpu.VMEM((G, rows, LANES), F32)],
        compiler_params=_params(("parallel", "arbitrary")),
        name="nsa_attn",
    )(p, aux, kcmp, vcmp, ovt, p, p, p, p, p, p, p)

    tm2 = 512
    row = lambda i: (i, 0)
    out = pl.pallas_call(
        functools.partial(_post_kernel, d_ff=d_ff, ff_chunk=256, row_splits=1),
        grid=(N // tm2,),
        in_specs=[pl.BlockSpec((tm2, D), row),
                  pl.BlockSpec((tm2, 512), row), pl.BlockSpec((tm2, 512), row),
                  _resident((512, D)), _resident((512, D)), _resident((1, D)), _resident((1, D)),
                  _resident((D, d_ff)), _resident((D, d_ff)), _resident((d_ff, D)), _resident((1, D))],
        out_specs=pl.BlockSpec((tm2, D), row),
        out_shape=jax.ShapeDtypeStruct((N, D), F32),
        scratch_shapes=[pltpu.VMEM((tm2, d_ff), dt)],
        compiler_params=_params(("parallel",)),
        name="post",
    )(x.reshape(N, D), o_diff.reshape(N, 512), o_nsa.reshape(N, 512), w_out_d, w_out_n,
      attn_post_norm.reshape(1, D), ffn_pre_norm.reshape(1, D),
      w_gate.astype(dt), w_up.astype(dt), w_down.astype(dt), ffn_post_norm.reshape(1, D))
    return out.reshape(B, S, D)


def kernel(x, attn_pre_norm, w_in, lambda_q1, lambda_k1, lambda_q2, lambda_k2, diff_subln, k_cmp_pos, k_cmp_w1, k_cmp_w2, v_cmp_pos, v_cmp_w1, v_cmp_w2, w_out, attn_post_norm, ffn_pre_norm, w_gate, w_up, w_down, ffn_post_norm):
    for l in range(w_in.shape[0]):
        x = _layer(x, l, attn_pre_norm[l], w_in[l], lambda_q1[l], lambda_k1[l], lambda_q2[l], lambda_k2[l],
                   diff_subln[l], k_cmp_pos[l], k_cmp_w1[l], k_cmp_w2[l], v_cmp_pos[l], v_cmp_w1[l], v_cmp_w2[l],
                   w_out[l], attn_post_norm[l], ffn_pre_norm[l], w_gate[l], w_up[l], w_down[l], ffn_post_norm[l])
    return x
```

```python
import functools
import math

import numpy as np
import jax
import jax.numpy as jnp
from jax import lax
from jax.experimental import pallas as pl
from jax.experimental.pallas import tpu as pltpu

F32 = jnp.float32
MXU_DTYPE = jnp.bfloat16

LANES = 128
SUBLANES = 8
HEAD_DIM = 64
ROPE_THETA = 10000.0
NORM_EPS = 1e-6
NEG_INF = -1e30
LOG2_E = 1.4426950408889634
FORCED_IMPORTANCE = 3e38

DIFF_HEADS = 4
NSA_HEADS_PER_GROUP = 4
NSA_KV_GROUPS = 2
CMP_BLOCK = 32
CMP_STRIDE = 16
CMP_HIDDEN = 4 * HEAD_DIM
SLC_BLOCK = 64
SLC_TOPK = 16
WINDOW = 512
N_BRANCHES = 3
SEQ_PARTS = 4

M_DQ, M_DK, M_DV, M_NQ, M_KS, M_VS, M_KW, M_VW, M_KC, M_VC, M_GATE = 0, 4, 8, 12, 16, 17, 18, 19, 20, 21, 22
N_MAIN_SLABS = 20
N_AUX_SLABS = 3
P_DQ, P_DK, P_DV, P_NQ, P_KS, P_VS, P_KW, P_VW = 0, 4, 8, 12, 16, 18, 20, 21
N_P_SLABS = 23
A_KC, A_VC, A_GATE = 0, 1, 2
VMEM_LIMIT = 56 * 1024 * 1024

MXU_WIDTH = 256
INPROJ_ROWS = 1024
POST_ROWS = 512
FF_CHUNK = MXU_WIDTH
DIFF_TILE = (512, 1024)
DIFF_HEADS_PER_STEP = 4
NSA_TILE = (256, 512)


def _nt_dot(a, b):
    return lax.dot_general(a, b, (((1,), (1,)), ((), ())), preferred_element_type=F32)


def _dot(a, b):
    return jnp.dot(a, b, preferred_element_type=F32)


def _rms(x, g):
    return x * lax.rsqrt(jnp.mean(x * x, axis=-1, keepdims=True) + NORM_EPS) * g


def _inproj_kernel(x_ref, g_ref, w_ref, cos_ref, sin_ref, p_ref, aux_ref, *, tm, seq_tiles):
    h = _rms(x_ref[...], g_ref[...]).astype(MXU_DTYPE)
    cos = cos_ref[...]
    sin = sin_ref[...]
    lane = lax.broadcasted_iota(jnp.int32, (tm, LANES), 1)
    low_half = (lane & (HEAD_DIM - 1)) < HEAD_DIM // 2

    def rope(y):
        fwd = pltpu.roll(y, HEAD_DIM // 2, 1)
        bwd = pltpu.roll(y, LANES - HEAD_DIM // 2, 1)
        return y * cos + jnp.where(low_half, bwd, fwd) * sin

    pos = (pl.program_id(0) % seq_tiles) * tm + lax.broadcasted_iota(jnp.int32, (tm, LANES), 0)
    blk = pos // SLC_BLOCK
    scale = HEAD_DIM ** -0.5 * LOG2_E

    n_slabs = N_MAIN_SLABS + N_AUX_SLABS
    per_dot = MXU_WIDTH // LANES
    for c0 in range(0, n_slabs, per_dot):
        c1 = min(c0 + per_dot, n_slabs)
        y2 = _dot(h, w_ref[:, c0 * LANES:c1 * LANES])
        for s in range(c0, c1):
            y = y2[:, (s - c0) * LANES:(s - c0 + 1) * LANES]

            def put(ref, slab, val):
                ref[:, slab * LANES:(slab + 1) * LANES] = val.astype(ref.dtype)

            if M_DQ <= s < M_DK or M_NQ <= s < M_KS:
                put(p_ref, s, rope(y) * scale)
            elif M_DK <= s < M_DV:
                put(p_ref, s, rope(y))
            elif M_DV <= s < M_NQ:
                put(p_ref, s, y)
            elif s == M_KS:
                r = rope(y)
                put(p_ref, P_KS, jnp.where(lane < HEAD_DIM, r, jnp.where(lane - HEAD_DIM == blk, 1.0, 0.0)))
                put(p_ref, P_KS + 1, jnp.where(lane >= HEAD_DIM, r, jnp.where(lane == blk, 1.0, 0.0)))
            elif s in (M_VS, M_VW):
                o = P_VS if s == M_VS else P_VW
                put(p_ref, o, jnp.where(lane < HEAD_DIM, y, 1.0))
                put(p_ref, o + 1, jnp.where(lane >= HEAD_DIM, y, 1.0))
            elif s == M_KW:
                put(p_ref, P_KW, rope(y))
            elif s == M_KC:
                put(aux_ref, A_KC, rope(y))
            elif s == M_VC:
                put(aux_ref, A_VC, y)
            else:
                put(aux_ref, A_GATE, jax.nn.sigmoid(y))


def _compress_kernel(tk_ref, tv_ref, kpl_ref, kph_ref, vpl_ref, vph_ref,
                     kwl_ref, kwh_ref, kw2_ref, vwl_ref, vwh_ref, vw2_ref, ko_ref, vo_ref, *, nc):
    lane = lax.broadcasted_iota(jnp.int32, (nc, CMP_STRIDE * LANES), 1)
    group_lanes = [(lane & HEAD_DIM) == 0, (lane & HEAD_DIM) != 0]

    def compress(t_ref, plo_ref, phi_ref, wlo_ref, whi_ref, w2_ref):
        x = jnp.concatenate([t_ref[pl.ds(l, nc, stride=CMP_STRIDE), :] for l in range(CMP_STRIDE)], axis=1)
        x_lo = x + plo_ref[...]
        x_hi = x + phi_ref[...]
        out = None
        for g in range(NSA_KV_GROUPS):
            a = _dot(jnp.where(group_lanes[g], x_lo, 0.0).astype(MXU_DTYPE), wlo_ref[...])
            b = _dot(jnp.where(group_lanes[g], x_hi, 0.0).astype(MXU_DTYPE), whi_ref[...])
            hid = a + pltpu.roll(b, nc - 1, 0)
            act = hid * jax.nn.sigmoid(hid)
            o = _dot(act.astype(MXU_DTYPE), w2_ref[g])
            out = o if out is None else out + o
        return out

    ko_ref[...] = compress(tk_ref, kpl_ref, kph_ref, kwl_ref, kwh_ref, kw2_ref).astype(ko_ref.dtype)
    vo_ref[...] = compress(tv_ref, vpl_ref, vph_ref, vwl_ref, vwh_ref, vw2_ref).astype(vo_ref.dtype)


def _flash_init(m_ref, l_ref, acc_ref):
    m_ref[...] = jnp.full(m_ref.shape, NEG_INF, F32)
    if l_ref is not None:
        l_ref[...] = jnp.zeros(l_ref.shape, F32)
    acc_ref[...] = jnp.zeros(acc_ref.shape, F32)


def _lane_tiles(x):
    return [x[:, c:c + LANES] for c in range(0, x.shape[1], LANES)]


def _stack_rows(x, n):
    return jnp.concatenate([x] * n, axis=0)


def _softmax_pv(s, v, m_ref, l_ref, acc_ref, bias=None):
    if bias is not None:
        s = s + bias
    n_tiles = s.shape[1] // LANES
    m_prev = m_ref[...]
    m_new = jnp.maximum(m_prev, jnp.max(s, axis=-1, keepdims=True))
    alpha = jnp.exp2(m_prev - m_new)
    x = s - jnp.concatenate([m_new] * n_tiles, axis=1)
    if l_ref is None:
        p = jnp.exp2(x.astype(MXU_DTYPE))
    else:
        p = jnp.exp2(x)
        l_ref[...] = alpha * l_ref[...] + functools.reduce(lambda a, b: a + b, _lane_tiles(p))
        p = p.astype(MXU_DTYPE)
    acc_ref[...] = alpha * acc_ref[...] + _dot(p, v)
    m_ref[...] = m_new


def _flash_finish(l_ref, acc_ref):
    if l_ref is None:
        return acc_ref[...]
    return acc_ref[...] / jnp.sum(l_ref[...], axis=-1, keepdims=True)


def _causal_flash(streams, s0, tq, tk):
    assert tk in (tq, 2 * tq)
    rows = streams[0][0].shape[0]
    for _, _, _, m_ref, l_ref, acc_ref in streams:
        _flash_init(m_ref, l_ref, acc_ref)

    def step(stream, k0, width, bias=None):
        q, k_ref, v_ref, m_ref, l_ref, acc_ref = stream
        k0 = pl.multiple_of(k0, width)
        _softmax_pv(_nt_dot(q, k_ref[pl.ds(k0, width), :]), v_ref[pl.ds(k0, width), :],
                    m_ref, l_ref, acc_ref, bias)

    def tail_bias(width):
        qpos = (width - tq) + lax.broadcasted_iota(jnp.int32, (tq, width), 0)
        causal = lax.broadcasted_iota(jnp.int32, (tq, width), 1) <= qpos
        return _stack_rows(jnp.where(causal, 0.0, NEG_INF), rows // tq)

    n_full = s0 // tk
    aligned = (s0 - n_full * tk) == 0
    widths = (tq,) if tk == tq else (tq, tk)

    def tail(width, odd_tile=False):
        def body():
            bias = tail_bias(width)
            for stream in streams:
                if odd_tile:
                    step(stream, (n_full - 1) * tk, tk)
                step(stream, s0 + tq - width, width, bias)
        return body

    def when_width(width, extra=None):
        cond = aligned if width == tq else jnp.logical_not(aligned)
        if tk == tq:
            cond = extra
        elif extra is not None:
            cond = cond & extra
        return pl.when(cond) if cond is not None else (lambda f: f())

    def pair(jj, carry):
        for t in range(2):
            for stream in streams:
                step(stream, (2 * jj + t) * tk, tk)
        return carry

    lax.fori_loop(0, n_full // 2, pair, 0)
    for odd_tile in (False, True):
        parity = (n_full % 2 == 1) if odd_tile else (n_full % 2 == 0)
        for width in widths:
            when_width(width, parity)(tail(width, odd_tile))

    return [_flash_finish(l_ref, acc_ref) for _, _, _, _, l_ref, acc_ref in streams]


def _diff_kernel(lq1_ref, lk1_ref, lq2_ref, lk2_ref, subln_ref, q_ref, k_ref, v_ref, o_ref,
                 m_ref, l_ref, acc_ref, *, tq, tk, lam_init, heads):
    qi = pl.program_id(2)
    s0 = qi * tq
    lane = lax.broadcasted_iota(jnp.int32, (tq, LANES), 1)
    zero = jnp.zeros((tq, LANES), F32)
    streams = []
    for h in range(heads):
        slab = pl.ds(h * LANES, LANES)
        qf = q_ref[:, h * LANES:(h + 1) * LANES].astype(F32)
        q2 = jnp.concatenate([jnp.where(lane < HEAD_DIM, qf, zero),
                              jnp.where(lane >= HEAD_DIM, qf, zero)], axis=0).astype(MXU_DTYPE)
        streams.append((q2, k_ref.at[:, slab], v_ref.at[:, slab], m_ref.at[h], l_ref.at[h], acc_ref.at[h]))

    lam = (jnp.exp(jnp.sum(lq1_ref[...] * lk1_ref[...], axis=-1, keepdims=True))
           - jnp.exp(jnp.sum(lq2_ref[...] * lk2_ref[...], axis=-1, keepdims=True)) + lam_init)
    for h, o12 in enumerate(_causal_flash(streams, s0, tq, tk)):
        o = o12[:tq] - lam * o12[tq:]
        o_ref[:, h * LANES:(h + 1) * LANES] = (_rms(o, subln_ref[...]) * (1.0 - lam_init)).astype(o_ref.dtype)


def _nsa_kernel(q_ref, gate_ref, kc_ref, vc_ref, ovt_ref, ka0_ref, ka1_ref, vs0_ref, vs1_ref,
                kw_ref, vw0_ref, vw1_ref, o_ref, m_ref, acc_ref, qsel_ref, part_ref,
                *, tq, tk, nc, n_sel, top_n, n_q_tiles):
    hg = NSA_HEADS_PER_GROUP
    rows = hg * tq
    qi = pl.program_id(1)
    s0 = qi * tq
    lane = lax.broadcasted_iota(jnp.int32, (tq, LANES), 1)
    lane_rows = lax.broadcasted_iota(jnp.int32, (rows, LANES), 1)
    qf = [q_ref[:, h * LANES:(h + 1) * LANES].astype(F32) for h in range(hg)]
    gates = gate_ref[...]
    zero = jnp.zeros((tq, LANES), F32)

    wk = WINDOW + tq
    w0 = pl.multiple_of(jnp.maximum(s0 - WINDOW, 0), tq)
    back = (s0 - w0) + lax.broadcasted_iota(jnp.int32, (tq, wk), 0) \
        - lax.broadcasted_iota(jnp.int32, (tq, wk), 1)
    win_bias = _stack_rows(jnp.where((back >= 0) & (back < WINDOW), 0.0, NEG_INF), hg)

    def normalise(raw, g):
        in_half_rows = (lane_rows >= HEAD_DIM) if g else (lane_rows < HEAD_DIM)
        return raw / jnp.where(in_half_rows, pltpu.roll(raw, HEAD_DIM, 1), 1.0)

    def gate(g, h, branch):
        c = (g * hg + h) * N_BRANCHES + branch
        return gates[:, c:c + 1]

    def phase1(n_blk, nc_eff):
        n_idx = lax.broadcasted_iota(jnp.int32, (tq, nc_eff), 1)
        qpos_c = s0 + lax.broadcasted_iota(jnp.int32, (tq, nc_eff), 0)
        cmp_ok = n_idx * CMP_STRIDE + (CMP_BLOCK - 1) <= qpos_c
        cmp_bias = _stack_rows(jnp.where(cmp_ok, 0.0, NEG_INF), hg)
        cmp_keep = _stack_rows(jnp.where(cmp_ok, 1.0, 0.0), hg)
        jb = lax.broadcasted_iota(jnp.int32, (n_blk, tq), 0)
        qp = s0 + lax.broadcasted_iota(jnp.int32, (n_blk, tq), 1)
        cur = qp // SLC_BLOCK
        valid = jb * SLC_BLOCK <= qp
        forced = (jb == 0) | (jb == cur) | (jb == cur - 1)

        for g in range(NSA_KV_GROUPS):
            in_half = (lane >= HEAD_DIM) if g else (lane < HEAD_DIM)
            q_plain = jnp.concatenate([jnp.where(in_half, qf[h], zero) for h in range(hg)],
                                      axis=0).astype(MXU_DTYPE)

            sc = _nt_dot(q_plain, kc_ref[:nc_eff, :]) + cmp_bias
            pc = jnp.exp2(sc - jnp.max(sc, axis=-1, keepdims=True)) * cmp_keep
            lc = jnp.sum(pc, axis=-1, keepdims=True)
            pc = (pc / jnp.where(lc > 0.0, lc, 1.0)).astype(MXU_DTYPE)
            o_cmp = _dot(pc, vc_ref[:nc_eff, :])

            if n_blk <= top_n:
                selected = valid
            else:
                imp_t = None
                for h in range(hg):
                    part = _nt_dot(ovt_ref[:, :nc_eff], pc[h * tq:(h + 1) * tq])
                    imp_t = part if imp_t is None else imp_t + part
                key = jnp.where(forced, FORCED_IMPORTANCE, jnp.where(valid, imp_t[:n_blk], -1.0))
                n_grp = n_blk // SUBLANES
                key_g = [key[a * SUBLANES:(a + 1) * SUBLANES] for a in range(n_grp)]
                jb_g = lax.broadcasted_iota(jnp.int32, (SUBLANES, tq), 0)
                rank_g = [jnp.zeros((SUBLANES, tq), jnp.int32) for _ in range(n_grp)]
                for i in range(n_blk):
                    ri = key[i:i + 1, :]
                    for a in range(n_grp):
                        if a < i // SUBLANES:
                            ahead = jnp.where(ri > key_g[a], 1, 0)
                        elif a > i // SUBLANES:
                            ahead = jnp.where(ri >= key_g[a], 1, 0)
                        else:
                            ahead = (jnp.where(ri > key_g[a], 1, 0)
                                     + jnp.where((ri == key_g[a]) & (jb_g > i % SUBLANES), 1, 0))
                        rank_g[a] = rank_g[a] + ahead
                selected = (jnp.concatenate(rank_g, axis=0) < top_n) & valid
            bias_t = jnp.where(selected, 0.0, NEG_INF)
            if n_blk < SLC_BLOCK:
                bias_t = jnp.concatenate([bias_t, jnp.full((SLC_BLOCK - n_blk, tq), NEG_INF, F32)], axis=0)
            pad_t = jnp.zeros((SLC_BLOCK, tq), F32)
            bias = jnp.transpose(jnp.concatenate([pad_t, bias_t] if g == 0 else [bias_t, pad_t], axis=0))
            qsel_ref[g] = jnp.concatenate([jnp.where(in_half, qf[h], bias) for h in range(hg)],
                                          axis=0).astype(qsel_ref.dtype)

            sw = _nt_dot(q_plain, kw_ref[pl.ds(w0, wk), :]) + win_bias
            pw = jnp.exp2((sw - jnp.max(sw, axis=-1, keepdims=True)).astype(MXU_DTYPE))
            o_win = normalise(_dot(pw, (vw1_ref if g else vw0_ref)[pl.ds(w0, wk), :]), g)
            part_ref[g] = jnp.concatenate(
                [gate(g, h, 0) * o_cmp[h * tq:(h + 1) * tq] + gate(g, h, 2) * o_win[h * tq:(h + 1) * tq]
                 for h in range(hg)], axis=0)

    tiles_per_part = n_q_tiles // SEQ_PARTS
    for c in range(SEQ_PARTS):
        nc_part = min(nc, -(-(nc * (c + 1) // SEQ_PARTS) // LANES) * LANES)
        pl.when(qi // tiles_per_part == c)(
            functools.partial(phase1, n_sel * (c + 1) // SEQ_PARTS, nc_part))

    outs = []
    raw = _causal_flash([(qsel_ref[0], ka0_ref, vs0_ref, m_ref.at[0], None, acc_ref.at[0]),
                         (qsel_ref[1], ka1_ref, vs1_ref, m_ref.at[1], None, acc_ref.at[1])], s0, tq, tk)
    for g in range(NSA_KV_GROUPS):
        o_slc = normalise(raw[g], g)
        outs.append([part_ref[g, h * tq:(h + 1) * tq, :] + gate(g, h, 1) * o_slc[h * tq:(h + 1) * tq]
                     for h in range(hg)])

    for h in range(hg):
        o_ref[:, h * LANES:(h + 1) * LANES] = jnp.where(lane < HEAD_DIM, outs[0][h], outs[1][h]).astype(o_ref.dtype)


def _post_kernel(x_ref, od_ref, on_ref, wod_ref, won_ref, gpost_ref, gpre_ref, wg_ref, wu_ref, wd_ref,
                 gffn_ref, o_ref, act_ref, *, d_ff):
    mix = _dot(od_ref[...], wod_ref[...]) + _dot(on_ref[...], won_ref[...])
    x1 = x_ref[...] + _rms(mix, gpost_ref[...])
    h = _rms(x1, gpre_ref[...]).astype(MXU_DTYPE)
    for c in range(0, d_ff, FF_CHUNK):
        gate = _dot(h, wg_ref[:, c:c + FF_CHUNK])
        up = _dot(h, wu_ref[:, c:c + FF_CHUNK])
        act_ref[:, c:c + FF_CHUNK] = (gate * jax.nn.sigmoid(gate) * up).astype(act_ref.dtype)
    f = _dot(act_ref[...], wd_ref[...])
    o_ref[...] = x1 + _rms(f, gffn_ref[...])


def _resident(shape):
    nd = len(shape)
    return pl.BlockSpec(shape, lambda *_: (0,) * nd, pipeline_mode=pl.Buffered(1))


def _params(sem):
    return pltpu.CompilerParams(dimension_semantics=sem, vmem_limit_bytes=VMEM_LIMIT)


def _rope_tables(S):
    inv = 1.0 / (ROPE_THETA ** (jnp.arange(0, HEAD_DIM, 2, dtype=F32) / HEAD_DIM))
    ang = jnp.arange(S, dtype=F32)[:, None] * inv[None, :]
    cos, sin = jnp.cos(ang), jnp.sin(ang)
    return jnp.tile(cos, (1, 4)), jnp.concatenate([-sin, sin, -sin, sin], axis=1)


def _selection_overlap_t(nc, n_cmp, n_sel):
    c0 = np.arange(n_cmp)[:, None] * CMP_STRIDE
    b0 = np.arange(n_sel)[None, :] * SLC_BLOCK
    ov = np.clip(np.minimum(c0 + CMP_BLOCK, b0 + SLC_BLOCK) - np.maximum(c0, b0), 0, None) / CMP_BLOCK
    full = np.zeros((LANES, nc), np.float32)
    full[:n_sel, :n_cmp] = ov.T
    return full


def _layer(x, layer, attn_pre_norm, w_in, lq1, lk1, lq2, lk2, diff_subln, k_pos, k_w1, k_w2,
           v_pos, v_w1, v_w2, w_out, attn_post_norm, ffn_pre_norm, w_gate, w_up, w_down, ffn_post_norm):
    B, S, D = x.shape
    N = B * S
    d_ff = w_gate.shape[1]
    nc = S // CMP_STRIDE
    n_cmp = (S - CMP_BLOCK) // CMP_STRIDE + 1
    n_sel = S // SLC_BLOCK
    top_n = min(SLC_TOPK, n_sel)
    assert n_sel <= SLC_BLOCK
    assert S % max(INPROJ_ROWS, DIFF_TILE[1], NSA_TILE[1]) == 0 and S >= WINDOW + NSA_TILE[0]
    assert (S // NSA_TILE[0]) % SEQ_PARTS == 0 and (n_sel // SEQ_PARTS) % SUBLANES == 0
    lam_init = 0.8 - 0.6 * math.exp(-0.3 * layer)
    dt = MXU_DTYPE

    hg, G, d = NSA_HEADS_PER_GROUP, NSA_KV_GROUPS, HEAD_DIM
    diff_w = DIFF_HEADS * 2 * d
    nsa_w = G * hg * d
    kv_w = G * d
    off_nq = 3 * diff_w
    off_kc = off_nq + nsa_w
    off_ks = off_kc + 2 * kv_w
    off_gate = off_kc + 6 * kv_w
    n_gates = G * hg * N_BRANCHES
    head_starts = [(g * hg + h) * d for h in range(hg) for g in range(G)]
    w_cat = jnp.concatenate(
        [w_in[:, :off_nq]] + [w_in[:, off_nq + c:off_nq + c + d] for c in head_starts]
        + [w_in[:, off_ks:off_gate],
           w_in[:, off_kc:off_ks],
           w_in[:, off_gate:off_gate + n_gates],
           jnp.zeros((D, LANES - n_gates), w_in.dtype)], axis=1).astype(dt)
    w_out_d = w_out[:diff_w].astype(dt)
    w_out_n = jnp.concatenate([w_out[diff_w + c:diff_w + c + d] for c in head_starts], axis=0).astype(dt)
    cos_t, sin_t = _rope_tables(S)

    tm = INPROJ_ROWS
    seq_tiles = S // tm
    n_cols = (N_MAIN_SLABS + N_AUX_SLABS) * LANES
    p, aux = pl.pallas_call(
        functools.partial(_inproj_kernel, tm=tm, seq_tiles=seq_tiles),
        grid=(N // tm,),
        in_specs=[pl.BlockSpec((tm, D), lambda i: (i, 0)),
                  _resident((1, D)),
                  _resident((D, n_cols)),
                  pl.BlockSpec((tm, LANES), lambda i: (i % seq_tiles, 0)),
                  pl.BlockSpec((tm, LANES), lambda i: (i % seq_tiles, 0))],
        out_specs=[pl.BlockSpec((tm, N_P_SLABS * LANES), lambda i: (i, 0)),
                   pl.BlockSpec((tm, N_AUX_SLABS * LANES), lambda i: (i, 0))],
        out_shape=[jax.ShapeDtypeStruct((N, N_P_SLABS * LANES), dt),
                   jax.ShapeDtypeStruct((N, N_AUX_SLABS * LANES), F32)],
        compiler_params=_params(("parallel",)),
        name="inproj",
    )(x.reshape(N, D), attn_pre_norm.reshape(1, D), w_cat, cos_t, sin_t)
    p = p.reshape(B, S, N_P_SLABS * LANES)
    aux = aux.reshape(B, S, N_AUX_SLABS * LANES)

    def pos_rows(pos):
        tiled = jnp.broadcast_to(pos.reshape(2, CMP_STRIDE, 1, d), (2, CMP_STRIDE, G, d))
        return tiled[0].reshape(1, -1), tiled[1].reshape(1, -1)

    def w1_rows(w1):
        w = jnp.broadcast_to(w1.astype(dt).reshape(2, CMP_STRIDE, 1, d, CMP_HIDDEN),
                             (2, CMP_STRIDE, G, d, CMP_HIDDEN)).reshape(2, CMP_STRIDE * G * d, CMP_HIDDEN)
        return w[0], w[1]

    def w2_halves(w2):
        return jnp.stack([jnp.pad(w2, ((0, 0), (g * d, (G - 1 - g) * d))) for g in range(G)]).astype(dt)

    kpl, kph = pos_rows(k_pos)
    vpl, vph = pos_rows(v_pos)
    kwl, kwh = w1_rows(k_w1)
    vwl, vwh = w1_rows(v_w1)
    chunk_w = CMP_STRIDE * G * d
    w1_spec = _resident((chunk_w, CMP_HIDDEN))
    w2_spec = _resident((G, CMP_HIDDEN, LANES))
    kcmp, vcmp = pl.pallas_call(
        functools.partial(_compress_kernel, nc=nc),
        grid=(B,),
        in_specs=[pl.BlockSpec((None, S, LANES), lambda b: (b, 0, A_KC)),
                  pl.BlockSpec((None, S, LANES), lambda b: (b, 0, A_VC)),
                  _resident((1, chunk_w)), _resident((1, chunk_w)), _resident((1, chunk_w)), _resident((1, chunk_w)),
                  w1_spec, w1_spec, w2_spec, w1_spec, w1_spec, w2_spec],
        out_specs=[pl.BlockSpec((None, nc, LANES), lambda b: (b, 0, 0)),
                   pl.BlockSpec((None, nc, LANES), lambda b: (b, 0, 0))],
        out_shape=[jax.ShapeDtypeStruct((B, nc, LANES), dt), jax.ShapeDtypeStruct((B, nc, LANES), dt)],
        compiler_params=_params(("parallel",)),
        name="compress",
    )(aux, aux, kpl, kph, vpl, vph, kwl, kwh, w2_halves(k_w2), vwl, vwh, w2_halves(v_w2))

    tq_d, tk_d = DIFF_TILE
    lam_vec = [v.reshape(1, d) for v in (lq1, lk1, lq2, lk2)]
    hp = DIFF_HEADS_PER_STEP
    hw = hp * LANES
    n_hp = DIFF_HEADS // hp
    o_diff = pl.pallas_call(
        functools.partial(_diff_kernel, tq=tq_d, tk=tk_d, lam_init=lam_init, heads=hp),
        grid=(B, n_hp, S // tq_d),
        in_specs=[_resident((1, d))] * 4 + [_resident((1, LANES)),
                  pl.BlockSpec((None, tq_d, hw), lambda b, h, i: (b, i, h)),
                  pl.BlockSpec((None, S, hw), lambda b, h, i: (b, 0, P_DK // hp + h)),
                  pl.BlockSpec((None, S, hw), lambda b, h, i: (b, 0, P_DV // hp + h))],
        out_specs=pl.BlockSpec((None, tq_d, hw), lambda b, h, i: (b, i, h)),
        out_shape=jax.ShapeDtypeStruct((B, S, DIFF_HEADS * LANES), dt),
        scratch_shapes=[pltpu.VMEM((hp, 2 * tq_d, LANES), F32)] * 3,
        compiler_params=_params(("parallel", "parallel", "arbitrary")),
        name="diff_attn",
    )(*lam_vec, diff_subln.reshape(1, LANES), p, p, p)

    tq_n, tk_n = NSA_TILE
    ovt = jnp.asarray(_selection_overlap_t(nc, n_cmp, n_sel)).astype(dt)
    rows = NSA_HEADS_PER_GROUP * tq_n

    def seq_slab(c):
        return pl.BlockSpec((None, S, LANES), lambda b, i: (b, 0, c))

    o_nsa = pl.pallas_call(
        functools.partial(_nsa_kernel, tq=tq_n, tk=tk_n, nc=nc, n_sel=n_sel, top_n=top_n, n_q_tiles=S // tq_n),
        grid=(B, S // tq_n),
        in_specs=[pl.BlockSpec((None, tq_n, hg * LANES), lambda b, i: (b, i, P_NQ // hg)),
                  pl.BlockSpec((None, tq_n, LANES), lambda b, i: (b, i, A_GATE)),
                  pl.BlockSpec((None, nc, LANES), lambda b, i: (b, 0, 0)),
                  pl.BlockSpec((None, nc, LANES), lambda b, i: (b, 0, 0)),
                  _resident((LANES, nc)),
                  seq_slab(P_KS), seq_slab(P_KS + 1), seq_slab(P_VS), seq_slab(P_VS + 1), seq_slab(P_KW),
                  seq_slab(P_VW), seq_slab(P_VW + 1)],
        out_specs=pl.BlockSpec((None, tq_n, 4 * LANES), lambda b, i: (b, i, 0)),
        out_shape=jax.ShapeDtypeStruct((B, S, 4 * LANES), dt),
        scratch_shapes=[pltpu.VMEM((G, rows, LANES), F32)] * 2
        + [pltpu.VMEM((G, rows, LANES), dt), pltpu.VMEM((G, rows, LANES), F32)],
        compiler_params=_params(("parallel", "arbitrary")),
        name="nsa_attn",
    )(p, aux, kcmp, vcmp, ovt, p, p, p, p, p, p, p)

    tm2 = POST_ROWS
    row = lambda i: (i, 0)
    out = pl.pallas_call(
        functools.partial(_post_kernel, d_ff=d_ff),
        grid=(N // tm2,),
        in_specs=[pl.BlockSpec((tm2, D), row),
                  pl.BlockSpec((tm2, diff_w), row), pl.BlockSpec((tm2, nsa_w), row),
                  _resident((diff_w, D)), _resident((nsa_w, D)), _resident((1, D)), _resident((1, D)),
                  _resident((D, d_ff)), _resident((D, d_ff)), _resident((d_ff, D)), _resident((1, D))],
        out_specs=pl.BlockSpec((tm2, D), row),
        out_shape=jax.ShapeDtypeStruct((N, D), F32),
        scratch_shapes=[pltpu.VMEM((tm2, d_ff), dt)],
        compiler_params=_params(("parallel",)),
        name="post",
    )(x.reshape(N, D), o_diff.reshape(N, diff_w), o_nsa.reshape(N, nsa_w), w_out_d, w_out_n,
      attn_post_norm.reshape(1, D), ffn_pre_norm.reshape(1, D),
      w_gate.astype(dt), w_up.astype(dt), w_down.astype(dt), ffn_post_norm.reshape(1, D))
    return out.reshape(B, S, D)


def kernel(x, attn_pre_norm, w_in, lambda_q1, lambda_k1, lambda_q2, lambda_k2, diff_subln, k_cmp_pos, k_cmp_w1, k_cmp_w2, v_cmp_pos, v_cmp_w1, v_cmp_w2, w_out, attn_post_norm, ffn_pre_norm, w_gate, w_up, w_down, ffn_post_norm):
    for l in range(w_in.shape[0]):
        x = _layer(x, l, attn_pre_norm[l], w_in[l], lambda_q1[l], lambda_k1[l], lambda_q2[l], lambda_k2[l],
                   diff_subln[l], k_cmp_pos[l], k_cmp_w1[l], k_cmp_w2[l], v_cmp_pos[l], v_cmp_w1[l], v_cmp_w2[l],
                   w_out[l], attn_post_norm[l], ffn_pre_norm[l], w_gate[l], w_up[l], w_down[l], ffn_post_norm[l])
    return x
```

```python
import functools
import math

import numpy as np
import jax
import jax.numpy as jnp
from jax import lax
from jax.experimental import pallas as pl
from jax.experimental.pallas import tpu as pltpu

F32 = jnp.float32
MXU_DTYPE = jnp.bfloat16

LANES = 128
SUBLANES = 8
HEAD_DIM = 64
ROPE_THETA = 10000.0
NORM_EPS = 1e-6
NEG_INF = -1e30
LOG2_E = 1.4426950408889634
FORCED_IMPORTANCE = 3e38

DIFF_HEADS = 4
NSA_HEADS_PER_GROUP = 4
NSA_KV_GROUPS = 2
CMP_BLOCK = 32
CMP_STRIDE = 16
CMP_HIDDEN = 4 * HEAD_DIM
SLC_BLOCK = 64
SLC_TOPK = 16
WINDOW = 512
N_BRANCHES = 3
SEQ_PARTS = 2

M_DQ, M_DK, M_DV, M_NQ, M_KS, M_VS, M_KW, M_VW, M_KC, M_VC, M_GATE = 0, 4, 8, 12, 16, 17, 18, 19, 20, 21, 22
N_MAIN_SLABS = 20
N_AUX_SLABS = 3
P_DQ, P_DK, P_DV, P_NQ, P_KS, P_VS, P_KW, P_VW = 0, 4, 8, 12, 16, 18, 20, 21
N_P_SLABS = 23
A_KC, A_VC, A_GATE = 0, 1, 2
VMEM_LIMIT = 56 * 1024 * 1024

MXU_WIDTH = 256
INPROJ_ROWS = 1024
POST_ROWS = 512
FF_CHUNK = MXU_WIDTH
DIFF_TILE = (512, 1024)
DIFF_HEADS_PER_STEP = 2
NSA_TILE = (256, 512)


def _nt_dot(a, b):
    return lax.dot_general(a, b, (((1,), (1,)), ((), ())), preferred_element_type=F32)


def _dot(a, b):
    return jnp.dot(a, b, preferred_element_type=F32)


def _rms(x, g):
    return x * lax.rsqrt(jnp.mean(x * x, axis=-1, keepdims=True) + NORM_EPS) * g


def _inproj_kernel(x_ref, g_ref, w_ref, cos_ref, sin_ref, p_ref, aux_ref, *, tm, seq_tiles):
    h = _rms(x_ref[...], g_ref[...]).astype(MXU_DTYPE)
    cos = cos_ref[...]
    sin = sin_ref[...]
    lane = lax.broadcasted_iota(jnp.int32, (tm, LANES), 1)
    low_half = (lane & (HEAD_DIM - 1)) < HEAD_DIM // 2

    def rope(y):
        fwd = pltpu.roll(y, HEAD_DIM // 2, 1)
        bwd = pltpu.roll(y, LANES - HEAD_DIM // 2, 1)
        return y * cos + jnp.where(low_half, bwd, fwd) * sin

    pos = (pl.program_id(0) % seq_tiles) * tm + lax.broadcasted_iota(jnp.int32, (tm, LANES), 0)
    blk = pos // SLC_BLOCK
    scale = HEAD_DIM ** -0.5 * LOG2_E

    n_slabs = N_MAIN_SLABS + N_AUX_SLABS
    per_dot = MXU_WIDTH // LANES
    for c0 in range(0, n_slabs, per_dot):
        c1 = min(c0 + per_dot, n_slabs)
        y2 = _dot(h, w_ref[:, c0 * LANES:c1 * LANES])
        for s in range(c0, c1):
            y = y2[:, (s - c0) * LANES:(s - c0 + 1) * LANES]

            def put(ref, slab, val):
                ref[:, slab * LANES:(slab + 1) * LANES] = val.astype(ref.dtype)

            if M_DQ <= s < M_DK or M_NQ <= s < M_KS:
                put(p_ref, s, rope(y) * scale)
            elif M_DK <= s < M_DV:
                put(p_ref, s, rope(y))
            elif M_DV <= s < M_NQ:
                put(p_ref, s, y)
            elif s == M_KS:
                r = rope(y)
                put(p_ref, P_KS, jnp.where(lane < HEAD_DIM, r, jnp.where(lane - HEAD_DIM == blk, 1.0, 0.0)))
                put(p_ref, P_KS + 1, jnp.where(lane >= HEAD_DIM, r, jnp.where(lane == blk, 1.0, 0.0)))
            elif s in (M_VS, M_VW):
                o = P_VS if s == M_VS else P_VW
                put(p_ref, o, jnp.where(lane < HEAD_DIM, y, 1.0))
                put(p_ref, o + 1, jnp.where(lane >= HEAD_DIM, y, 1.0))
            elif s == M_KW:
                put(p_ref, P_KW, rope(y))
            elif s == M_KC:
                put(aux_ref, A_KC, rope(y))
            elif s == M_VC:
                put(aux_ref, A_VC, y)
            else:
                put(aux_ref, A_GATE, jax.nn.sigmoid(y))


def _compress_kernel(tk_ref, tv_ref, kpl_ref, kph_ref, vpl_ref, vph_ref,
                     kwl_ref, kwh_ref, kw2_ref, vwl_ref, vwh_ref, vw2_ref, ko_ref, vo_ref, *, nc):
    lane = lax.broadcasted_iota(jnp.int32, (nc, CMP_STRIDE * LANES), 1)
    group_lanes = [(lane & HEAD_DIM) == 0, (lane & HEAD_DIM) != 0]

    def compress(t_ref, plo_ref, phi_ref, wlo_ref, whi_ref, w2_ref):
        x = jnp.concatenate([t_ref[pl.ds(l, nc, stride=CMP_STRIDE), :] for l in range(CMP_STRIDE)], axis=1)
        x_lo = x + plo_ref[...]
        x_hi = x + phi_ref[...]
        out = None
        for g in range(NSA_KV_GROUPS):
            a = _dot(jnp.where(group_lanes[g], x_lo, 0.0).astype(MXU_DTYPE), wlo_ref[...])
            b = _dot(jnp.where(group_lanes[g], x_hi, 0.0).astype(MXU_DTYPE), whi_ref[...])
            hid = a + pltpu.roll(b, nc - 1, 0)
            act = hid * jax.nn.sigmoid(hid)
            o = _dot(act.astype(MXU_DTYPE), w2_ref[g])
            out = o if out is None else out + o
        return out

    ko_ref[...] = compress(tk_ref, kpl_ref, kph_ref, kwl_ref, kwh_ref, kw2_ref).astype(ko_ref.dtype)
    vo_ref[...] = compress(tv_ref, vpl_ref, vph_ref, vwl_ref, vwh_ref, vw2_ref).astype(vo_ref.dtype)


def _flash_init(m_ref, l_ref, acc_ref):
    m_ref[...] = jnp.full(m_ref.shape, NEG_INF, F32)
    if l_ref is not None:
        l_ref[...] = jnp.zeros(l_ref.shape, F32)
    acc_ref[...] = jnp.zeros(acc_ref.shape, F32)


def _lane_tiles(x):
    return [x[:, c:c + LANES] for c in range(0, x.shape[1], LANES)]


def _stack_rows(x, n):
    return jnp.concatenate([x] * n, axis=0)


def _softmax_pv(s, v, m_ref, l_ref, acc_ref, bias=None):
    if bias is not None:
        s = s + bias
    n_tiles = s.shape[1] // LANES
    m_prev = m_ref[...]
    m_new = jnp.maximum(m_prev, jnp.max(s, axis=-1, keepdims=True))
    alpha = jnp.exp2(m_prev - m_new)
    x = s - jnp.concatenate([m_new] * n_tiles, axis=1)
    if l_ref is None:
        p = jnp.exp2(x.astype(MXU_DTYPE))
    else:
        p = jnp.exp2(x)
        l_ref[...] = alpha * l_ref[...] + functools.reduce(lambda a, b: a + b, _lane_tiles(p))
        p = p.astype(MXU_DTYPE)
    acc_ref[...] = alpha * acc_ref[...] + _dot(p, v)
    m_ref[...] = m_new


def _flash_finish(l_ref, acc_ref):
    if l_ref is None:
        return acc_ref[...]
    return acc_ref[...] / jnp.sum(l_ref[...], axis=-1, keepdims=True)


def _causal_flash(streams, s0, tq, tk):
    assert tk in (tq, 2 * tq)
    rows = streams[0][0].shape[0]
    for _, _, _, m_ref, l_ref, acc_ref in streams:
        _flash_init(m_ref, l_ref, acc_ref)

    def step(stream, k0, width, bias=None):
        q, k_ref, v_ref, m_ref, l_ref, acc_ref = stream
        k0 = pl.multiple_of(k0, width)
        _softmax_pv(_nt_dot(q, k_ref[pl.ds(k0, width), :]), v_ref[pl.ds(k0, width), :],
                    m_ref, l_ref, acc_ref, bias)

    def tail_bias(width):
        qpos = (width - tq) + lax.broadcasted_iota(jnp.int32, (tq, width), 0)
        causal = lax.broadcasted_iota(jnp.int32, (tq, width), 1) <= qpos
        return _stack_rows(jnp.where(causal, 0.0, NEG_INF), rows // tq)

    n_full = s0 // tk
    aligned = (s0 - n_full * tk) == 0
    widths = (tq,) if tk == tq else (tq, tk)

    def tail(width, odd_tile=False):
        def body():
            bias = tail_bias(width)
            for stream in streams:
                if odd_tile:
                    step(stream, (n_full - 1) * tk, tk)
                step(stream, s0 + tq - width, width, bias)
        return body

    def when_width(width, extra=None):
        cond = aligned if width == tq else jnp.logical_not(aligned)
        if tk == tq:
            cond = extra
        elif extra is not None:
            cond = cond & extra
        return pl.when(cond) if cond is not None else (lambda f: f())

    def pair(jj, carry):
        for t in range(2):
            for stream in streams:
                step(stream, (2 * jj + t) * tk, tk)
        return carry

    lax.fori_loop(0, n_full // 2, pair, 0)
    for odd_tile in (False, True):
        parity = (n_full % 2 == 1) if odd_tile else (n_full % 2 == 0)
        for width in widths:
            when_width(width, parity)(tail(width, odd_tile))

    return [_flash_finish(l_ref, acc_ref) for _, _, _, _, l_ref, acc_ref in streams]


def _diff_kernel(lq1_ref, lk1_ref, lq2_ref, lk2_ref, subln_ref, q_ref, k_ref, v_ref, o_ref,
                 m_ref, l_ref, acc_ref, *, tq, tk, lam_init, heads):
    qi = pl.program_id(2)
    s0 = qi * tq
    lane = lax.broadcasted_iota(jnp.int32, (tq, LANES), 1)
    zero = jnp.zeros((tq, LANES), F32)
    streams = []
    for h in range(heads):
        slab = pl.ds(h * LANES, LANES)
        qf = q_ref[:, h * LANES:(h + 1) * LANES].astype(F32)
        q2 = jnp.concatenate([jnp.where(lane < HEAD_DIM, qf, zero),
                              jnp.where(lane >= HEAD_DIM, qf, zero)], axis=0).astype(MXU_DTYPE)
        streams.append((q2, k_ref.at[:, slab], v_ref.at[:, slab], m_ref.at[h], l_ref.at[h], acc_ref.at[h]))

    lam = (jnp.exp(jnp.sum(lq1_ref[...] * lk1_ref[...], axis=-1, keepdims=True))
           - jnp.exp(jnp.sum(lq2_ref[...] * lk2_ref[...], axis=-1, keepdims=True)) + lam_init)
    for h, o12 in enumerate(_causal_flash(streams, s0, tq, tk)):
        o = o12[:tq] - lam * o12[tq:]
        o_ref[:, h * LANES:(h + 1) * LANES] = (_rms(o, subln_ref[...]) * (1.0 - lam_init)).astype(o_ref.dtype)


def _nsa_kernel(q_ref, gate_ref, kc_ref, vc_ref, ovt_ref, ka0_ref, ka1_ref, vs0_ref, vs1_ref,
                kw_ref, vw0_ref, vw1_ref, o_ref, m_ref, acc_ref, qsel_ref, part_ref,
                *, tq, tk, nc, n_sel, top_n, n_q_tiles):
    hg = NSA_HEADS_PER_GROUP
    rows = hg * tq
    qi = pl.program_id(1)
    s0 = qi * tq
    lane = lax.broadcasted_iota(jnp.int32, (tq, LANES), 1)
    lane_rows = lax.broadcasted_iota(jnp.int32, (rows, LANES), 1)
    qf = [q_ref[:, h * LANES:(h + 1) * LANES].astype(F32) for h in range(hg)]
    gates = gate_ref[...]
    zero = jnp.zeros((tq, LANES), F32)

    wk = WINDOW + tq
    w0 = pl.multiple_of(jnp.maximum(s0 - WINDOW, 0), tq)
    back = (s0 - w0) + lax.broadcasted_iota(jnp.int32, (tq, wk), 0) \
        - lax.broadcasted_iota(jnp.int32, (tq, wk), 1)
    win_bias = _stack_rows(jnp.where((back >= 0) & (back < WINDOW), 0.0, NEG_INF), hg)

    def normalise(raw, g):
        in_half_rows = (lane_rows >= HEAD_DIM) if g else (lane_rows < HEAD_DIM)
        return raw / jnp.where(in_half_rows, pltpu.roll(raw, HEAD_DIM, 1), 1.0)

    def gate(g, h, branch):
        c = (g * hg + h) * N_BRANCHES + branch
        return gates[:, c:c + 1]

    def phase1(n_blk, nc_eff):
        n_idx = lax.broadcasted_iota(jnp.int32, (tq, nc_eff), 1)
        qpos_c = s0 + lax.broadcasted_iota(jnp.int32, (tq, nc_eff), 0)
        cmp_ok = n_idx * CMP_STRIDE + (CMP_BLOCK - 1) <= qpos_c
        cmp_bias = _stack_rows(jnp.where(cmp_ok, 0.0, NEG_INF), hg)
        cmp_keep = _stack_rows(jnp.where(cmp_ok, 1.0, 0.0), hg)
        jb = lax.broadcasted_iota(jnp.int32, (n_blk, tq), 0)
        qp = s0 + lax.broadcasted_iota(jnp.int32, (n_blk, tq), 1)
        cur = qp // SLC_BLOCK
        valid = jb * SLC_BLOCK <= qp
        forced = (jb == 0) | (jb == cur) | (jb == cur - 1)

        for g in range(NSA_KV_GROUPS):
            in_half = (lane >= HEAD_DIM) if g else (lane < HEAD_DIM)
            q_plain = jnp.concatenate([jnp.where(in_half, qf[h], zero) for h in range(hg)],
                                      axis=0).astype(MXU_DTYPE)

            sc = _nt_dot(q_plain, kc_ref[:nc_eff, :]) + cmp_bias
            pc = jnp.exp2(sc - jnp.max(sc, axis=-1, keepdims=True)) * cmp_keep
            lc = jnp.sum(pc, axis=-1, keepdims=True)
            pc = (pc / jnp.where(lc > 0.0, lc, 1.0)).astype(MXU_DTYPE)
            o_cmp = _dot(pc, vc_ref[:nc_eff, :])

            if n_blk <= top_n:
                selected = valid
            else:
                imp_t = None
                for h in range(hg):
                    part = _nt_dot(ovt_ref[:, :nc_eff], pc[h * tq:(h + 1) * tq])
                    imp_t = part if imp_t is None else imp_t + part
                key = jnp.where(forced, FORCED_IMPORTANCE, jnp.where(valid, imp_t[:n_blk], -1.0))
                n_grp = n_blk // SUBLANES
                key_g = [key[a * SUBLANES:(a + 1) * SUBLANES] for a in range(n_grp)]
                jb_g = lax.broadcasted_iota(jnp.int32, (SUBLANES, tq), 0)
                rank_g = [jnp.zeros((SUBLANES, tq), jnp.int32) for _ in range(n_grp)]
                for i in range(n_blk):
                    ri = key[i:i + 1, :]
                    for a in range(n_grp):
                        if a < i // SUBLANES:
                            ahead = jnp.where(ri > key_g[a], 1, 0)
                        elif a > i // SUBLANES:
                            ahead = jnp.where(ri >= key_g[a], 1, 0)
                        else:
                            ahead = (jnp.where(ri > key_g[a], 1, 0)
                                     + jnp.where((ri == key_g[a]) & (jb_g > i % SUBLANES), 1, 0))
                        rank_g[a] = rank_g[a] + ahead
                selected = (jnp.concatenate(rank_g, axis=0) < top_n) & valid
            bias_t = jnp.where(selected, 0.0, NEG_INF)
            if n_blk < SLC_BLOCK:
                bias_t = jnp.concatenate([bias_t, jnp.full((SLC_BLOCK - n_blk, tq), NEG_INF, F32)], axis=0)
            pad_t = jnp.zeros((SLC_BLOCK, tq), F32)
            bias = jnp.transpose(jnp.concatenate([pad_t, bias_t] if g == 0 else [bias_t, pad_t], axis=0))
            qsel_ref[g] = jnp.concatenate([jnp.where(in_half, qf[h], bias) for h in range(hg)],
                                          axis=0).astype(qsel_ref.dtype)

            sw = _nt_dot(q_plain, kw_ref[pl.ds(w0, wk), :]) + win_bias
            pw = jnp.exp2((sw - jnp.max(sw, axis=-1, keepdims=True)).astype(MXU_DTYPE))
            o_win = normalise(_dot(pw, (vw1_ref if g else vw0_ref)[pl.ds(w0, wk), :]), g)
            part_ref[g] = jnp.concatenate(
                [gate(g, h, 0) * o_cmp[h * tq:(h + 1) * tq] + gate(g, h, 2) * o_win[h * tq:(h + 1) * tq]
                 for h in range(hg)], axis=0)

    tiles_per_part = n_q_tiles // SEQ_PARTS
    for c in range(SEQ_PARTS):
        nc_part = min(nc, -(-(nc * (c + 1) // SEQ_PARTS) // LANES) * LANES)
        pl.when(qi // tiles_per_part == c)(
            functools.partial(phase1, n_sel * (c + 1) // SEQ_PARTS, nc_part))

    outs = []
    raw = _causal_flash([(qsel_ref[0], ka0_ref, vs0_ref, m_ref.at[0], None, acc_ref.at[0]),
                         (qsel_ref[1], ka1_ref, vs1_ref, m_ref.at[1], None, acc_ref.at[1])], s0, tq, tk)
    for g in range(NSA_KV_GROUPS):
        o_slc = normalise(raw[g], g)
        outs.append([part_ref[g, h * tq:(h + 1) * tq, :] + gate(g, h, 1) * o_slc[h * tq:(h + 1) * tq]
                     for h in range(hg)])

    for h in range(hg):
        o_ref[:, h * LANES:(h + 1) * LANES] = jnp.where(lane < HEAD_DIM, outs[0][h], outs[1][h]).astype(o_ref.dtype)


def _post_kernel(x_ref, od_ref, on_ref, wod_ref, won_ref, gpost_ref, gpre_ref, wg_ref, wu_ref, wd_ref,
                 gffn_ref, o_ref, act_ref, *, d_ff):
    mix = _dot(od_ref[...], wod_ref[...]) + _dot(on_ref[...], won_ref[...])
    x1 = x_ref[...] + _rms(mix, gpost_ref[...])
    h = _rms(x1, gpre_ref[...]).astype(MXU_DTYPE)
    for c in range(0, d_ff, FF_CHUNK):
        gate = _dot(h, wg_ref[:, c:c + FF_CHUNK])
        up = _dot(h, wu_ref[:, c:c + FF_CHUNK])
        act_ref[:, c:c + FF_CHUNK] = (gate * jax.nn.sigmoid(gate) * up).astype(act_ref.dtype)
    f = _dot(act_ref[...], wd_ref[...])
    o_ref[...] = x1 + _rms(f, gffn_ref[...])


def _resident(shape):
    nd = len(shape)
    return pl.BlockSpec(shape, lambda *_: (0,) * nd, pipeline_mode=pl.Buffered(1))


def _params(sem):
    return pltpu.CompilerParams(dimension_semantics=sem, vmem_limit_bytes=VMEM_LIMIT)


def _rope_tables(S):
    inv = 1.0 / (ROPE_THETA ** (jnp.arange(0, HEAD_DIM, 2, dtype=F32) / HEAD_DIM))
    ang = jnp.arange(S, dtype=F32)[:, None] * inv[None, :]
    cos, sin = jnp.cos(ang), jnp.sin(ang)
    return jnp.tile(cos, (1, 4)), jnp.concatenate([-sin, sin, -sin, sin], axis=1)


def _selection_overlap_t(nc, n_cmp, n_sel):
    c0 = np.arange(n_cmp)[:, None] * CMP_STRIDE
    b0 = np.arange(n_sel)[None, :] * SLC_BLOCK
    ov = np.clip(np.minimum(c0 + CMP_BLOCK, b0 + SLC_BLOCK) - np.maximum(c0, b0), 0, None) / CMP_BLOCK
    full = np.zeros((LANES, nc), np.float32)
    full[:n_sel, :n_cmp] = ov.T
    return full


def _layer(x, layer, attn_pre_norm, w_in, lq1, lk1, lq2, lk2, diff_subln, k_pos, k_w1, k_w2,
           v_pos, v_w1, v_w2, w_out, attn_post_norm, ffn_pre_norm, w_gate, w_up, w_down, ffn_post_norm):
    B, S, D = x.shape
    N = B * S
    d_ff = w_gate.shape[1]
    nc = S // CMP_STRIDE
    n_cmp = (S - CMP_BLOCK) // CMP_STRIDE + 1
    n_sel = S // SLC_BLOCK
    top_n = min(SLC_TOPK, n_sel)
    assert n_sel <= SLC_BLOCK
    assert S % max(INPROJ_ROWS, DIFF_TILE[1], NSA_TILE[1]) == 0 and S >= WINDOW + NSA_TILE[0]
    assert (S // NSA_TILE[0]) % SEQ_PARTS == 0 and (n_sel // SEQ_PARTS) % SUBLANES == 0
    lam_init = 0.8 - 0.6 * math.exp(-0.3 * layer)
    dt = MXU_DTYPE

    hg, G, d = NSA_HEADS_PER_GROUP, NSA_KV_GROUPS, HEAD_DIM
    diff_w = DIFF_HEADS * 2 * d
    nsa_w = G * hg * d
    kv_w = G * d
    off_nq = 3 * diff_w
    off_kc = off_nq + nsa_w
    off_ks = off_kc + 2 * kv_w
    off_gate = off_kc + 6 * kv_w
    n_gates = G * hg * N_BRANCHES
    head_starts = [(g * hg + h) * d for h in range(hg) for g in range(G)]
    w_cat = jnp.concatenate(
        [w_in[:, :off_nq]] + [w_in[:, off_nq + c:off_nq + c + d] for c in head_starts]
        + [w_in[:, off_ks:off_gate],
           w_in[:, off_kc:off_ks],
           w_in[:, off_gate:off_gate + n_gates],
           jnp.zeros((D, LANES - n_gates), w_in.dtype)], axis=1).astype(dt)
    w_out_d = w_out[:diff_w].astype(dt)
    w_out_n = jnp.concatenate([w_out[diff_w + c:diff_w + c + d] for c in head_starts], axis=0).astype(dt)
    cos_t, sin_t = _rope_tables(S)

    tm = INPROJ_ROWS
    seq_tiles = S // tm
    n_cols = (N_MAIN_SLABS + N_AUX_SLABS) * LANES
    p, aux = pl.pallas_call(
        functools.partial(_inproj_kernel, tm=tm, seq_tiles=seq_tiles),
        grid=(N // tm,),
        in_specs=[pl.BlockSpec((tm, D), lambda i: (i, 0)),
                  _resident((1, D)),
                  _resident((D, n_cols)),
                  pl.BlockSpec((tm, LANES), lambda i: (i % seq_tiles, 0)),
                  pl.BlockSpec((tm, LANES), lambda i: (i % seq_tiles, 0))],
        out_specs=[pl.BlockSpec((tm, N_P_SLABS * LANES), lambda i: (i, 0)),
                   pl.BlockSpec((tm, N_AUX_SLABS * LANES), lambda i: (i, 0))],
        out_shape=[jax.ShapeDtypeStruct((N, N_P_SLABS * LANES), dt),
                   jax.ShapeDtypeStruct((N, N_AUX_SLABS * LANES), F32)],
        compiler_params=_params(("parallel",)),
        name="inproj",
    )(x.reshape(N, D), attn_pre_norm.reshape(1, D), w_cat, cos_t, sin_t)
    p = p.reshape(B, S, N_P_SLABS * LANES)
    aux = aux.reshape(B, S, N_AUX_SLABS * LANES)

    def pos_rows(pos):
        tiled = jnp.broadcast_to(pos.reshape(2, CMP_STRIDE, 1, d), (2, CMP_STRIDE, G, d))
        return tiled[0].reshape(1, -1), tiled[1].reshape(1, -1)

    def w1_rows(w1):
        w = jnp.broadcast_to(w1.astype(dt).reshape(2, CMP_STRIDE, 1, d, CMP_HIDDEN),
                             (2, CMP_STRIDE, G, d, CMP_HIDDEN)).reshape(2, CMP_STRIDE * G * d, CMP_HIDDEN)
        return w[0], w[1]

    def w2_halves(w2):
        return jnp.stack([jnp.pad(w2, ((0, 0), (g * d, (G - 1 - g) * d))) for g in range(G)]).astype(dt)

    kpl, kph = pos_rows(k_pos)
    vpl, vph = pos_rows(v_pos)
    kwl, kwh = w1_rows(k_w1)
    vwl, vwh = w1_rows(v_w1)
    chunk_w = CMP_STRIDE * G * d
    w1_spec = _resident((chunk_w, CMP_HIDDEN))
    w2_spec = _resident((G, CMP_HIDDEN, LANES))
    kcmp, vcmp = pl.pallas_call(
        functools.partial(_compress_kernel, nc=nc),
        grid=(B,),
        in_specs=[pl.BlockSpec((None, S, LANES), lambda b: (b, 0, A_KC)),
                  pl.BlockSpec((None, S, LANES), lambda b: (b, 0, A_VC)),
                  _resident((1, chunk_w)), _resident((1, chunk_w)), _resident((1, chunk_w)), _resident((1, chunk_w)),
                  w1_spec, w1_spec, w2_spec, w1_spec, w1_spec, w2_spec],
        out_specs=[pl.BlockSpec((None, nc, LANES), lambda b: (b, 0, 0)),
                   pl.BlockSpec((None, nc, LANES), lambda b: (b, 0, 0))],
        out_shape=[jax.ShapeDtypeStruct((B, nc, LANES), dt), jax.ShapeDtypeStruct((B, nc, LANES), dt)],
        compiler_params=_params(("parallel",)),
        name="compress",
    )(aux, aux, kpl, kph, vpl, vph, kwl, kwh, w2_halves(k_w2), vwl, vwh, w2_halves(v_w2))

    tq_d, tk_d = DIFF_TILE
    lam_vec = [v.reshape(1, d) for v in (lq1, lk1, lq2, lk2)]
    hp = DIFF_HEADS_PER_STEP
    hw = hp * LANES
    n_hp = DIFF_HEADS // hp
    o_diff = pl.pallas_call(
        functools.partial(_diff_kernel, tq=tq_d, tk=tk_d, lam_init=lam_init, heads=hp),
        grid=(B, n_hp, S // tq_d),
        in_specs=[_resident((1, d))] * 4 + [_resident((1, LANES)),
                  pl.BlockSpec((None, tq_d, hw), lambda b, h, i: (b, i, h)),
                  pl.BlockSpec((None, S, hw), lambda b, h, i: (b, 0, P_DK // hp + h)),
                  pl.BlockSpec((None, S, hw), lambda b, h, i: (b, 0, P_DV // hp + h))],
        out_specs=pl.BlockSpec((None, tq_d, hw), lambda b, h, i: (b, i, h)),
        out_shape=jax.ShapeDtypeStruct((B, S, DIFF_HEADS * LANES), dt),
        scratch_shapes=[pltpu.VMEM((hp, 2 * tq_d, LANES), F32)] * 3,
        compiler_params=_params(("parallel", "parallel", "arbitrary")),
        name="diff_attn",
    )(*lam_vec, diff_subln.reshape(1, LANES), p, p, p)

    tq_n, tk_n = NSA_TILE
    ovt = jnp.asarray(_selection_overlap_t(nc, n_cmp, n_sel)).astype(dt)
    rows = NSA_HEADS_PER_GROUP * tq_n

    def seq_slab(c):
        return pl.BlockSpec((None, S, LANES), lambda b, i: (b, 0, c))

    o_nsa = pl.pallas_call(
        functools.partial(_nsa_kernel, tq=tq_n, tk=tk_n, nc=nc, n_sel=n_sel, top_n=top_n, n_q_tiles=S // tq_n),
        grid=(B, S // tq_n),
        in_specs=[pl.BlockSpec((None, tq_n, hg * LANES), lambda b, i: (b, i, P_NQ // hg)),
                  pl.BlockSpec((None, tq_n, LANES), lambda b, i: (b, i, A_GATE)),
                  pl.BlockSpec((None, nc, LANES), lambda b, i: (b, 0, 0)),
                  pl.BlockSpec((None, nc, LANES), lambda b, i: (b, 0, 0)),
                  _resident((LANES, nc)),
                  seq_slab(P_KS), seq_slab(P_KS + 1), seq_slab(P_VS), seq_slab(P_VS + 1), seq_slab(P_KW),
                  seq_slab(P_VW), seq_slab(P_VW + 1)],
        out_specs=pl.BlockSpec((None, tq_n, 4 * LANES), lambda b, i: (b, i, 0)),
        out_shape=jax.ShapeDtypeStruct((B, S, 4 * LANES), dt),
        scratch_shapes=[pltpu.VMEM((G, rows, LANES), F32)] * 2
        + [pltpu.VMEM((G, rows, LANES), dt), pltpu.VMEM((G, rows, LANES), F32)],
        compiler_params=_params(("parallel", "arbitrary")),
        name="nsa_attn",
    )(p, aux, kcmp, vcmp, ovt, p, p, p, p, p, p, p)

    tm2 = POST_ROWS
    row = lambda i: (i, 0)
    out = pl.pallas_call(
        functools.partial(_post_kernel, d_ff=d_ff),
        grid=(N // tm2,),
        in_specs=[pl.BlockSpec((tm2, D), row),
                  pl.BlockSpec((tm2, diff_w), row), pl.BlockSpec((tm2, nsa_w), row),
                  _resident((diff_w, D)), _resident((nsa_w, D)), _resident((1, D)), _resident((1, D)),
                  _resident((D, d_ff)), _resident((D, d_ff)), _resident((d_ff, D)), _resident((1, D))],
        out_specs=pl.BlockSpec((tm2, D), row),
        out_shape=jax.ShapeDtypeStruct((N, D), F32),
        scratch_shapes=[pltpu.VMEM((tm2, d_ff), dt)],
        compiler_params=_params(("parallel",)),
        name="post",
    )(x.reshape(N, D), o_diff.reshape(N, diff_w), o_nsa.reshape(N, nsa_w), w_out_d, w_out_n,
      attn_post_norm.reshape(1, D), ffn_pre_norm.reshape(1, D),
      w_gate.astype(dt), w_up.astype(dt), w_down.astype(dt), ffn_post_norm.reshape(1, D))
    return out.reshape(B, S, D)


def kernel(x, attn_pre_norm, w_in, lambda_q1, lambda_k1, lambda_q2, lambda_k2, diff_subln, k_cmp_pos, k_cmp_w1, k_cmp_w2, v_cmp_pos, v_cmp_w1, v_cmp_w2, w_out, attn_post_norm, ffn_pre_norm, w_gate, w_up, w_down, ffn_post_norm):
    for l in range(w_in.shape[0]):
        x = _layer(x, l, attn_pre_norm[l], w_in[l], lambda_q1[l], lambda_k1[l], lambda_q2[l], lambda_k2[l],
                   diff_subln[l], k_cmp_pos[l], k_cmp_w1[l], k_cmp_w2[l], v_cmp_pos[l], v_cmp_w1[l], v_cmp_w2[l],
                   w_out[l], attn_post_norm[l], ffn_pre_norm[l], w_gate[l], w_up[l], w_down[l], ffn_post_norm[l])
    return x
```

```python
import functools
import math

import numpy as np
import jax
import jax.numpy as jnp
from jax import lax
from jax.experimental import pallas as pl
from jax.experimental.pallas import tpu as pltpu

F32 = jnp.float32
MXU_DTYPE = jnp.bfloat16

LANES = 128
SUBLANES = 8
HEAD_DIM = 64
ROPE_THETA = 10000.0
NORM_EPS = 1e-6
NEG_INF = -1e30
LOG2_E = 1.4426950408889634
FORCED_IMPORTANCE = 3e38

DIFF_HEADS = 4
NSA_HEADS_PER_GROUP = 4
NSA_KV_GROUPS = 2
CMP_BLOCK = 32
CMP_STRIDE = 16
CMP_HIDDEN = 4 * HEAD_DIM
SLC_BLOCK = 64
SLC_TOPK = 16
WINDOW = 512
N_BRANCHES = 3
SEQ_PARTS = 4

M_DQ, M_DK, M_DV, M_NQ, M_KS, M_VS, M_KW, M_VW, M_KC, M_VC, M_GATE = 0, 4, 8, 12, 16, 17, 18, 19, 20, 21, 22
N_MAIN_SLABS = 20
N_AUX_SLABS = 3
P_DQ, P_DK, P_DV, P_NQ, P_KS, P_VS, P_KW, P_VW = 0, 4, 8, 12, 16, 18, 20, 21
N_P_SLABS = 23
A_KC, A_VC, A_GATE = 0, 1, 2
VMEM_LIMIT = 56 * 1024 * 1024

MXU_WIDTH = 256
INPROJ_ROWS = 1024
POST_ROWS = 512
FF_CHUNK = MXU_WIDTH
DIFF_TILE = (512, 1024)
DIFF_HEADS_PER_STEP = 2
NSA_TILE = (256, 512)


def _nt_dot(a, b):
    return lax.dot_general(a, b, (((1,), (1,)), ((), ())), preferred_element_type=F32)


def _dot(a, b):
    return jnp.dot(a, b, preferred_element_type=F32)


def _rms(x, g):
    return x * lax.rsqrt(jnp.mean(x * x, axis=-1, keepdims=True) + NORM_EPS) * g


def _inproj_kernel(x_ref, g_ref, w_ref, cos_ref, sin_ref, p_ref, aux_ref, *, tm, seq_tiles):
    h = _rms(x_ref[...], g_ref[...]).astype(MXU_DTYPE)
    cos = cos_ref[...]
    sin = sin_ref[...]
    lane = lax.broadcasted_iota(jnp.int32, (tm, LANES), 1)
    low_half = (lane & (HEAD_DIM - 1)) < HEAD_DIM // 2

    def rope(y):
        fwd = pltpu.roll(y, HEAD_DIM // 2, 1)
        bwd = pltpu.roll(y, LANES - HEAD_DIM // 2, 1)
        return y * cos + jnp.where(low_half, bwd, fwd) * sin

    pos = (pl.program_id(0) % seq_tiles) * tm + lax.broadcasted_iota(jnp.int32, (tm, LANES), 0)
    blk = pos // SLC_BLOCK
    scale = HEAD_DIM ** -0.5 * LOG2_E

    n_slabs = N_MAIN_SLABS + N_AUX_SLABS
    per_dot = MXU_WIDTH // LANES
    for c0 in range(0, n_slabs, per_dot):
        c1 = min(c0 + per_dot, n_slabs)
        y2 = _dot(h, w_ref[:, c0 * LANES:c1 * LANES])
        for s in range(c0, c1):
            y = y2[:, (s - c0) * LANES:(s - c0 + 1) * LANES]

            def put(ref, slab, val):
                ref[:, slab * LANES:(slab + 1) * LANES] = val.astype(ref.dtype)

            if M_DQ <= s < M_DK or M_NQ <= s < M_KS:
                put(p_ref, s, rope(y) * scale)
            elif M_DK <= s < M_DV:
                put(p_ref, s, rope(y))
            elif M_DV <= s < M_NQ:
                put(p_ref, s, y)
            elif s == M_KS:
                r = rope(y)
                put(p_ref, P_KS, jnp.where(lane < HEAD_DIM, r, jnp.where(lane - HEAD_DIM == blk, 1.0, 0.0)))
                put(p_ref, P_KS + 1, jnp.where(lane >= HEAD_DIM, r, jnp.where(lane == blk, 1.0, 0.0)))
            elif s in (M_VS, M_VW):
                o = P_VS if s == M_VS else P_VW
                put(p_ref, o, jnp.where(lane < HEAD_DIM, y, 1.0))
                put(p_ref, o + 1, jnp.where(lane >= HEAD_DIM, y, 1.0))
            elif s == M_KW:
                put(p_ref, P_KW, rope(y))
            elif s == M_KC:
                put(aux_ref, A_KC, rope(y))
            elif s == M_VC:
                put(aux_ref, A_VC, y)
            else:
                put(aux_ref, A_GATE, jax.nn.sigmoid(y))


def _compress_kernel(tk_ref, tv_ref, kpl_ref, kph_ref, vpl_ref, vph_ref,
                     kwl_ref, kwh_ref, kw2_ref, vwl_ref, vwh_ref, vw2_ref, ko_ref, vo_ref, *, nc):
    lane = lax.broadcasted_iota(jnp.int32, (nc, CMP_STRIDE * LANES), 1)
    group_lanes = [(lane & HEAD_DIM) == 0, (lane & HEAD_DIM) != 0]

    def compress(t_ref, plo_ref, phi_ref, wlo_ref, whi_ref, w2_ref):
        x = jnp.concatenate([t_ref[pl.ds(l, nc, stride=CMP_STRIDE), :] for l in range(CMP_STRIDE)], axis=1)
        x_lo = x + plo_ref[...]
        x_hi = x + phi_ref[...]
        out = None
        for g in range(NSA_KV_GROUPS):
            a = _dot(jnp.where(group_lanes[g], x_lo, 0.0).astype(MXU_DTYPE), wlo_ref[...])
            b = _dot(jnp.where(group_lanes[g], x_hi, 0.0).astype(MXU_DTYPE), whi_ref[...])
            hid = a + pltpu.roll(b, nc - 1, 0)
            act = hid * jax.nn.sigmoid(hid)
            o = _dot(act.astype(MXU_DTYPE), w2_ref[g])
            out = o if out is None else out + o
        return out

    ko_ref[...] = compress(tk_ref, kpl_ref, kph_ref, kwl_ref, kwh_ref, kw2_ref).astype(ko_ref.dtype)
    vo_ref[...] = compress(tv_ref, vpl_ref, vph_ref, vwl_ref, vwh_ref, vw2_ref).astype(vo_ref.dtype)


def _flash_init(m_ref, l_ref, acc_ref):
    m_ref[...] = jnp.full(m_ref.shape, NEG_INF, F32)
    if l_ref is not None:
        l_ref[...] = jnp.zeros(l_ref.shape, F32)
    acc_ref[...] = jnp.zeros(acc_ref.shape, F32)


def _lane_tiles(x):
    return [x[:, c:c + LANES] for c in range(0, x.shape[1], LANES)]


def _stack_rows(x, n):
    return jnp.concatenate([x] * n, axis=0)


def _softmax_pv(s, v, m_ref, l_ref, acc_ref, bias=None):
    if bias is not None:
        s = s + bias
    n_tiles = s.shape[1] // LANES
    m_prev = m_ref[...]
    m_new = jnp.maximum(m_prev, jnp.max(s, axis=-1, keepdims=True))
    alpha = jnp.exp2(m_prev - m_new)
    x = s - jnp.concatenate([m_new] * n_tiles, axis=1)
    if l_ref is None:
        p = jnp.exp2(x.astype(MXU_DTYPE))
    else:
        p = jnp.exp2(x)
        l_ref[...] = alpha * l_ref[...] + functools.reduce(lambda a, b: a + b, _lane_tiles(p))
        p = p.astype(MXU_DTYPE)
    acc_ref[...] = alpha * acc_ref[...] + _dot(p, v)
    m_ref[...] = m_new


def _flash_finish(l_ref, acc_ref):
    if l_ref is None:
        return acc_ref[...]
    return acc_ref[...] / jnp.sum(l_ref[...], axis=-1, keepdims=True)


def _causal_flash(streams, s0, tq, tk):
    assert tk in (tq, 2 * tq)
    rows = streams[0][0].shape[0]
    for _, _, _, m_ref, l_ref, acc_ref in streams:
        _flash_init(m_ref, l_ref, acc_ref)

    def step(stream, k0, width, bias=None):
        q, k_ref, v_ref, m_ref, l_ref, acc_ref = stream
        k0 = pl.multiple_of(k0, width)
        _softmax_pv(_nt_dot(q, k_ref[pl.ds(k0, width), :]), v_ref[pl.ds(k0, width), :],
                    m_ref, l_ref, acc_ref, bias)

    def tail_bias(width):
        qpos = (width - tq) + lax.broadcasted_iota(jnp.int32, (tq, width), 0)
        causal = lax.broadcasted_iota(jnp.int32, (tq, width), 1) <= qpos
        return _stack_rows(jnp.where(causal, 0.0, NEG_INF), rows // tq)

    n_full = s0 // tk
    aligned = (s0 - n_full * tk) == 0
    widths = (tq,) if tk == tq else (tq, tk)

    def tail(width, odd_tile=False):
        def body():
            bias = tail_bias(width)
            for stream in streams:
                if odd_tile:
                    step(stream, (n_full - 1) * tk, tk)
                step(stream, s0 + tq - width, width, bias)
        return body

    def when_width(width, extra=None):
        cond = aligned if width == tq else jnp.logical_not(aligned)
        if tk == tq:
            cond = extra
        elif extra is not None:
            cond = cond & extra
        return pl.when(cond) if cond is not None else (lambda f: f())

    def pair(jj, carry):
        for t in range(2):
            for stream in streams:
                step(stream, (2 * jj + t) * tk, tk)
        return carry

    lax.fori_loop(0, n_full // 2, pair, 0)
    for odd_tile in (False, True):
        parity = (n_full % 2 == 1) if odd_tile else (n_full % 2 == 0)
        for width in widths:
            when_width(width, parity)(tail(width, odd_tile))

    return [_flash_finish(l_ref, acc_ref) for _, _, _, _, l_ref, acc_ref in streams]


def _diff_kernel(lq1_ref, lk1_ref, lq2_ref, lk2_ref, subln_ref, q_ref, k_ref, v_ref, o_ref,
                 m_ref, l_ref, acc_ref, *, tq, tk, lam_init, heads):
    qi = pl.program_id(2)
    s0 = qi * tq
    lane = lax.broadcasted_iota(jnp.int32, (tq, LANES), 1)
    zero = jnp.zeros((tq, LANES), F32)
    streams = []
    for h in range(heads):
        slab = pl.ds(h * LANES, LANES)
        qf = q_ref[:, h * LANES:(h + 1) * LANES].astype(F32)
        q2 = jnp.concatenate([jnp.where(lane < HEAD_DIM, qf, zero),
                              jnp.where(lane >= HEAD_DIM, qf, zero)], axis=0).astype(MXU_DTYPE)
        streams.append((q2, k_ref.at[:, slab], v_ref.at[:, slab], m_ref.at[h], l_ref.at[h], acc_ref.at[h]))

    lam = (jnp.exp(jnp.sum(lq1_ref[...] * lk1_ref[...], axis=-1, keepdims=True))
           - jnp.exp(jnp.sum(lq2_ref[...] * lk2_ref[...], axis=-1, keepdims=True)) + lam_init)
    for h, o12 in enumerate(_causal_flash(streams, s0, tq, tk)):
        o = o12[:tq] - lam * o12[tq:]
        o_ref[:, h * LANES:(h + 1) * LANES] = (_rms(o, subln_ref[...]) * (1.0 - lam_init)).astype(o_ref.dtype)


def _nsa_kernel(q_ref, gate_ref, kc_ref, vc_ref, ovt_ref, ka0_ref, ka1_ref, vs0_ref, vs1_ref,
                kw_ref, vw0_ref, vw1_ref, o_ref, m_ref, acc_ref, qsel_ref, part_ref,
                *, tq, tk, nc, n_sel, top_n, n_q_tiles):
    hg = NSA_HEADS_PER_GROUP
    rows = hg * tq
    qi = pl.program_id(1)
    s0 = qi * tq
    lane = lax.broadcasted_iota(jnp.int32, (tq, LANES), 1)
    lane_rows = lax.broadcasted_iota(jnp.int32, (rows, LANES), 1)
    qf = [q_ref[:, h * LANES:(h + 1) * LANES].astype(F32) for h in range(hg)]
    gates = gate_ref[...]
    zero = jnp.zeros((tq, LANES), F32)

    wk = WINDOW + tq
    w0 = pl.multiple_of(jnp.maximum(s0 - WINDOW, 0), tq)
    back = (s0 - w0) + lax.broadcasted_iota(jnp.int32, (tq, wk), 0) \
        - lax.broadcasted_iota(jnp.int32, (tq, wk), 1)
    win_bias = _stack_rows(jnp.where((back >= 0) & (back < WINDOW), 0.0, NEG_INF), hg)

    def normalise(raw, g):
        in_half_rows = (lane_rows >= HEAD_DIM) if g else (lane_rows < HEAD_DIM)
        return raw / jnp.where(in_half_rows, pltpu.roll(raw, HEAD_DIM, 1), 1.0)

    def gate(g, h, branch):
        c = (g * hg + h) * N_BRANCHES + branch
        return gates[:, c:c + 1]

    def phase1(n_blk, nc_eff):
        n_idx = lax.broadcasted_iota(jnp.int32, (tq, nc_eff), 1)
        qpos_c = s0 + lax.broadcasted_iota(jnp.int32, (tq, nc_eff), 0)
        cmp_ok = n_idx * CMP_STRIDE + (CMP_BLOCK - 1) <= qpos_c
        cmp_bias = _stack_rows(jnp.where(cmp_ok, 0.0, NEG_INF), hg)
        cmp_keep = _stack_rows(jnp.where(cmp_ok, 1.0, 0.0), hg)
        jb = lax.broadcasted_iota(jnp.int32, (n_blk, tq), 0)
        qp = s0 + lax.broadcasted_iota(jnp.int32, (n_blk, tq), 1)
        cur = qp // SLC_BLOCK
        valid = jb * SLC_BLOCK <= qp
        forced = (jb == 0) | (jb == cur) | (jb == cur - 1)

        for g in range(NSA_KV_GROUPS):
            in_half = (lane >= HEAD_DIM) if g else (lane < HEAD_DIM)
            q_plain = jnp.concatenate([jnp.where(in_half, qf[h], zero) for h in range(hg)],
                                      axis=0).astype(MXU_DTYPE)

            sc = _nt_dot(q_plain, kc_ref[:nc_eff, :]) + cmp_bias
            pc = jnp.exp2(sc - jnp.max(sc, axis=-1, keepdims=True)) * cmp_keep
            lc = jnp.sum(pc, axis=-1, keepdims=True)
            pc = (pc / jnp.where(lc > 0.0, lc, 1.0)).astype(MXU_DTYPE)
            o_cmp = _dot(pc, vc_ref[:nc_eff, :])

            if n_blk <= top_n:
                selected = valid
            else:
                imp_t = None
                for h in range(hg):
                    part = _nt_dot(ovt_ref[:, :nc_eff], pc[h * tq:(h + 1) * tq])
                    imp_t = part if imp_t is None else imp_t + part
                key = jnp.where(forced, FORCED_IMPORTANCE, jnp.where(valid, imp_t[:n_blk], -1.0))
                n_grp = n_blk // SUBLANES
                key_g = [key[a * SUBLANES:(a + 1) * SUBLANES] for a in range(n_grp)]
                jb_g = lax.broadcasted_iota(jnp.int32, (SUBLANES, tq), 0)
                rank_g = [jnp.zeros((SUBLANES, tq), jnp.int32) for _ in range(n_grp)]
                for i in range(n_blk):
                    ri = key[i:i + 1, :]
                    for a in range(n_grp):
                        if a < i // SUBLANES:
                            ahead = jnp.where(ri > key_g[a], 1, 0)
                        elif a > i // SUBLANES:
                            ahead = jnp.where(ri >= key_g[a], 1, 0)
                        else:
                            ahead = (jnp.where(ri > key_g[a], 1, 0)
                                     + jnp.where((ri == key_g[a]) & (jb_g > i % SUBLANES), 1, 0))
                        rank_g[a] = rank_g[a] + ahead
                selected = (jnp.concatenate(rank_g, axis=0) < top_n) & valid
            bias_t = jnp.where(selected, 0.0, NEG_INF)
            if n_blk < SLC_BLOCK:
                bias_t = jnp.concatenate([bias_t, jnp.full((SLC_BLOCK - n_blk, tq), NEG_INF, F32)], axis=0)
            pad_t = jnp.zeros((SLC_BLOCK, tq), F32)
            bias = jnp.transpose(jnp.concatenate([pad_t, bias_t] if g == 0 else [bias_t, pad_t], axis=0))
            qsel_ref[g] = jnp.concatenate([jnp.where(in_half, qf[h], bias) for h in range(hg)],
                                          axis=0).astype(qsel_ref.dtype)

            sw = _nt_dot(q_plain, kw_ref[pl.ds(w0, wk), :]) + win_bias
            pw = jnp.exp2((sw - jnp.max(sw, axis=-1, keepdims=True)).astype(MXU_DTYPE))
            o_win = normalise(_dot(pw, (vw1_ref if g else vw0_ref)[pl.ds(w0, wk), :]), g)
            part_ref[g] = jnp.concatenate(
                [gate(g, h, 0) * o_cmp[h * tq:(h + 1) * tq] + gate(g, h, 2) * o_win[h * tq:(h + 1) * tq]
                 for h in range(hg)], axis=0)

    tiles_per_part = n_q_tiles // SEQ_PARTS
    for c in range(SEQ_PARTS):
        nc_part = min(nc, -(-(nc * (c + 1) // SEQ_PARTS) // LANES) * LANES)
        pl.when(qi // tiles_per_part == c)(
            functools.partial(phase1, n_sel * (c + 1) // SEQ_PARTS, nc_part))

    outs = []
    raw = _causal_flash([(qsel_ref[0], ka0_ref, vs0_ref, m_ref.at[0], None, acc_ref.at[0]),
                         (qsel_ref[1], ka1_ref, vs1_ref, m_ref.at[1], None, acc_ref.at[1])], s0, tq, tk)
    for g in range(NSA_KV_GROUPS):
        o_slc = normalise(raw[g], g)
        outs.append([part_ref[g, h * tq:(h + 1) * tq, :] + gate(g, h, 1) * o_slc[h * tq:(h + 1) * tq]
                     for h in range(hg)])

    for h in range(hg):
        o_ref[:, h * LANES:(h + 1) * LANES] = jnp.where(lane < HEAD_DIM, outs[0][h], outs[1][h]).astype(o_ref.dtype)


def _post_kernel(x_ref, od_ref, on_ref, wod_ref, won_ref, gpost_ref, gpre_ref, wg_ref, wu_ref, wd_ref,
                 gffn_ref, o_ref, act_ref, *, d_ff):
    mix = _dot(od_ref[...], wod_ref[...]) + _dot(on_ref[...], won_ref[...])
    x1 = x_ref[...] + _rms(mix, gpost_ref[...])
    h = _rms(x1, gpre_ref[...]).astype(MXU_DTYPE)
    for c in range(0, d_ff, FF_CHUNK):
        gate = _dot(h, wg_ref[:, c:c + FF_CHUNK])
        up = _dot(h, wu_ref[:, c:c + FF_CHUNK])
        act_ref[:, c:c + FF_CHUNK] = (gate * jax.nn.sigmoid(gate) * up).astype(act_ref.dtype)
    f = _dot(act_ref[...], wd_ref[...])
    o_ref[...] = x1 + _rms(f, gffn_ref[...])


def _resident(shape):
    nd = len(shape)
    return pl.BlockSpec(shape, lambda *_: (0,) * nd, pipeline_mode=pl.Buffered(1))


def _params(sem):
    return pltpu.CompilerParams(dimension_semantics=sem, vmem_limit_bytes=VMEM_LIMIT)


def _rope_tables(S):
    inv = 1.0 / (ROPE_THETA ** (jnp.arange(0, HEAD_DIM, 2, dtype=F32) / HEAD_DIM))
    ang = jnp.arange(S, dtype=F32)[:, None] * inv[None, :]
    cos, sin = jnp.cos(ang), jnp.sin(ang)
    return jnp.tile(cos, (1, 4)), jnp.concatenate([-sin, sin, -sin, sin], axis=1)


def _selection_overlap_t(nc, n_cmp, n_sel):
    c0 = np.arange(n_cmp)[:, None] * CMP_STRIDE
    b0 = np.arange(n_sel)[None, :] * SLC_BLOCK
    ov = np.clip(np.minimum(c0 + CMP_BLOCK, b0 + SLC_BLOCK) - np.maximum(c0, b0), 0, None) / CMP_BLOCK
    full = np.zeros((LANES, nc), np.float32)
    full[:n_sel, :n_cmp] = ov.T
    return full


def _layer(x, layer, attn_pre_norm, w_in, lq1, lk1, lq2, lk2, diff_subln, k_pos, k_w1, k_w2,
           v_pos, v_w1, v_w2, w_out, attn_post_norm, ffn_pre_norm, w_gate, w_up, w_down, ffn_post_norm):
    B, S, D = x.shape
    N = B * S
    d_ff = w_gate.shape[1]
    nc = S // CMP_STRIDE
    n_cmp = (S - CMP_BLOCK) // CMP_STRIDE + 1
    n_sel = S // SLC_BLOCK
    top_n = min(SLC_TOPK, n_sel)
    assert n_sel <= SLC_BLOCK
    assert S % max(INPROJ_ROWS, DIFF_TILE[1], NSA_TILE[1]) == 0 and S >= WINDOW + NSA_TILE[0]
    assert (S // NSA_TILE[0]) % SEQ_PARTS == 0 and (n_sel // SEQ_PARTS) % SUBLANES == 0
    lam_init = 0.8 - 0.6 * math.exp(-0.3 * layer)
    dt = MXU_DTYPE

    hg, G, d = NSA_HEADS_PER_GROUP, NSA_KV_GROUPS, HEAD_DIM
    diff_w = DIFF_HEADS * 2 * d
    nsa_w = G * hg * d
    kv_w = G * d
    off_nq = 3 * diff_w
    off_kc = off_nq + nsa_w
    off_ks = off_kc + 2 * kv_w
    off_gate = off_kc + 6 * kv_w
    n_gates = G * hg * N_BRANCHES
    head_starts = [(g * hg + h) * d for h in range(hg) for g in range(G)]
    w_cat = jnp.concatenate(
        [w_in[:, :off_nq]] + [w_in[:, off_nq + c:off_nq + c + d] for c in head_starts]
        + [w_in[:, off_ks:off_gate],
           w_in[:, off_kc:off_ks],
           w_in[:, off_gate:off_gate + n_gates],
           jnp.zeros((D, LANES - n_gates), w_in.dtype)], axis=1).astype(dt)
    w_out_d = w_out[:diff_w].astype(dt)
    w_out_n = jnp.concatenate([w_out[diff_w + c:diff_w + c + d] for c in head_starts], axis=0).astype(dt)
    cos_t, sin_t = _rope_tables(S)

    tm = INPROJ_ROWS
    seq_tiles = S // tm
    n_cols = (N_MAIN_SLABS + N_AUX_SLABS) * LANES
    p, aux = pl.pallas_call(
        functools.partial(_inproj_kernel, tm=tm, seq_tiles=seq_tiles),
        grid=(N // tm,),
        in_specs=[pl.BlockSpec((tm, D), lambda i: (i, 0)),
                  _resident((1, D)),
                  _resident((D, n_cols)),
                  pl.BlockSpec((tm, LANES), lambda i: (i % seq_tiles, 0)),
                  pl.BlockSpec((tm, LANES), lambda i: (i % seq_tiles, 0))],
        out_specs=[pl.BlockSpec((tm, N_P_SLABS * LANES), lambda i: (i, 0)),
                   pl.BlockSpec((tm, N_AUX_SLABS * LANES), lambda i: (i, 0))],
        out_shape=[jax.ShapeDtypeStruct((N, N_P_SLABS * LANES), dt),
                   jax.ShapeDtypeStruct((N, N_AUX_SLABS * LANES), F32)],
        compiler_params=_params(("parallel",)),
        name="inproj",
    )(x.reshape(N, D), attn_pre_norm.reshape(1, D), w_cat, cos_t, sin_t)
    p = p.reshape(B, S, N_P_SLABS * LANES)
    aux = aux.reshape(B, S, N_AUX_SLABS * LANES)

    def pos_rows(pos):
        tiled = jnp.broadcast_to(pos.reshape(2, CMP_STRIDE, 1, d), (2, CMP_STRIDE, G, d))
        return tiled[0].reshape(1, -1), tiled[1].reshape(1, -1)

    def w1_rows(w1):
        w = jnp.broadcast_to(w1.astype(dt).reshape(2, CMP_STRIDE, 1, d, CMP_HIDDEN),
                             (2, CMP_STRIDE, G, d, CMP_HIDDEN)).reshape(2, CMP_STRIDE * G * d, CMP_HIDDEN)
        return w[0], w[1]

    def w2_halves(w2):
        return jnp.stack([jnp.pad(w2, ((0, 0), (g * d, (G - 1 - g) * d))) for g in range(G)]).astype(dt)

    kpl, kph = pos_rows(k_pos)
    vpl, vph = pos_rows(v_pos)
    kwl, kwh = w1_rows(k_w1)
    vwl, vwh = w1_rows(v_w1)
    chunk_w = CMP_STRIDE * G * d
    w1_spec = _resident((chunk_w, CMP_HIDDEN))
    w2_spec = _resident((G, CMP_HIDDEN, LANES))
    kcmp, vcmp = pl.pallas_call(
        functools.partial(_compress_kernel, nc=nc),
        grid=(B,),
        in_specs=[pl.BlockSpec((None, S, LANES), lambda b: (b, 0, A_KC)),
                  pl.BlockSpec((None, S, LANES), lambda b: (b, 0, A_VC)),
                  _resident((1, chunk_w)), _resident((1, chunk_w)), _resident((1, chunk_w)), _resident((1, chunk_w)),
                  w1_spec, w1_spec, w2_spec, w1_spec, w1_spec, w2_spec],
        out_specs=[pl.BlockSpec((None, nc, LANES), lambda b: (b, 0, 0)),
                   pl.BlockSpec((None, nc, LANES), lambda b: (b, 0, 0))],
        out_shape=[jax.ShapeDtypeStruct((B, nc, LANES), dt), jax.ShapeDtypeStruct((B, nc, LANES), dt)],
        compiler_params=_params(("parallel",)),
        name="compress",
    )(aux, aux, kpl, kph, vpl, vph, kwl, kwh, w2_halves(k_w2), vwl, vwh, w2_halves(v_w2))

    tq_d, tk_d = DIFF_TILE
    lam_vec = [v.reshape(1, d) for v in (lq1, lk1, lq2, lk2)]
    hp = DIFF_HEADS_PER_STEP
    hw = hp * LANES
    n_hp = DIFF_HEADS // hp
    o_diff = pl.pallas_call(
        functools.partial(_diff_kernel, tq=tq_d, tk=tk_d, lam_init=lam_init, heads=hp),
        grid=(B, n_hp, S // tq_d),
        in_specs=[_resident((1, d))] * 4 + [_resident((1, LANES)),
                  pl.BlockSpec((None, tq_d, hw), lambda b, h, i: (b, i, h)),
                  pl.BlockSpec((None, S, hw), lambda b, h, i: (b, 0, P_DK // hp + h)),
                  pl.BlockSpec((None, S, hw), lambda b, h, i: (b, 0, P_DV // hp + h))],
        out_specs=pl.BlockSpec((None, tq_d, hw), lambda b, h, i: (b, i, h)),
        out_shape=jax.ShapeDtypeStruct((B, S, DIFF_HEADS * LANES), dt),
        scratch_shapes=[pltpu.VMEM((hp, 2 * tq_d, LANES), F32)] * 3,
        compiler_params=_params(("parallel", "parallel", "arbitrary")),
        name="diff_attn",
    )(*lam_vec, diff_subln.reshape(1, LANES), p, p, p)

    tq_n, tk_n = NSA_TILE
    ovt = jnp.asarray(_selection_overlap_t(nc, n_cmp, n_sel)).astype(dt)
    rows = NSA_HEADS_PER_GROUP * tq_n

    def seq_slab(c):
        return pl.BlockSpec((None, S, LANES), lambda b, i: (b, 0, c))

    o_nsa = pl.pallas_call(
        functools.partial(_nsa_kernel, tq=tq_n, tk=tk_n, nc=nc, n_sel=n_sel, top_n=top_n, n_q_tiles=S // tq_n),
        grid=(B, S // tq_n),
        in_specs=[pl.BlockSpec((None, tq_n, hg * LANES), lambda b, i: (b, i, P_NQ // hg)),
                  pl.BlockSpec((None, tq_n, LANES), lambda b, i: (b, i, A_GATE)),
                  pl.BlockSpec((None, nc, LANES), lambda b, i: (b, 0, 0)),
                  pl.BlockSpec((None, nc, LANES), lambda b, i: (b, 0, 0)),
                  _resident((LANES, nc)),
                  seq_slab(P_KS), seq_slab(P_KS + 1), seq_slab(P_VS), seq_slab(P_VS + 1), seq_slab(P_KW),
                  seq_slab(P_VW), seq_slab(P_VW + 1)],
        out_specs=pl.BlockSpec((None, tq_n, 4 * LANES), lambda b, i: (b, i, 0)),
        out_shape=jax.ShapeDtypeStruct((B, S, 4 * LANES), dt),
        scratch_shapes=[pltpu.VMEM((G, rows, LANES), F32)] * 2
        + [pltpu.VMEM((G, rows, LANES), dt), pltpu.VMEM((G, rows, LANES), F32)],
        compiler_params=_params(("parallel", "arbitrary")),
        name="nsa_attn",
    )(p, aux, kcmp, vcmp, ovt, p, p, p, p, p, p, p)

    tm2 = POST_ROWS
    row = lambda i: (i, 0)
    out = pl.pallas_call(
        functools.partial(_post_kernel, d_ff=d_ff),
        grid=(N // tm2,),
        in_specs=[pl.BlockSpec((tm2, D), row),
                  pl.BlockSpec((tm2, diff_w), row), pl.BlockSpec((tm2, nsa_w), row),
                  _resident((diff_w, D)), _resident((nsa_w, D)), _resident((1, D)), _resident((1, D)),
                  _resident((D, d_ff)), _resident((D, d_ff)), _resident((d_ff, D)), _resident((1, D))],
        out_specs=pl.BlockSpec((tm2, D), row),
        out_shape=jax.ShapeDtypeStruct((N, D), F32),
        scratch_shapes=[pltpu.VMEM((tm2, d_ff), dt)],
        compiler_params=_params(("parallel",)),
        name="post",
    )(x.reshape(N, D), o_diff.reshape(N, diff_w), o_nsa.reshape(N, nsa_w), w_out_d, w_out_n,
      attn_post_norm.reshape(1, D), ffn_pre_norm.reshape(1, D),
      w_gate.astype(dt), w_up.astype(dt), w_down.astype(dt), ffn_post_norm.reshape(1, D))
    return out.reshape(B, S, D)


def kernel(x, attn_pre_norm, w_in, lambda_q1, lambda_k1, lambda_q2, lambda_k2, diff_subln, k_cmp_pos, k_cmp_w1, k_cmp_w2, v_cmp_pos, v_cmp_w1, v_cmp_w2, w_out, attn_post_norm, ffn_pre_norm, w_gate, w_up, w_down, ffn_post_norm):
    for l in range(w_in.shape[0]):
        x = _layer(x, l, attn_pre_norm[l], w_in[l], lambda_q1[l], lambda_k1[l], lambda_q2[l], lambda_k2[l],
                   diff_subln[l], k_cmp_pos[l], k_cmp_w1[l], k_cmp_w2[l], v_cmp_pos[l], v_cmp_w1[l], v_cmp_w2[l],
                   w_out[l], attn_post_norm[l], ffn_pre_norm[l], w_gate[l], w_up[l], w_down[l], ffn_post_norm[l])
    return x
```

```python
import functools
import math

import numpy as np
import jax
import jax.numpy as jnp
from jax import lax
from jax.experimental import pallas as pl
from jax.experimental.pallas import tpu as pltpu

F32 = jnp.float32
MXU_DTYPE = jnp.bfloat16

LANES = 128
SUBLANES = 8
HEAD_DIM = 64
ROPE_THETA = 10000.0
NORM_EPS = 1e-6
NEG_INF = -1e30
LOG2_E = 1.4426950408889634
FORCED_IMPORTANCE = 3e38

DIFF_HEADS = 4
NSA_HEADS_PER_GROUP = 4
NSA_KV_GROUPS = 2
CMP_BLOCK = 32
CMP_STRIDE = 16
CMP_HIDDEN = 4 * HEAD_DIM
SLC_BLOCK = 64
SLC_TOPK = 16
WINDOW = 512
N_BRANCHES = 3
SEQ_PARTS = 4

M_DQ, M_DK, M_DV, M_NQ, M_KS, M_VS, M_KW, M_VW, M_KC, M_VC, M_GATE = 0, 4, 8, 12, 16, 17, 18, 19, 20, 21, 22
N_MAIN_SLABS = 20
N_AUX_SLABS = 3
P_DQ, P_DK, P_DV, P_NQ, P_KS, P_VS, P_KW, P_VW = 0, 4, 8, 12, 16, 18, 20, 21
N_P_SLABS = 23
A_KC, A_VC, A_GATE = 0, 1, 2
VMEM_LIMIT = 56 * 1024 * 1024

MXU_WIDTH = 256
INPROJ_ROWS = 1024
POST_ROWS = 512
FF_CHUNK = MXU_WIDTH
DIFF_TILE = (512, 1024)
DIFF_HEADS_PER_STEP = 4
NSA_TILE = (256, 512)


def _nt_dot(a, b):
    return lax.dot_general(a, b, (((1,), (1,)), ((), ())), preferred_element_type=F32)


def _dot(a, b):
    return jnp.dot(a, b, preferred_element_type=F32)


def _rms(x, g):
    return x * lax.rsqrt(jnp.mean(x * x, axis=-1, keepdims=True) + NORM_EPS) * g


def _inproj_kernel(x_ref, g_ref, w_ref, cos_ref, sin_ref, p_ref, aux_ref, *, tm, seq_tiles):
    h = _rms(x_ref[...], g_ref[...]).astype(MXU_DTYPE)
    cos = cos_ref[...]
    sin = sin_ref[...]
    lane = lax.broadcasted_iota(jnp.int32, (tm, LANES), 1)
    low_half = (lane & (HEAD_DIM - 1)) < HEAD_DIM // 2

    def rope(y):
        fwd = pltpu.roll(y, HEAD_DIM // 2, 1)
        bwd = pltpu.roll(y, LANES - HEAD_DIM // 2, 1)
        return y * cos + jnp.where(low_half, bwd, fwd) * sin

    pos = (pl.program_id(0) % seq_tiles) * tm + lax.broadcasted_iota(jnp.int32, (tm, LANES), 0)
    blk = pos // SLC_BLOCK
    scale = HEAD_DIM ** -0.5 * LOG2_E

    n_slabs = N_MAIN_SLABS + N_AUX_SLABS
    per_dot = MXU_WIDTH // LANES
    for c0 in range(0, n_slabs, per_dot):
        c1 = min(c0 + per_dot, n_slabs)
        y2 = _dot(h, w_ref[:, c0 * LANES:c1 * LANES])
        for s in range(c0, c1):
            y = y2[:, (s - c0) * LANES:(s - c0 + 1) * LANES]

            def put(ref, slab, val):
                ref[:, slab * LANES:(slab + 1) * LANES] = val.astype(ref.dtype)

            if M_DQ <= s < M_DK or M_NQ <= s < M_KS:
                put(p_ref, s, rope(y) * scale)
            elif M_DK <= s < M_DV:
                put(p_ref, s, rope(y))
            elif M_DV <= s < M_NQ:
                put(p_ref, s, y)
            elif s == M_KS:
                r = rope(y)
                put(p_ref, P_KS, jnp.where(lane < HEAD_DIM, r, jnp.where(lane - HEAD_DIM == blk, 1.0, 0.0)))
                put(p_ref, P_KS + 1, jnp.where(lane >= HEAD_DIM, r, jnp.where(lane == blk, 1.0, 0.0)))
            elif s in (M_VS, M_VW):
                o = P_VS if s == M_VS else P_VW
                put(p_ref, o, jnp.where(lane < HEAD_DIM, y, 1.0))
                put(p_ref, o + 1, jnp.where(lane >= HEAD_DIM, y, 1.0))
            elif s == M_KW:
                put(p_ref, P_KW, rope(y))
            elif s == M_KC:
                put(aux_ref, A_KC, rope(y))
            elif s == M_VC:
                put(aux_ref, A_VC, y)
            else:
                put(aux_ref, A_GATE, jax.nn.sigmoid(y))


def _compress_kernel(tk_ref, tv_ref, kpl_ref, kph_ref, vpl_ref, vph_ref,
                     kwl_ref, kwh_ref, kw2_ref, vwl_ref, vwh_ref, vw2_ref, ko_ref, vo_ref, *, nc):
    lane = lax.broadcasted_iota(jnp.int32, (nc, CMP_STRIDE * LANES), 1)
    group_lanes = [(lane & HEAD_DIM) == 0, (lane & HEAD_DIM) != 0]

    def compress(t_ref, plo_ref, phi_ref, wlo_ref, whi_ref, w2_ref):
        x = jnp.concatenate([t_ref[pl.ds(l, nc, stride=CMP_STRIDE), :] for l in range(CMP_STRIDE)], axis=1)
        x_lo = x + plo_ref[...]
        x_hi = x + phi_ref[...]
        out = None
        for g in range(NSA_KV_GROUPS):
            a = _dot(jnp.where(group_lanes[g], x_lo, 0.0).astype(MXU_DTYPE), wlo_ref[...])
            b = _dot(jnp.where(group_lanes[g], x_hi, 0.0).astype(MXU_DTYPE), whi_ref[...])
            hid = a + pltpu.roll(b, nc - 1, 0)
            act = hid * jax.nn.sigmoid(hid)
            o = _dot(act.astype(MXU_DTYPE), w2_ref[g])
            out = o if out is None else out + o
        return out

    ko_ref[...] = compress(tk_ref, kpl_ref, kph_ref, kwl_ref, kwh_ref, kw2_ref).astype(ko_ref.dtype)
    vo_ref[...] = compress(tv_ref, vpl_ref, vph_ref, vwl_ref, vwh_ref, vw2_ref).astype(vo_ref.dtype)


def _flash_init(m_ref, l_ref, acc_ref):
    m_ref[...] = jnp.full(m_ref.shape, NEG_INF, F32)
    if l_ref is not None:
        l_ref[...] = jnp.zeros(l_ref.shape, F32)
    acc_ref[...] = jnp.zeros(acc_ref.shape, F32)


def _lane_tiles(x):
    return [x[:, c:c + LANES] for c in range(0, x.shape[1], LANES)]


def _stack_rows(x, n):
    return jnp.concatenate([x] * n, axis=0)


def _softmax_pv(s, v, m_ref, l_ref, acc_ref, bias=None):
    if bias is not None:
        s = s + bias
    n_tiles = s.shape[1] // LANES
    m_prev = m_ref[...]
    m_new = jnp.maximum(m_prev, jnp.max(s, axis=-1, keepdims=True))
    alpha = jnp.exp2(m_prev - m_new)
    x = s - jnp.concatenate([m_new] * n_tiles, axis=1)
    if l_ref is None:
        p = jnp.exp2(x.astype(MXU_DTYPE))
    else:
        p = jnp.exp2(x)
        l_ref[...] = alpha * l_ref[...] + functools.reduce(lambda a, b: a + b, _lane_tiles(p))
        p = p.astype(MXU_DTYPE)
    acc_ref[...] = alpha * acc_ref[...] + _dot(p, v)
    m_ref[...] = m_new


def _flash_finish(l_ref, acc_ref):
    if l_ref is None:
        return acc_ref[...]
    return acc_ref[...] / jnp.sum(l_ref[...], axis=-1, keepdims=True)


def _causal_flash(streams, s0, tq, tk):
    assert tk in (tq, 2 * tq)
    rows = streams[0][0].shape[0]
    for _, _, _, m_ref, l_ref, acc_ref in streams:
        _flash_init(m_ref, l_ref, acc_ref)

    def step(stream, k0, width, bias=None):
        q, k_ref, v_ref, m_ref, l_ref, acc_ref = stream
        k0 = pl.multiple_of(k0, width)
        _softmax_pv(_nt_dot(q, k_ref[pl.ds(k0, width), :]), v_ref[pl.ds(k0, width), :],
                    m_ref, l_ref, acc_ref, bias)

    def tail_bias(width):
        qpos = (width - tq) + lax.broadcasted_iota(jnp.int32, (tq, width), 0)
        causal = lax.broadcasted_iota(jnp.int32, (tq, width), 1) <= qpos
        return _stack_rows(jnp.where(causal, 0.0, NEG_INF), rows // tq)

    n_full = s0 // tk
    aligned = (s0 - n_full * tk) == 0
    widths = (tq,) if tk == tq else (tq, tk)

    def tail(width, odd_tile=False):
        def body():
            bias = tail_bias(width)
            for stream in streams:
                if odd_tile:
                    step(stream, (n_full - 1) * tk, tk)
                step(stream, s0 + tq - width, width, bias)
        return body

    def when_width(width, extra=None):
        cond = aligned if width == tq else jnp.logical_not(aligned)
        if tk == tq:
            cond = extra
        elif extra is not None:
            cond = cond & extra
        return pl.when(cond) if cond is not None else (lambda f: f())

    tiles_per_trip = 2 if len(streams) <= 2 else 1

    def trip(jj, carry):
        for t in range(tiles_per_trip):
            for stream in streams:
                step(stream, (tiles_per_trip * jj + t) * tk, tk)
        return carry

    lax.fori_loop(0, n_full // tiles_per_trip, trip, 0)
    if tiles_per_trip == 1:
        for width in widths:
            when_width(width)(tail(width))
    else:
        for odd_tile in (False, True):
            parity = (n_full % 2 == 1) if odd_tile else (n_full % 2 == 0)
            for width in widths:
                when_width(width, parity)(tail(width, odd_tile))

    return [_flash_finish(l_ref, acc_ref) for _, _, _, _, l_ref, acc_ref in streams]


def _diff_kernel(lq1_ref, lk1_ref, lq2_ref, lk2_ref, subln_ref, q_ref, k_ref, v_ref, o_ref,
                 m_ref, l_ref, acc_ref, *, tq, tk, lam_init, heads):
    qi = pl.program_id(2)
    s0 = qi * tq
    lane = lax.broadcasted_iota(jnp.int32, (tq, LANES), 1)
    zero = jnp.zeros((tq, LANES), F32)
    streams = []
    for h in range(heads):
        slab = pl.ds(h * LANES, LANES)
        qf = q_ref[:, h * LANES:(h + 1) * LANES].astype(F32)
        q2 = jnp.concatenate([jnp.where(lane < HEAD_DIM, qf, zero),
                              jnp.where(lane >= HEAD_DIM, qf, zero)], axis=0).astype(MXU_DTYPE)
        streams.append((q2, k_ref.at[:, slab], v_ref.at[:, slab], m_ref.at[h], l_ref.at[h], acc_ref.at[h]))

    lam = (jnp.exp(jnp.sum(lq1_ref[...] * lk1_ref[...], axis=-1, keepdims=True))
           - jnp.exp(jnp.sum(lq2_ref[...] * lk2_ref[...], axis=-1, keepdims=True)) + lam_init)
    for h, o12 in enumerate(_causal_flash(streams, s0, tq, tk)):
        o = o12[:tq] - lam * o12[tq:]
        o_ref[:, h * LANES:(h + 1) * LANES] = (_rms(o, subln_ref[...]) * (1.0 - lam_init)).astype(o_ref.dtype)


def _nsa_kernel(q_ref, gate_ref, kc_ref, vc_ref, ovt_ref, ka0_ref, ka1_ref, vs0_ref, vs1_ref,
                kw_ref, vw0_ref, vw1_ref, o_ref, m_ref, acc_ref, qsel_ref, part_ref,
                *, tq, tk, nc, n_sel, top_n, n_q_tiles):
    hg = NSA_HEADS_PER_GROUP
    rows = hg * tq
    qi = pl.program_id(1)
    s0 = qi * tq
    lane = lax.broadcasted_iota(jnp.int32, (tq, LANES), 1)
    lane_rows = lax.broadcasted_iota(jnp.int32, (rows, LANES), 1)
    qf = [q_ref[:, h * LANES:(h + 1) * LANES].astype(F32) for h in range(hg)]
    gates = gate_ref[...]
    zero = jnp.zeros((tq, LANES), F32)

    wk = WINDOW + tq
    w0 = pl.multiple_of(jnp.maximum(s0 - WINDOW, 0), tq)
    back = (s0 - w0) + lax.broadcasted_iota(jnp.int32, (tq, wk), 0) \
        - lax.broadcasted_iota(jnp.int32, (tq, wk), 1)
    win_bias = _stack_rows(jnp.where((back >= 0) & (back < WINDOW), 0.0, NEG_INF), hg)

    def normalise(raw, g):
        in_half_rows = (lane_rows >= HEAD_DIM) if g else (lane_rows < HEAD_DIM)
        return raw / jnp.where(in_half_rows, pltpu.roll(raw, HEAD_DIM, 1), 1.0)

    def gate(g, h, branch):
        c = (g * hg + h) * N_BRANCHES + branch
        return gates[:, c:c + 1]

    def phase1(n_blk, nc_eff):
        n_idx = lax.broadcasted_iota(jnp.int32, (tq, nc_eff), 1)
        qpos_c = s0 + lax.broadcasted_iota(jnp.int32, (tq, nc_eff), 0)
        cmp_ok = n_idx * CMP_STRIDE + (CMP_BLOCK - 1) <= qpos_c
        cmp_bias = _stack_rows(jnp.where(cmp_ok, 0.0, NEG_INF), hg)
        cmp_keep = _stack_rows(jnp.where(cmp_ok, 1.0, 0.0), hg)
        jb = lax.broadcasted_iota(jnp.int32, (n_blk, tq), 0)
        qp = s0 + lax.broadcasted_iota(jnp.int32, (n_blk, tq), 1)
        cur = qp // SLC_BLOCK
        valid = jb * SLC_BLOCK <= qp
        forced = (jb == 0) | (jb == cur) | (jb == cur - 1)

        for g in range(NSA_KV_GROUPS):
            in_half = (lane >= HEAD_DIM) if g else (lane < HEAD_DIM)
            q_plain = jnp.concatenate([jnp.where(in_half, qf[h], zero) for h in range(hg)],
                                      axis=0).astype(MXU_DTYPE)

            sc = _nt_dot(q_plain, kc_ref[:nc_eff, :]) + cmp_bias
            pc = jnp.exp2(sc - jnp.max(sc, axis=-1, keepdims=True)) * cmp_keep
            lc = jnp.sum(pc, axis=-1, keepdims=True)
            pc = (pc / jnp.where(lc > 0.0, lc, 1.0)).astype(MXU_DTYPE)
            o_cmp = _dot(pc, vc_ref[:nc_eff, :])

            if n_blk <= top_n:
                selected = valid
            else:
                imp_t = None
                for h in range(hg):
                    part = _nt_dot(ovt_ref[:, :nc_eff], pc[h * tq:(h + 1) * tq])
                    imp_t = part if imp_t is None else imp_t + part
                key = jnp.where(forced, FORCED_IMPORTANCE, jnp.where(valid, imp_t[:n_blk], -1.0))
                n_grp = n_blk // SUBLANES
                key_g = [key[a * SUBLANES:(a + 1) * SUBLANES] for a in range(n_grp)]
                jb_g = lax.broadcasted_iota(jnp.int32, (SUBLANES, tq), 0)
                rank_g = [jnp.zeros((SUBLANES, tq), jnp.int32) for _ in range(n_grp)]
                for i in range(n_blk):
                    ri = key[i:i + 1, :]
                    for a in range(n_grp):
                        if a < i // SUBLANES:
                            ahead = jnp.where(ri > key_g[a], 1, 0)
                        elif a > i // SUBLANES:
                            ahead = jnp.where(ri >= key_g[a], 1, 0)
                        else:
                            ahead = (jnp.where(ri > key_g[a], 1, 0)
                                     + jnp.where((ri == key_g[a]) & (jb_g > i % SUBLANES), 1, 0))
                        rank_g[a] = rank_g[a] + ahead
                selected = (jnp.concatenate(rank_g, axis=0) < top_n) & valid
            bias_t = jnp.where(selected, 0.0, NEG_INF)
            if n_blk < SLC_BLOCK:
                bias_t = jnp.concatenate([bias_t, jnp.full((SLC_BLOCK - n_blk, tq), NEG_INF, F32)], axis=0)
            pad_t = jnp.zeros((SLC_BLOCK, tq), F32)
            bias = jnp.transpose(jnp.concatenate([pad_t, bias_t] if g == 0 else [bias_t, pad_t], axis=0))
            qsel_ref[g] = jnp.concatenate([jnp.where(in_half, qf[h], bias) for h in range(hg)],
                                          axis=0).astype(qsel_ref.dtype)

            sw = _nt_dot(q_plain, kw_ref[pl.ds(w0, wk), :]) + win_bias
            pw = jnp.exp2((sw - jnp.max(sw, axis=-1, keepdims=True)).astype(MXU_DTYPE))
            o_win = normalise(_dot(pw, (vw1_ref if g else vw0_ref)[pl.ds(w0, wk), :]), g)
            part_ref[g] = jnp.concatenate(
                [gate(g, h, 0) * o_cmp[h * tq:(h + 1) * tq] + gate(g, h, 2) * o_win[h * tq:(h + 1) * tq]
                 for h in range(hg)], axis=0)

    tiles_per_part = n_q_tiles // SEQ_PARTS
    for c in range(SEQ_PARTS):
        nc_part = min(nc, -(-(nc * (c + 1) // SEQ_PARTS) // LANES) * LANES)
        pl.when(qi // tiles_per_part == c)(
            functools.partial(phase1, n_sel * (c + 1) // SEQ_PARTS, nc_part))

    outs = []
    raw = _causal_flash([(qsel_ref[0], ka0_ref, vs0_ref, m_ref.at[0], None, acc_ref.at[0]),
                         (qsel_ref[1], ka1_ref, vs1_ref, m_ref.at[1], None, acc_ref.at[1])], s0, tq, tk)
    for g in range(NSA_KV_GROUPS):
        o_slc = normalise(raw[g], g)
        outs.append([part_ref[g, h * tq:(h + 1) * tq, :] + gate(g, h, 1) * o_slc[h * tq:(h + 1) * tq]
                     for h in range(hg)])

    for h in range(hg):
        o_ref[:, h * LANES:(h + 1) * LANES] = jnp.where(lane < HEAD_DIM, outs[0][h], outs[1][h]).astype(o_ref.dtype)


def _post_kernel(x_ref, od_ref, on_ref, wod_ref, won_ref, gpost_ref, gpre_ref, wg_ref, wu_ref, wd_ref,
                 gffn_ref, o_ref, act_ref, *, d_ff):
    mix = _dot(od_ref[...], wod_ref[...]) + _dot(on_ref[...], won_ref[...])
    x1 = x_ref[...] + _rms(mix, gpost_ref[...])
    h = _rms(x1, gpre_ref[...]).astype(MXU_DTYPE)
    for c in range(0, d_ff, FF_CHUNK):
        gate = _dot(h, wg_ref[:, c:c + FF_CHUNK])
        up = _dot(h, wu_ref[:, c:c + FF_CHUNK])
        act_ref[:, c:c + FF_CHUNK] = (gate * jax.nn.sigmoid(gate) * up).astype(act_ref.dtype)
    f = _dot(act_ref[...], wd_ref[...])
    o_ref[...] = x1 + _rms(f, gffn_ref[...])


def _resident(shape):
    nd = len(shape)
    return pl.BlockSpec(shape, lambda *_: (0,) * nd, pipeline_mode=pl.Buffered(1))


def _params(sem):
    return pltpu.CompilerParams(dimension_semantics=sem, vmem_limit_bytes=VMEM_LIMIT)


def _rope_tables(S):
    inv = 1.0 / (ROPE_THETA ** (jnp.arange(0, HEAD_DIM, 2, dtype=F32) / HEAD_DIM))
    ang = jnp.arange(S, dtype=F32)[:, None] * inv[None, :]
    cos, sin = jnp.cos(ang), jnp.sin(ang)
    return jnp.tile(cos, (1, 4)), jnp.concatenate([-sin, sin, -sin, sin], axis=1)


def _selection_overlap_t(nc, n_cmp, n_sel):
    c0 = np.arange(n_cmp)[:, None] * CMP_STRIDE
    b0 = np.arange(n_sel)[None, :] * SLC_BLOCK
    ov = np.clip(np.minimum(c0 + CMP_BLOCK, b0 + SLC_BLOCK) - np.maximum(c0, b0), 0, None) / CMP_BLOCK
    full = np.zeros((LANES, nc), np.float32)
    full[:n_sel, :n_cmp] = ov.T
    return full


def _layer(x, layer, attn_pre_norm, w_in, lq1, lk1, lq2, lk2, diff_subln, k_pos, k_w1, k_w2,
           v_pos, v_w1, v_w2, w_out, attn_post_norm, ffn_pre_norm, w_gate, w_up, w_down, ffn_post_norm):
    B, S, D = x.shape
    N = B * S
    d_ff = w_gate.shape[1]
    nc = S // CMP_STRIDE
    n_cmp = (S - CMP_BLOCK) // CMP_STRIDE + 1
    n_sel = S // SLC_BLOCK
    top_n = min(SLC_TOPK, n_sel)
    assert n_sel <= SLC_BLOCK
    assert S % max(INPROJ_ROWS, DIFF_TILE[1], NSA_TILE[1]) == 0 and S >= WINDOW + NSA_TILE[0]
    assert (S // NSA_TILE[0]) % SEQ_PARTS == 0 and (n_sel // SEQ_PARTS) % SUBLANES == 0
    assert d_ff % FF_CHUNK == 0 and N % max(INPROJ_ROWS, POST_ROWS) == 0
    lam_init = 0.8 - 0.6 * math.exp(-0.3 * layer)
    dt = MXU_DTYPE

    hg, G, d = NSA_HEADS_PER_GROUP, NSA_KV_GROUPS, HEAD_DIM
    diff_w = DIFF_HEADS * 2 * d
    nsa_w = G * hg * d
    kv_w = G * d
    off_nq = 3 * diff_w
    off_kc = off_nq + nsa_w
    off_ks = off_kc + 2 * kv_w
    off_gate = off_kc + 6 * kv_w
    n_gates = G * hg * N_BRANCHES
    head_starts = [(g * hg + h) * d for h in range(hg) for g in range(G)]
    w_cat = jnp.concatenate(
        [w_in[:, :off_nq]] + [w_in[:, off_nq + c:off_nq + c + d] for c in head_starts]
        + [w_in[:, off_ks:off_gate],
           w_in[:, off_kc:off_ks],
           w_in[:, off_gate:off_gate + n_gates],
           jnp.zeros((D, LANES - n_gates), w_in.dtype)], axis=1).astype(dt)
    w_out_d = w_out[:diff_w].astype(dt)
    w_out_n = jnp.concatenate([w_out[diff_w + c:diff_w + c + d] for c in head_starts], axis=0).astype(dt)
    cos_t, sin_t = _rope_tables(S)

    tm = INPROJ_ROWS
    seq_tiles = S // tm
    n_cols = (N_MAIN_SLABS + N_AUX_SLABS) * LANES
    p, aux = pl.pallas_call(
        functools.partial(_inproj_kernel, tm=tm, seq_tiles=seq_tiles),
        grid=(N // tm,),
        in_specs=[pl.BlockSpec((tm, D), lambda i: (i, 0)),
                  _resident((1, D)),
                  _resident((D, n_cols)),
                  pl.BlockSpec((tm, LANES), lambda i: (i % seq_tiles, 0)),
                  pl.BlockSpec((tm, LANES), lambda i: (i % seq_tiles, 0))],
        out_specs=[pl.BlockSpec((tm, N_P_SLABS * LANES), lambda i: (i, 0)),
                   pl.BlockSpec((tm, N_AUX_SLABS * LANES), lambda i: (i, 0))],
        out_shape=[jax.ShapeDtypeStruct((N, N_P_SLABS * LANES), dt),
                   jax.ShapeDtypeStruct((N, N_AUX_SLABS * LANES), F32)],
        compiler_params=_params(("parallel",)),
        name="inproj",
    )(x.reshape(N, D), attn_pre_norm.reshape(1, D), w_cat, cos_t, sin_t)
    p = p.reshape(B, S, N_P_SLABS * LANES)
    aux = aux.reshape(B, S, N_AUX_SLABS * LANES)

    def pos_rows(pos):
        tiled = jnp.broadcast_to(pos.reshape(2, CMP_STRIDE, 1, d), (2, CMP_STRIDE, G, d))
        return tiled[0].reshape(1, -1), tiled[1].reshape(1, -1)

    def w1_rows(w1):
        w = jnp.broadcast_to(w1.astype(dt).reshape(2, CMP_STRIDE, 1, d, CMP_HIDDEN),
                             (2, CMP_STRIDE, G, d, CMP_HIDDEN)).reshape(2, CMP_STRIDE * G * d, CMP_HIDDEN)
        return w[0], w[1]

    def w2_halves(w2):
        return jnp.stack([jnp.pad(w2, ((0, 0), (g * d, (G - 1 - g) * d))) for g in range(G)]).astype(dt)

    kpl, kph = pos_rows(k_pos)
    vpl, vph = pos_rows(v_pos)
    kwl, kwh = w1_rows(k_w1)
    vwl, vwh = w1_rows(v_w1)
    chunk_w = CMP_STRIDE * G * d
    w1_spec = _resident((chunk_w, CMP_HIDDEN))
    w2_spec = _resident((G, CMP_HIDDEN, LANES))
    kcmp, vcmp = pl.pallas_call(
        functools.partial(_compress_kernel, nc=nc),
        grid=(B,),
        in_specs=[pl.BlockSpec((None, S, LANES), lambda b: (b, 0, A_KC)),
                  pl.BlockSpec((None, S, LANES), lambda b: (b, 0, A_VC)),
                  _resident((1, chunk_w)), _resident((1, chunk_w)), _resident((1, chunk_w)), _resident((1, chunk_w)),
                  w1_spec, w1_spec, w2_spec, w1_spec, w1_spec, w2_spec],
        out_specs=[pl.BlockSpec((None, nc, LANES), lambda b: (b, 0, 0)),
                   pl.BlockSpec((None, nc, LANES), lambda b: (b, 0, 0))],
        out_shape=[jax.ShapeDtypeStruct((B, nc, LANES), dt), jax.ShapeDtypeStruct((B, nc, LANES), dt)],
        compiler_params=_params(("parallel",)),
        name="compress",
    )(aux, aux, kpl, kph, vpl, vph, kwl, kwh, w2_halves(k_w2), vwl, vwh, w2_halves(v_w2))

    tq_d, tk_d = DIFF_TILE
    lam_vec = [v.reshape(1, d) for v in (lq1, lk1, lq2, lk2)]
    hp = DIFF_HEADS_PER_STEP
    hw = hp * LANES
    n_hp = DIFF_HEADS // hp
    o_diff = pl.pallas_call(
        functools.partial(_diff_kernel, tq=tq_d, tk=tk_d, lam_init=lam_init, heads=hp),
        grid=(B, n_hp, S // tq_d),
        in_specs=[_resident((1, d))] * 4 + [_resident((1, LANES)),
                  pl.BlockSpec((None, tq_d, hw), lambda b, h, i: (b, i, h)),
                  pl.BlockSpec((None, S, hw), lambda b, h, i: (b, 0, P_DK // hp + h)),
                  pl.BlockSpec((None, S, hw), lambda b, h, i: (b, 0, P_DV // hp + h))],
        out_specs=pl.BlockSpec((None, tq_d, hw), lambda b, h, i: (b, i, h)),
        out_shape=jax.ShapeDtypeStruct((B, S, DIFF_HEADS * LANES), dt),
        scratch_shapes=[pltpu.VMEM((hp, 2 * tq_d, LANES), F32)] * 3,
        compiler_params=_params(("parallel", "parallel", "arbitrary")),
        name="diff_attn",
    )(*lam_vec, diff_subln.reshape(1, LANES), p, p, p)

    tq_n, tk_n = NSA_TILE
    ovt = jnp.asarray(_selection_overlap_t(nc, n_cmp, n_sel)).astype(dt)
    rows = NSA_HEADS_PER_GROUP * tq_n

    def seq_slab(c):
        return pl.BlockSpec((None, S, LANES), lambda b, i: (b, 0, c))

    o_nsa = pl.pallas_call(
        functools.partial(_nsa_kernel, tq=tq_n, tk=tk_n, nc=nc, n_sel=n_sel, top_n=top_n, n_q_tiles=S // tq_n),
        grid=(B, S // tq_n),
        in_specs=[pl.BlockSpec((None, tq_n, hg * LANES), lambda b, i: (b, i, P_NQ // hg)),
                  pl.BlockSpec((None, tq_n, LANES), lambda b, i: (b, i, A_GATE)),
                  pl.BlockSpec((None, nc, LANES), lambda b, i: (b, 0, 0)),
                  pl.BlockSpec((None, nc, LANES), lambda b, i: (b, 0, 0)),
                  _resident((LANES, nc)),
                  seq_slab(P_KS), seq_slab(P_KS + 1), seq_slab(P_VS), seq_slab(P_VS + 1), seq_slab(P_KW),
                  seq_slab(P_VW), seq_slab(P_VW + 1)],
        out_specs=pl.BlockSpec((None, tq_n, 4 * LANES), lambda b, i: (b, i, 0)),
        out_shape=jax.ShapeDtypeStruct((B, S, 4 * LANES), dt),
        scratch_shapes=[pltpu.VMEM((G, rows, LANES), F32)] * 2
        + [pltpu.VMEM((G, rows, LANES), dt), pltpu.VMEM((G, rows, LANES), F32)],
        compiler_params=_params(("parallel", "arbitrary")),
        name="nsa_attn",
    )(p, aux, kcmp, vcmp, ovt, p, p, p, p, p, p, p)

    tm2 = POST_ROWS
    row = lambda i: (i, 0)
    out = pl.pallas_call(
        functools.partial(_post_kernel, d_ff=d_ff),
        grid=(N // tm2,),
        in_specs=[pl.BlockSpec((tm2, D), row),
                  pl.BlockSpec((tm2, diff_w), row), pl.BlockSpec((tm2, nsa_w), row),
                  _resident((diff_w, D)), _resident((nsa_w, D)), _resident((1, D)), _resident((1, D)),
                  _resident((D, d_ff)), _resident((D, d_ff)), _resident((d_ff, D)), _resident((1, D))],
        out_specs=pl.BlockSpec((tm2, D), row),
        out_shape=jax.ShapeDtypeStruct((N, D), F32),
        scratch_shapes=[pltpu.VMEM((tm2, d_ff), dt)],
        compiler_params=_params(("parallel",)),
        name="post",
    )(x.reshape(N, D), o_diff.reshape(N, diff_w), o_nsa.reshape(N, nsa_w), w_out_d, w_out_n,
      attn_post_norm.reshape(1, D), ffn_pre_norm.reshape(1, D),
      w_gate.astype(dt), w_up.astype(dt), w_down.astype(dt), ffn_post_norm.reshape(1, D))
    return out.reshape(B, S, D)


def kernel(x, attn_pre_norm, w_in, lambda_q1, lambda_k1, lambda_q2, lambda_k2, diff_subln, k_cmp_pos, k_cmp_w1, k_cmp_w2, v_cmp_pos, v_cmp_w1, v_cmp_w2, w_out, attn_post_norm, ffn_pre_norm, w_gate, w_up, w_down, ffn_post_norm):
    for l in range(w_in.shape[0]):
        x = _layer(x, l, attn_pre_norm[l], w_in[l], lambda_q1[l], lambda_k1[l], lambda_q2[l], lambda_k2[l],
                   diff_subln[l], k_cmp_pos[l], k_cmp_w1[l], k_cmp_w2[l], v_cmp_pos[l], v_cmp_w1[l], v_cmp_w2[l],
                   w_out[l], attn_post_norm[l], ffn_pre_norm[l], w_gate[l], w_up[l], w_down[l], ffn_post_norm[l])
    return x
```

```python
import functools
import math

import numpy as np
import jax
import jax.numpy as jnp
from jax import lax
from jax.experimental import pallas as pl
from jax.experimental.pallas import tpu as pltpu

F32 = jnp.float32
MXU_DTYPE = jnp.bfloat16

LANES = 128
SUBLANES = 8
HEAD_DIM = 64
ROPE_THETA = 10000.0
NORM_EPS = 1e-6
NEG_INF = -1e30
LOG2_E = 1.4426950408889634
FORCED_IMPORTANCE = 3e38

DIFF_HEADS = 4
NSA_HEADS_PER_GROUP = 4
NSA_KV_GROUPS = 2
CMP_BLOCK = 32
CMP_STRIDE = 16
CMP_HIDDEN = 4 * HEAD_DIM
SLC_BLOCK = 64
SLC_TOPK = 16
WINDOW = 512
N_BRANCHES = 3
SEQ_PARTS = 4

M_DQ, M_DK, M_DV, M_NQ, M_KS, M_VS, M_KW, M_VW, M_KC, M_VC, M_GATE = 0, 4, 8, 12, 16, 17, 18, 19, 20, 21, 22
N_MAIN_SLABS = 20
N_AUX_SLABS = 3
P_DQ, P_DK, P_DV, P_NQ, P_KS, P_VS, P_KW, P_VW = 0, 4, 8, 12, 16, 18, 20, 21
N_P_SLABS = 23
A_KC, A_VC, A_GATE = 0, 1, 2
VMEM_LIMIT = 56 * 1024 * 1024

MXU_WIDTH = 256
INPROJ_ROWS = 1024
POST_ROWS = 512
FF_CHUNK = MXU_WIDTH
DIFF_TILE = (512, 1024)
DIFF_HEADS_PER_STEP = 4
NSA_TILE = (256, 512)


def _nt_dot(a, b):
    return lax.dot_general(a, b, (((1,), (1,)), ((), ())), preferred_element_type=F32)


def _dot(a, b):
    return jnp.dot(a, b, preferred_element_type=F32)


def _rms(x, g):
    return x * lax.rsqrt(jnp.mean(x * x, axis=-1, keepdims=True) + NORM_EPS) * g


def _inproj_kernel(x_ref, g_ref, w_ref, cos_ref, sin_ref, p_ref, aux_ref, *, tm, seq_tiles):
    h = _rms(x_ref[...], g_ref[...]).astype(MXU_DTYPE)
    cos = cos_ref[...]
    sin = sin_ref[...]
    lane = lax.broadcasted_iota(jnp.int32, (tm, LANES), 1)
    low_half = (lane & (HEAD_DIM - 1)) < HEAD_DIM // 2

    def rope(y):
        fwd = pltpu.roll(y, HEAD_DIM // 2, 1)
        bwd = pltpu.roll(y, LANES - HEAD_DIM // 2, 1)
        return y * cos + jnp.where(low_half, bwd, fwd) * sin

    pos = (pl.program_id(0) % seq_tiles) * tm + lax.broadcasted_iota(jnp.int32, (tm, LANES), 0)
    blk = pos // SLC_BLOCK
    scale = HEAD_DIM ** -0.5 * LOG2_E

    n_slabs = N_MAIN_SLABS + N_AUX_SLABS
    per_dot = MXU_WIDTH // LANES
    for c0 in range(0, n_slabs, per_dot):
        c1 = min(c0 + per_dot, n_slabs)
        y2 = _dot(h, w_ref[:, c0 * LANES:c1 * LANES])
        for s in range(c0, c1):
            y = y2[:, (s - c0) * LANES:(s - c0 + 1) * LANES]

            def put(ref, slab, val):
                ref[:, slab * LANES:(slab + 1) * LANES] = val.astype(ref.dtype)

            if M_DQ <= s < M_DK or M_NQ <= s < M_KS:
                put(p_ref, s, rope(y) * scale)
            elif M_DK <= s < M_DV:
                put(p_ref, s, rope(y))
            elif M_DV <= s < M_NQ:
                put(p_ref, s, y)
            elif s == M_KS:
                r = rope(y)
                put(p_ref, P_KS, jnp.where(lane < HEAD_DIM, r, jnp.where(lane - HEAD_DIM == blk, 1.0, 0.0)))
                put(p_ref, P_KS + 1, jnp.where(lane >= HEAD_DIM, r, jnp.where(lane == blk, 1.0, 0.0)))
            elif s in (M_VS, M_VW):
                o = P_VS if s == M_VS else P_VW
                put(p_ref, o, jnp.where(lane < HEAD_DIM, y, 1.0))
                put(p_ref, o + 1, jnp.where(lane >= HEAD_DIM, y, 1.0))
            elif s == M_KW:
                put(p_ref, P_KW, rope(y))
            elif s == M_KC:
                put(aux_ref, A_KC, rope(y))
            elif s == M_VC:
                put(aux_ref, A_VC, y)
            else:
                put(aux_ref, A_GATE, jax.nn.sigmoid(y))


def _compress_kernel(tk_ref, tv_ref, kpl_ref, kph_ref, vpl_ref, vph_ref,
                     kwl_ref, kwh_ref, kw2_ref, vwl_ref, vwh_ref, vw2_ref, ko_ref, vo_ref, *, nc):
    lane = lax.broadcasted_iota(jnp.int32, (nc, CMP_STRIDE * LANES), 1)
    group_lanes = [(lane & HEAD_DIM) == 0, (lane & HEAD_DIM) != 0]

    def compress(t_ref, plo_ref, phi_ref, wlo_ref, whi_ref, w2_ref):
        x = jnp.concatenate([t_ref[pl.ds(l, nc, stride=CMP_STRIDE), :] for l in range(CMP_STRIDE)], axis=1)
        x_lo = x + plo_ref[...]
        x_hi = x + phi_ref[...]
        out = None
        for g in range(NSA_KV_GROUPS):
            a = _dot(jnp.where(group_lanes[g], x_lo, 0.0).astype(MXU_DTYPE), wlo_ref[...])
            b = _dot(jnp.where(group_lanes[g], x_hi, 0.0).astype(MXU_DTYPE), whi_ref[...])
            hid = a + pltpu.roll(b, nc - 1, 0)
            act = hid * jax.nn.sigmoid(hid)
            o = _dot(act.astype(MXU_DTYPE), w2_ref[g])
            out = o if out is None else out + o
        return out

    ko_ref[...] = compress(tk_ref, kpl_ref, kph_ref, kwl_ref, kwh_ref, kw2_ref).astype(ko_ref.dtype)
    vo_ref[...] = compress(tv_ref, vpl_ref, vph_ref, vwl_ref, vwh_ref, vw2_ref).astype(vo_ref.dtype)


def _flash_init(m_ref, l_ref, acc_ref):
    m_ref[...] = jnp.full(m_ref.shape, NEG_INF, F32)
    if l_ref is not None:
        l_ref[...] = jnp.zeros(l_ref.shape, F32)
    acc_ref[...] = jnp.zeros(acc_ref.shape, F32)


def _lane_tiles(x):
    return [x[:, c:c + LANES] for c in range(0, x.shape[1], LANES)]


def _stack_rows(x, n):
    return jnp.concatenate([x] * n, axis=0)


def _softmax_pv(s, v, m_ref, l_ref, acc_ref, bias=None):
    if bias is not None:
        s = s + bias
    n_tiles = s.shape[1] // LANES
    m_prev = m_ref[...]
    m_new = jnp.maximum(m_prev, jnp.max(s, axis=-1, keepdims=True))
    alpha = jnp.exp2(m_prev - m_new)
    x = s - jnp.concatenate([m_new] * n_tiles, axis=1)
    if l_ref is None:
        p = jnp.exp2(x.astype(MXU_DTYPE))
    else:
        p = jnp.exp2(x)
        l_ref[...] = alpha * l_ref[...] + functools.reduce(lambda a, b: a + b, _lane_tiles(p))
        p = p.astype(MXU_DTYPE)
    acc_ref[...] = alpha * acc_ref[...] + _dot(p, v)
    m_ref[...] = m_new


def _flash_finish(l_ref, acc_ref):
    if l_ref is None:
        return acc_ref[...]
    return acc_ref[...] / jnp.sum(l_ref[...], axis=-1, keepdims=True)


def _causal_flash(streams, s0, tq, tk, tiles_per_trip):
    assert tk % tq == 0 and tiles_per_trip in (1, 2)
    rows = streams[0][0].shape[0]
    for _, _, _, m_ref, l_ref, acc_ref in streams:
        _flash_init(m_ref, l_ref, acc_ref)

    def step(stream, k0, width, bias=None):
        q, k_ref, v_ref, m_ref, l_ref, acc_ref = stream
        k0 = pl.multiple_of(k0, tq)
        _softmax_pv(_nt_dot(q, k_ref[pl.ds(k0, width), :]), v_ref[pl.ds(k0, width), :],
                    m_ref, l_ref, acc_ref, bias)

    def tail_bias(width):
        qpos = (width - tq) + lax.broadcasted_iota(jnp.int32, (tq, width), 0)
        causal = lax.broadcasted_iota(jnp.int32, (tq, width), 1) <= qpos
        return _stack_rows(jnp.where(causal, 0.0, NEG_INF), rows // tq)

    n_full = s0 // tk
    widths = [tq * (i + 1) for i in range(tk // tq)]

    def tail(width, odd_tile=False):
        def body():
            bias = tail_bias(width)
            for stream in streams:
                if odd_tile:
                    step(stream, (n_full - 1) * tk, tk)
                step(stream, s0 + tq - width, width, bias)
        return body

    def when_width(width, extra=None):
        cond = None if tk == tq else (s0 - n_full * tk) == width - tq
        if extra is not None:
            cond = extra if cond is None else cond & extra
        return pl.when(cond) if cond is not None else (lambda f: f())

    def trip(jj, carry):
        for t in range(tiles_per_trip):
            for stream in streams:
                step(stream, (tiles_per_trip * jj + t) * tk, tk)
        return carry

    lax.fori_loop(0, n_full // tiles_per_trip, trip, 0)
    if tiles_per_trip == 1:
        for width in widths:
            when_width(width)(tail(width))
    else:
        for odd_tile in (False, True):
            parity = (n_full % 2 == 1) if odd_tile else (n_full % 2 == 0)
            for width in widths:
                when_width(width, parity)(tail(width, odd_tile))

    return [_flash_finish(l_ref, acc_ref) for _, _, _, _, l_ref, acc_ref in streams]


def _diff_kernel(lq1_ref, lk1_ref, lq2_ref, lk2_ref, subln_ref, q_ref, k_ref, v_ref, o_ref,
                 m_ref, l_ref, acc_ref, *, tq, tk, lam_init, heads):
    qi = pl.program_id(2)
    s0 = qi * tq
    lane = lax.broadcasted_iota(jnp.int32, (tq, LANES), 1)
    zero = jnp.zeros((tq, LANES), F32)
    streams = []
    for h in range(heads):
        slab = pl.ds(h * LANES, LANES)
        qf = q_ref[:, h * LANES:(h + 1) * LANES].astype(F32)
        q2 = jnp.concatenate([jnp.where(lane < HEAD_DIM, qf, zero),
                              jnp.where(lane >= HEAD_DIM, qf, zero)], axis=0).astype(MXU_DTYPE)
        streams.append((q2, k_ref.at[:, slab], v_ref.at[:, slab], m_ref.at[h], l_ref.at[h], acc_ref.at[h]))

    lam = (jnp.exp(jnp.sum(lq1_ref[...] * lk1_ref[...], axis=-1, keepdims=True))
           - jnp.exp(jnp.sum(lq2_ref[...] * lk2_ref[...], axis=-1, keepdims=True)) + lam_init)
    for h, o12 in enumerate(_causal_flash(streams, s0, tq, tk, tiles_per_trip=1)):
        o = o12[:tq] - lam * o12[tq:]
        o_ref[:, h * LANES:(h + 1) * LANES] = (_rms(o, subln_ref[...]) * (1.0 - lam_init)).astype(o_ref.dtype)


def _nsa_kernel(q_ref, gate_ref, kc_ref, vc_ref, ovt_ref, ka0_ref, ka1_ref, vs0_ref, vs1_ref,
                kw_ref, vw0_ref, vw1_ref, o_ref, m_ref, acc_ref, qsel_ref, part_ref,
                *, tq, tk, nc, n_sel, top_n, n_q_tiles):
    hg = NSA_HEADS_PER_GROUP
    rows = hg * tq
    qi = pl.program_id(1)
    s0 = qi * tq
    lane = lax.broadcasted_iota(jnp.int32, (tq, LANES), 1)
    lane_rows = lax.broadcasted_iota(jnp.int32, (rows, LANES), 1)
    qf = [q_ref[:, h * LANES:(h + 1) * LANES].astype(F32) for h in range(hg)]
    gates = gate_ref[...]
    zero = jnp.zeros((tq, LANES), F32)

    wk = WINDOW + tq
    w0 = pl.multiple_of(jnp.maximum(s0 - WINDOW, 0), tq)
    back = (s0 - w0) + lax.broadcasted_iota(jnp.int32, (tq, wk), 0) \
        - lax.broadcasted_iota(jnp.int32, (tq, wk), 1)
    win_bias = _stack_rows(jnp.where((back >= 0) & (back < WINDOW), 0.0, NEG_INF), hg)

    def normalise(raw, g):
        in_half_rows = (lane_rows >= HEAD_DIM) if g else (lane_rows < HEAD_DIM)
        return raw / jnp.where(in_half_rows, pltpu.roll(raw, HEAD_DIM, 1), 1.0)

    def gate(g, h, branch):
        c = (g * hg + h) * N_BRANCHES + branch
        return gates[:, c:c + 1]

    def phase1(n_blk, nc_eff):
        n_idx = lax.broadcasted_iota(jnp.int32, (tq, nc_eff), 1)
        qpos_c = s0 + lax.broadcasted_iota(jnp.int32, (tq, nc_eff), 0)
        cmp_ok = n_idx * CMP_STRIDE + (CMP_BLOCK - 1) <= qpos_c
        cmp_bias = _stack_rows(jnp.where(cmp_ok, 0.0, NEG_INF), hg)
        cmp_keep = _stack_rows(jnp.where(cmp_ok, 1.0, 0.0), hg)
        jb = lax.broadcasted_iota(jnp.int32, (n_blk, tq), 0)
        qp = s0 + lax.broadcasted_iota(jnp.int32, (n_blk, tq), 1)
        cur = qp // SLC_BLOCK
        valid = jb * SLC_BLOCK <= qp
        forced = (jb == 0) | (jb == cur) | (jb == cur - 1)

        for g in range(NSA_KV_GROUPS):
            in_half = (lane >= HEAD_DIM) if g else (lane < HEAD_DIM)
            q_plain = jnp.concatenate([jnp.where(in_half, qf[h], zero) for h in range(hg)],
                                      axis=0).astype(MXU_DTYPE)

            sc = _nt_dot(q_plain, kc_ref[:nc_eff, :]) + cmp_bias
            pc = jnp.exp2(sc - jnp.max(sc, axis=-1, keepdims=True)) * cmp_keep
            lc = jnp.sum(pc, axis=-1, keepdims=True)
            pc = (pc / jnp.where(lc > 0.0, lc, 1.0)).astype(MXU_DTYPE)
            o_cmp = _dot(pc, vc_ref[:nc_eff, :])

            if n_blk <= top_n:
                selected = valid
            else:
                imp_t = None
                for h in range(hg):
                    part = _nt_dot(ovt_ref[:, :nc_eff], pc[h * tq:(h + 1) * tq])
                    imp_t = part if imp_t is None else imp_t + part
                key = jnp.where(forced, FORCED_IMPORTANCE, jnp.where(valid, imp_t[:n_blk], -1.0))
                n_grp = n_blk // SUBLANES
                key_g = [key[a * SUBLANES:(a + 1) * SUBLANES] for a in range(n_grp)]
                jb_g = lax.broadcasted_iota(jnp.int32, (SUBLANES, tq), 0)
                rank_g = [jnp.zeros((SUBLANES, tq), jnp.int32) for _ in range(n_grp)]
                for i in range(n_blk):
                    ri = key[i:i + 1, :]
                    for a in range(n_grp):
                        if a < i // SUBLANES:
                            ahead = jnp.where(ri > key_g[a], 1, 0)
                        elif a > i // SUBLANES:
                            ahead = jnp.where(ri >= key_g[a], 1, 0)
                        else:
                            ahead = (jnp.where(ri > key_g[a], 1, 0)
                                     + jnp.where((ri == key_g[a]) & (jb_g > i % SUBLANES), 1, 0))
                        rank_g[a] = rank_g[a] + ahead
                selected = (jnp.concatenate(rank_g, axis=0) < top_n) & valid
            bias_t = jnp.where(selected, 0.0, NEG_INF)
            if n_blk < SLC_BLOCK:
                bias_t = jnp.concatenate([bias_t, jnp.full((SLC_BLOCK - n_blk, tq), NEG_INF, F32)], axis=0)
            pad_t = jnp.zeros((SLC_BLOCK, tq), F32)
            bias = jnp.transpose(jnp.concatenate([pad_t, bias_t] if g == 0 else [bias_t, pad_t], axis=0))
            qsel_ref[g] = jnp.concatenate([jnp.where(in_half, qf[h], bias) for h in range(hg)],
                                          axis=0).astype(qsel_ref.dtype)

            sw = _nt_dot(q_plain, kw_ref[pl.ds(w0, wk), :]) + win_bias
            pw = jnp.exp2((sw - jnp.max(sw, axis=-1, keepdims=True)).astype(MXU_DTYPE))
            o_win = normalise(_dot(pw, (vw1_ref if g else vw0_ref)[pl.ds(w0, wk), :]), g)
            part_ref[g] = jnp.concatenate(
                [gate(g, h, 0) * o_cmp[h * tq:(h + 1) * tq] + gate(g, h, 2) * o_win[h * tq:(h + 1) * tq]
                 for h in range(hg)], axis=0)

    tiles_per_part = n_q_tiles // SEQ_PARTS
    for c in range(SEQ_PARTS):
        nc_part = min(nc, -(-(nc * (c + 1) // SEQ_PARTS) // LANES) * LANES)
        pl.when(qi // tiles_per_part == c)(
            functools.partial(phase1, n_sel * (c + 1) // SEQ_PARTS, nc_part))

    outs = []
    raw = _causal_flash([(qsel_ref[0], ka0_ref, vs0_ref, m_ref.at[0], None, acc_ref.at[0]),
                         (qsel_ref[1], ka1_ref, vs1_ref, m_ref.at[1], None, acc_ref.at[1])],
                        s0, tq, tk, tiles_per_trip=2)
    for g in range(NSA_KV_GROUPS):
        o_slc = normalise(raw[g], g)
        outs.append([part_ref[g, h * tq:(h + 1) * tq, :] + gate(g, h, 1) * o_slc[h * tq:(h + 1) * tq]
                     for h in range(hg)])

    for h in range(hg):
        o_ref[:, h * LANES:(h + 1) * LANES] = jnp.where(lane < HEAD_DIM, outs[0][h], outs[1][h]).astype(o_ref.dtype)


def _post_kernel(x_ref, od_ref, on_ref, wod_ref, won_ref, gpost_ref, gpre_ref, wg_ref, wu_ref, wd_ref,
                 gffn_ref, o_ref, act_ref, *, d_ff):
    mix = _dot(od_ref[...], wod_ref[...]) + _dot(on_ref[...], won_ref[...])
    x1 = x_ref[...] + _rms(mix, gpost_ref[...])
    h = _rms(x1, gpre_ref[...]).astype(MXU_DTYPE)
    for c in range(0, d_ff, FF_CHUNK):
        gate = _dot(h, wg_ref[:, c:c + FF_CHUNK])
        up = _dot(h, wu_ref[:, c:c + FF_CHUNK])
        act_ref[:, c:c + FF_CHUNK] = (gate * jax.nn.sigmoid(gate) * up).astype(act_ref.dtype)
    f = _dot(act_ref[...], wd_ref[...])
    o_ref[...] = x1 + _rms(f, gffn_ref[...])


def _resident(shape):
    nd = len(shape)
    return pl.BlockSpec(shape, lambda *_: (0,) * nd, pipeline_mode=pl.Buffered(1))


def _params(sem):
    return pltpu.CompilerParams(dimension_semantics=sem, vmem_limit_bytes=VMEM_LIMIT)


def _rope_tables(S):
    inv = 1.0 / (ROPE_THETA ** (jnp.arange(0, HEAD_DIM, 2, dtype=F32) / HEAD_DIM))
    ang = jnp.arange(S, dtype=F32)[:, None] * inv[None, :]
    cos, sin = jnp.cos(ang), jnp.sin(ang)
    return jnp.tile(cos, (1, 4)), jnp.concatenate([-sin, sin, -sin, sin], axis=1)


def _selection_overlap_t(nc, n_cmp, n_sel):
    c0 = np.arange(n_cmp)[:, None] * CMP_STRIDE
    b0 = np.arange(n_sel)[None, :] * SLC_BLOCK
    ov = np.clip(np.minimum(c0 + CMP_BLOCK, b0 + SLC_BLOCK) - np.maximum(c0, b0), 0, None) / CMP_BLOCK
    full = np.zeros((LANES, nc), np.float32)
    full[:n_sel, :n_cmp] = ov.T
    return full


def _layer(x, layer, attn_pre_norm, w_in, lq1, lk1, lq2, lk2, diff_subln, k_pos, k_w1, k_w2,
           v_pos, v_w1, v_w2, w_out, attn_post_norm, ffn_pre_norm, w_gate, w_up, w_down, ffn_post_norm):
    B, S, D = x.shape
    N = B * S
    d_ff = w_gate.shape[1]
    nc = S // CMP_STRIDE
    n_cmp = (S - CMP_BLOCK) // CMP_STRIDE + 1
    n_sel = S // SLC_BLOCK
    top_n = min(SLC_TOPK, n_sel)
    assert n_sel <= SLC_BLOCK
    assert S % max(INPROJ_ROWS, DIFF_TILE[1], NSA_TILE[1]) == 0 and S >= WINDOW + NSA_TILE[0]
    assert (S // NSA_TILE[0]) % SEQ_PARTS == 0 and (n_sel // SEQ_PARTS) % SUBLANES == 0
    assert d_ff % FF_CHUNK == 0 and N % max(INPROJ_ROWS, POST_ROWS) == 0
    lam_init = 0.8 - 0.6 * math.exp(-0.3 * layer)
    dt = MXU_DTYPE

    hg, G, d = NSA_HEADS_PER_GROUP, NSA_KV_GROUPS, HEAD_DIM
    diff_w = DIFF_HEADS * 2 * d
    nsa_w = G * hg * d
    kv_w = G * d
    off_nq = 3 * diff_w
    off_kc = off_nq + nsa_w
    off_ks = off_kc + 2 * kv_w
    off_gate = off_kc + 6 * kv_w
    n_gates = G * hg * N_BRANCHES
    head_starts = [(g * hg + h) * d for h in range(hg) for g in range(G)]
    w_cat = jnp.concatenate(
        [w_in[:, :off_nq]] + [w_in[:, off_nq + c:off_nq + c + d] for c in head_starts]
        + [w_in[:, off_ks:off_gate],
           w_in[:, off_kc:off_ks],
           w_in[:, off_gate:off_gate + n_gates],
           jnp.zeros((D, LANES - n_gates), w_in.dtype)], axis=1).astype(dt)
    w_out_d = w_out[:diff_w].astype(dt)
    w_out_n = jnp.concatenate([w_out[diff_w + c:diff_w + c + d] for c in head_starts], axis=0).astype(dt)
    cos_t, sin_t = _rope_tables(S)

    tm = INPROJ_ROWS
    seq_tiles = S // tm
    n_cols = (N_MAIN_SLABS + N_AUX_SLABS) * LANES
    p, aux = pl.pallas_call(
        functools.partial(_inproj_kernel, tm=tm, seq_tiles=seq_tiles),
        grid=(N // tm,),
        in_specs=[pl.BlockSpec((tm, D), lambda i: (i, 0)),
                  _resident((1, D)),
                  _resident((D, n_cols)),
                  pl.BlockSpec((tm, LANES), lambda i: (i % seq_tiles, 0)),
                  pl.BlockSpec((tm, LANES), lambda i: (i % seq_tiles, 0))],
        out_specs=[pl.BlockSpec((tm, N_P_SLABS * LANES), lambda i: (i, 0)),
                   pl.BlockSpec((tm, N_AUX_SLABS * LANES), lambda i: (i, 0))],
        out_shape=[jax.ShapeDtypeStruct((N, N_P_SLABS * LANES), dt),
                   jax.ShapeDtypeStruct((N, N_AUX_SLABS * LANES), F32)],
        compiler_params=_params(("parallel",)),
        name="inproj",
    )(x.reshape(N, D), attn_pre_norm.reshape(1, D), w_cat, cos_t, sin_t)
    p = p.reshape(B, S, N_P_SLABS * LANES)
    aux = aux.reshape(B, S, N_AUX_SLABS * LANES)

    def pos_rows(pos):
        tiled = jnp.broadcast_to(pos.reshape(2, CMP_STRIDE, 1, d), (2, CMP_STRIDE, G, d))
        return tiled[0].reshape(1, -1), tiled[1].reshape(1, -1)

    def w1_rows(w1):
        w = jnp.broadcast_to(w1.astype(dt).reshape(2, CMP_STRIDE, 1, d, CMP_HIDDEN),
                             (2, CMP_STRIDE, G, d, CMP_HIDDEN)).reshape(2, CMP_STRIDE * G * d, CMP_HIDDEN)
        return w[0], w[1]

    def w2_halves(w2):
        return jnp.stack([jnp.pad(w2, ((0, 0), (g * d, (G - 1 - g) * d))) for g in range(G)]).astype(dt)

    kpl, kph = pos_rows(k_pos)
    vpl, vph = pos_rows(v_pos)
    kwl, kwh = w1_rows(k_w1)
    vwl, vwh = w1_rows(v_w1)
    chunk_w = CMP_STRIDE * G * d
    w1_spec = _resident((chunk_w, CMP_HIDDEN))
    w2_spec = _resident((G, CMP_HIDDEN, LANES))
    kcmp, vcmp = pl.pallas_call(
        functools.partial(_compress_kernel, nc=nc),
        grid=(B,),
        in_specs=[pl.BlockSpec((None, S, LANES), lambda b: (b, 0, A_KC)),
                  pl.BlockSpec((None, S, LANES), lambda b: (b, 0, A_VC)),
                  _resident((1, chunk_w)), _resident((1, chunk_w)), _resident((1, chunk_w)), _resident((1, chunk_w)),
                  w1_spec, w1_spec, w2_spec, w1_spec, w1_spec, w2_spec],
        out_specs=[pl.BlockSpec((None, nc, LANES), lambda b: (b, 0, 0)),
                   pl.BlockSpec((None, nc, LANES), lambda b: (b, 0, 0))],
        out_shape=[jax.ShapeDtypeStruct((B, nc, LANES), dt), jax.ShapeDtypeStruct((B, nc, LANES), dt)],
        compiler_params=_params(("parallel",)),
        name="compress",
    )(aux, aux, kpl, kph, vpl, vph, kwl, kwh, w2_halves(k_w2), vwl, vwh, w2_halves(v_w2))

    tq_d, tk_d = DIFF_TILE
    lam_vec = [v.reshape(1, d) for v in (lq1, lk1, lq2, lk2)]
    hp = DIFF_HEADS_PER_STEP
    hw = hp * LANES
    n_hp = DIFF_HEADS // hp
    o_diff = pl.pallas_call(
        functools.partial(_diff_kernel, tq=tq_d, tk=tk_d, lam_init=lam_init, heads=hp),
        grid=(B, n_hp, S // tq_d),
        in_specs=[_resident((1, d))] * 4 + [_resident((1, LANES)),
                  pl.BlockSpec((None, tq_d, hw), lambda b, h, i: (b, i, h)),
                  pl.BlockSpec((None, S, hw), lambda b, h, i: (b, 0, P_DK // hp + h)),
                  pl.BlockSpec((None, S, hw), lambda b, h, i: (b, 0, P_DV // hp + h))],
        out_specs=pl.BlockSpec((None, tq_d, hw), lambda b, h, i: (b, i, h)),
        out_shape=jax.ShapeDtypeStruct((B, S, DIFF_HEADS * LANES), dt),
        scratch_shapes=[pltpu.VMEM((hp, 2 * tq_d, LANES), F32)] * 3,
        compiler_params=_params(("parallel", "parallel", "arbitrary")),
        name="diff_attn",
    )(*lam_vec, diff_subln.reshape(1, LANES), p, p, p)

    tq_n, tk_n = NSA_TILE
    ovt = jnp.asarray(_selection_overlap_t(nc, n_cmp, n_sel)).astype(dt)
    rows = NSA_HEADS_PER_GROUP * tq_n

    def seq_slab(c):
        return pl.BlockSpec((None, S, LANES), lambda b, i: (b, 0, c))

    o_nsa = pl.pallas_call(
        functools.partial(_nsa_kernel, tq=tq_n, tk=tk_n, nc=nc, n_sel=n_sel, top_n=top_n, n_q_tiles=S // tq_n),
        grid=(B, S // tq_n),
        in_specs=[pl.BlockSpec((None, tq_n, hg * LANES), lambda b, i: (b, i, P_NQ // hg)),
                  pl.BlockSpec((None, tq_n, LANES), lambda b, i: (b, i, A_GATE)),
                  pl.BlockSpec((None, nc, LANES), lambda b, i: (b, 0, 0)),
                  pl.BlockSpec((None, nc, LANES), lambda b, i: (b, 0, 0)),
                  _resident((LANES, nc)),
                  seq_slab(P_KS), seq_slab(P_KS + 1), seq_slab(P_VS), seq_slab(P_VS + 1), seq_slab(P_KW),
                  seq_slab(P_VW), seq_slab(P_VW + 1)],
        out_specs=pl.BlockSpec((None, tq_n, 4 * LANES), lambda b, i: (b, i, 0)),
        out_shape=jax.ShapeDtypeStruct((B, S, 4 * LANES), dt),
        scratch_shapes=[pltpu.VMEM((G, rows, LANES), F32)] * 2
        + [pltpu.VMEM((G, rows, LANES), dt), pltpu.VMEM((G, rows, LANES), F32)],
        compiler_params=_params(("parallel", "arbitrary")),
        name="nsa_attn",
    )(p, aux, kcmp, vcmp, ovt, p, p, p, p, p, p, p)

    tm2 = POST_ROWS
    row = lambda i: (i, 0)
    out = pl.pallas_call(
        functools.partial(_post_kernel, d_ff=d_ff),
        grid=(N // tm2,),
        in_specs=[pl.BlockSpec((tm2, D), row),
                  pl.BlockSpec((tm2, diff_w), row), pl.BlockSpec((tm2, nsa_w), row),
                  _resident((diff_w, D)), _resident((nsa_w, D)), _resident((1, D)), _resident((1, D)),
                  _resident((D, d_ff)), _resident((D, d_ff)), _resident((d_ff, D)), _resident((1, D))],
        out_specs=pl.BlockSpec((tm2, D), row),
        out_shape=jax.ShapeDtypeStruct((N, D), F32),
        scratch_shapes=[pltpu.VMEM((tm2, d_ff), dt)],
        compiler_params=_params(("parallel",)),
        name="post",
    )(x.reshape(N, D), o_diff.reshape(N, diff_w), o_nsa.reshape(N, nsa_w), w_out_d, w_out_n,
      attn_post_norm.reshape(1, D), ffn_pre_norm.reshape(1, D),
      w_gate.astype(dt), w_up.astype(dt), w_down.astype(dt), ffn_post_norm.reshape(1, D))
    return out.reshape(B, S, D)


def kernel(x, attn_pre_norm, w_in, lambda_q1, lambda_k1, lambda_q2, lambda_k2, diff_subln, k_cmp_pos, k_cmp_w1, k_cmp_w2, v_cmp_pos, v_cmp_w1, v_cmp_w2, w_out, attn_post_norm, ffn_pre_norm, w_gate, w_up, w_down, ffn_post_norm):
    for l in range(w_in.shape[0]):
        x = _layer(x, l, attn_pre_norm[l], w_in[l], lambda_q1[l], lambda_k1[l], lambda_q2[l], lambda_k2[l],
                   diff_subln[l], k_cmp_pos[l], k_cmp_w1[l], k_cmp_w2[l], v_cmp_pos[l], v_cmp_w1[l], v_cmp_w2[l],
                   w_out[l], attn_post_norm[l], ffn_pre_norm[l], w_gate[l], w_up[l], w_down[l], ffn_post_norm[l])
    return x
```

```python
import functools
import math

import numpy as np
import jax
import jax.numpy as jnp
from jax import lax
from jax.experimental import pallas as pl
from jax.experimental.pallas import tpu as pltpu

F32 = jnp.float32
MXU_DTYPE = jnp.bfloat16

LANES = 128
SUBLANES = 8
HEAD_DIM = 64
ROPE_THETA = 10000.0
NORM_EPS = 1e-6
NEG_INF = -1e30
LOG2_E = 1.4426950408889634
FORCED_IMPORTANCE = 3e38

DIFF_HEADS = 4
NSA_HEADS_PER_GROUP = 4
NSA_KV_GROUPS = 2
CMP_BLOCK = 32
CMP_STRIDE = 16
CMP_HIDDEN = 4 * HEAD_DIM
SLC_BLOCK = 64
SLC_TOPK = 16
WINDOW = 512
N_BRANCHES = 3
SEQ_PARTS = 4

M_DQ, M_DK, M_DV, M_NQ, M_KS, M_VS, M_KW, M_VW, M_KC, M_VC, M_GATE = 0, 4, 8, 12, 16, 17, 18, 19, 20, 21, 22
N_MAIN_SLABS = 20
N_AUX_SLABS = 3
P_DQ, P_DK, P_DV, P_NQ, P_KS, P_VS, P_KW, P_VW = 0, 4, 8, 12, 16, 18, 20, 21
N_P_SLABS = 23
A_KC, A_VC, A_GATE = 0, 1, 2
VMEM_LIMIT = 56 * 1024 * 1024

MXU_WIDTH = 256
INPROJ_ROWS = 1024
POST_ROWS = 1024
FF_CHUNK = MXU_WIDTH
DIFF_TILE = (512, 1024)
DIFF_HEADS_PER_STEP = 4
NSA_TILE = (256, 512)


def _nt_dot(a, b):
    return lax.dot_general(a, b, (((1,), (1,)), ((), ())), preferred_element_type=F32)


def _dot(a, b):
    return jnp.dot(a, b, preferred_element_type=F32)


def _rms(x, g):
    return x * lax.rsqrt(jnp.mean(x * x, axis=-1, keepdims=True) + NORM_EPS) * g


def _inproj_kernel(x_ref, g_ref, w_ref, cos_ref, sin_ref, p_ref, aux_ref, *, tm, seq_tiles):
    h = _rms(x_ref[...], g_ref[...]).astype(MXU_DTYPE)
    cos = cos_ref[...]
    sin = sin_ref[...]
    lane = lax.broadcasted_iota(jnp.int32, (tm, LANES), 1)
    low_half = (lane & (HEAD_DIM - 1)) < HEAD_DIM // 2

    def rope(y):
        fwd = pltpu.roll(y, HEAD_DIM // 2, 1)
        bwd = pltpu.roll(y, LANES - HEAD_DIM // 2, 1)
        return y * cos + jnp.where(low_half, bwd, fwd) * sin

    pos = (pl.program_id(0) % seq_tiles) * tm + lax.broadcasted_iota(jnp.int32, (tm, LANES), 0)
    blk = pos // SLC_BLOCK
    scale = HEAD_DIM ** -0.5 * LOG2_E

    n_slabs = N_MAIN_SLABS + N_AUX_SLABS
    per_dot = MXU_WIDTH // LANES
    for c0 in range(0, n_slabs, per_dot):
        c1 = min(c0 + per_dot, n_slabs)
        y2 = _dot(h, w_ref[:, c0 * LANES:c1 * LANES])
        for s in range(c0, c1):
            y = y2[:, (s - c0) * LANES:(s - c0 + 1) * LANES]

            def put(ref, slab, val):
                ref[:, slab * LANES:(slab + 1) * LANES] = val.astype(ref.dtype)

            if M_DQ <= s < M_DK or M_NQ <= s < M_KS:
                put(p_ref, s, rope(y) * scale)
            elif M_DK <= s < M_DV:
                put(p_ref, s, rope(y))
            elif M_DV <= s < M_NQ:
                put(p_ref, s, y)
            elif s == M_KS:
                r = rope(y)
                put(p_ref, P_KS, jnp.where(lane < HEAD_DIM, r, jnp.where(lane - HEAD_DIM == blk, 1.0, 0.0)))
                put(p_ref, P_KS + 1, jnp.where(lane >= HEAD_DIM, r, jnp.where(lane == blk, 1.0, 0.0)))
            elif s in (M_VS, M_VW):
                o = P_VS if s == M_VS else P_VW
                put(p_ref, o, jnp.where(lane < HEAD_DIM, y, 1.0))
                put(p_ref, o + 1, jnp.where(lane >= HEAD_DIM, y, 1.0))
            elif s == M_KW:
                put(p_ref, P_KW, rope(y))
            elif s == M_KC:
                put(aux_ref, A_KC, rope(y))
            elif s == M_VC:
                put(aux_ref, A_VC, y)
            else:
                put(aux_ref, A_GATE, jax.nn.sigmoid(y))


def _compress_kernel(tk_ref, tv_ref, kpl_ref, kph_ref, vpl_ref, vph_ref,
                     kwl_ref, kwh_ref, kw2_ref, vwl_ref, vwh_ref, vw2_ref, ko_ref, vo_ref, *, nc):
    lane = lax.broadcasted_iota(jnp.int32, (nc, CMP_STRIDE * LANES), 1)
    group_lanes = [(lane & HEAD_DIM) == 0, (lane & HEAD_DIM) != 0]

    def compress(t_ref, plo_ref, phi_ref, wlo_ref, whi_ref, w2_ref):
        x = jnp.concatenate([t_ref[pl.ds(l, nc, stride=CMP_STRIDE), :] for l in range(CMP_STRIDE)], axis=1)
        x_lo = x + plo_ref[...]
        x_hi = x + phi_ref[...]
        out = None
        for g in range(NSA_KV_GROUPS):
            a = _dot(jnp.where(group_lanes[g], x_lo, 0.0).astype(MXU_DTYPE), wlo_ref[...])
            b = _dot(jnp.where(group_lanes[g], x_hi, 0.0).astype(MXU_DTYPE), whi_ref[...])
            hid = a + pltpu.roll(b, nc - 1, 0)
            act = hid * jax.nn.sigmoid(hid)
            o = _dot(act.astype(MXU_DTYPE), w2_ref[g])
            out = o if out is None else out + o
        return out

    ko_ref[...] = compress(tk_ref, kpl_ref, kph_ref, kwl_ref, kwh_ref, kw2_ref).astype(ko_ref.dtype)
    vo_ref[...] = compress(tv_ref, vpl_ref, vph_ref, vwl_ref, vwh_ref, vw2_ref).astype(vo_ref.dtype)


def _flash_init(m_ref, l_ref, acc_ref):
    m_ref[...] = jnp.full(m_ref.shape, NEG_INF, F32)
    if l_ref is not None:
        l_ref[...] = jnp.zeros(l_ref.shape, F32)
    acc_ref[...] = jnp.zeros(acc_ref.shape, F32)


def _lane_tiles(x):
    return [x[:, c:c + LANES] for c in range(0, x.shape[1], LANES)]


def _stack_rows(x, n):
    return jnp.concatenate([x] * n, axis=0)


def _softmax_pv(s, v, m_ref, l_ref, acc_ref, bias=None):
    if bias is not None:
        s = s + bias
    n_tiles = s.shape[1] // LANES
    m_prev = m_ref[...]
    m_new = jnp.maximum(m_prev, jnp.max(s, axis=-1, keepdims=True))
    alpha = jnp.exp2(m_prev - m_new)
    x = s - jnp.concatenate([m_new] * n_tiles, axis=1)
    if l_ref is None:
        p = jnp.exp2(x.astype(MXU_DTYPE))
    else:
        p = jnp.exp2(x)
        l_ref[...] = alpha * l_ref[...] + functools.reduce(lambda a, b: a + b, _lane_tiles(p))
        p = p.astype(MXU_DTYPE)
    acc_ref[...] = alpha * acc_ref[...] + _dot(p, v)
    m_ref[...] = m_new


def _flash_finish(l_ref, acc_ref):
    if l_ref is None:
        return acc_ref[...]
    return acc_ref[...] / jnp.sum(l_ref[...], axis=-1, keepdims=True)


def _causal_flash(streams, s0, tq, tk, tiles_per_trip):
    assert tk % tq == 0 and tiles_per_trip in (1, 2)
    rows = streams[0][0].shape[0]
    for _, _, _, m_ref, l_ref, acc_ref in streams:
        _flash_init(m_ref, l_ref, acc_ref)

    def step(stream, k0, width, bias=None):
        q, k_ref, v_ref, m_ref, l_ref, acc_ref = stream
        k0 = pl.multiple_of(k0, tq)
        _softmax_pv(_nt_dot(q, k_ref[pl.ds(k0, width), :]), v_ref[pl.ds(k0, width), :],
                    m_ref, l_ref, acc_ref, bias)

    def tail_bias(width):
        qpos = (width - tq) + lax.broadcasted_iota(jnp.int32, (tq, width), 0)
        causal = lax.broadcasted_iota(jnp.int32, (tq, width), 1) <= qpos
        return _stack_rows(jnp.where(causal, 0.0, NEG_INF), rows // tq)

    n_full = s0 // tk
    widths = [tq * (i + 1) for i in range(tk // tq)]

    def tail(width, odd_tile=False):
        def body():
            bias = tail_bias(width)
            for stream in streams:
                if odd_tile:
                    step(stream, (n_full - 1) * tk, tk)
                step(stream, s0 + tq - width, width, bias)
        return body

    def when_width(width, extra=None):
        cond = None if tk == tq else (s0 - n_full * tk) == width - tq
        if extra is not None:
            cond = extra if cond is None else cond & extra
        return pl.when(cond) if cond is not None else (lambda f: f())

    def trip(jj, carry):
        for t in range(tiles_per_trip):
            for stream in streams:
                step(stream, (tiles_per_trip * jj + t) * tk, tk)
        return carry

    lax.fori_loop(0, n_full // tiles_per_trip, trip, 0)
    if tiles_per_trip == 1:
        for width in widths:
            when_width(width)(tail(width))
    else:
        for odd_tile in (False, True):
            parity = (n_full % 2 == 1) if odd_tile else (n_full % 2 == 0)
            for width in widths:
                when_width(width, parity)(tail(width, odd_tile))

    return [_flash_finish(l_ref, acc_ref) for _, _, _, _, l_ref, acc_ref in streams]


def _diff_kernel(lq1_ref, lk1_ref, lq2_ref, lk2_ref, subln_ref, q_ref, k_ref, v_ref, o_ref,
                 m_ref, l_ref, acc_ref, *, tq, tk, lam_init, heads):
    qi = pl.program_id(2)
    s0 = qi * tq
    lane = lax.broadcasted_iota(jnp.int32, (tq, LANES), 1)
    zero = jnp.zeros((tq, LANES), F32)
    streams = []
    for h in range(heads):
        slab = pl.ds(h * LANES, LANES)
        qf = q_ref[:, h * LANES:(h + 1) * LANES].astype(F32)
        q2 = jnp.concatenate([jnp.where(lane < HEAD_DIM, qf, zero),
                              jnp.where(lane >= HEAD_DIM, qf, zero)], axis=0).astype(MXU_DTYPE)
        streams.append((q2, k_ref.at[:, slab], v_ref.at[:, slab], m_ref.at[h], l_ref.at[h], acc_ref.at[h]))

    lam = (jnp.exp(jnp.sum(lq1_ref[...] * lk1_ref[...], axis=-1, keepdims=True))
           - jnp.exp(jnp.sum(lq2_ref[...] * lk2_ref[...], axis=-1, keepdims=True)) + lam_init)
    for h, o12 in enumerate(_causal_flash(streams, s0, tq, tk, tiles_per_trip=1)):
        o = o12[:tq] - lam * o12[tq:]
        o_ref[:, h * LANES:(h + 1) * LANES] = (_rms(o, subln_ref[...]) * (1.0 - lam_init)).astype(o_ref.dtype)


def _nsa_kernel(q_ref, gate_ref, kc_ref, vc_ref, ovt_ref, ka0_ref, ka1_ref, vs0_ref, vs1_ref,
                kw_ref, vw0_ref, vw1_ref, o_ref, m_ref, acc_ref, qsel_ref, part_ref,
                *, tq, tk, nc, n_sel, top_n, n_q_tiles):
    hg = NSA_HEADS_PER_GROUP
    rows = hg * tq
    qi = pl.program_id(1)
    s0 = qi * tq
    lane = lax.broadcasted_iota(jnp.int32, (tq, LANES), 1)
    lane_rows = lax.broadcasted_iota(jnp.int32, (rows, LANES), 1)
    qf = [q_ref[:, h * LANES:(h + 1) * LANES].astype(F32) for h in range(hg)]
    gates = gate_ref[...]
    zero = jnp.zeros((tq, LANES), F32)

    wk = WINDOW + tq
    w0 = pl.multiple_of(jnp.maximum(s0 - WINDOW, 0), tq)
    back = (s0 - w0) + lax.broadcasted_iota(jnp.int32, (tq, wk), 0) \
        - lax.broadcasted_iota(jnp.int32, (tq, wk), 1)
    win_bias = _stack_rows(jnp.where((back >= 0) & (back < WINDOW), 0.0, NEG_INF), hg)

    def normalise(raw, g):
        in_half_rows = (lane_rows >= HEAD_DIM) if g else (lane_rows < HEAD_DIM)
        return raw / jnp.where(in_half_rows, pltpu.roll(raw, HEAD_DIM, 1), 1.0)

    def gate(g, h, branch):
        c = (g * hg + h) * N_BRANCHES + branch
        return gates[:, c:c + 1]

    def phase1(n_blk, nc_eff):
        n_idx = lax.broadcasted_iota(jnp.int32, (tq, nc_eff), 1)
        qpos_c = s0 + lax.broadcasted_iota(jnp.int32, (tq, nc_eff), 0)
        cmp_ok = n_idx * CMP_STRIDE + (CMP_BLOCK - 1) <= qpos_c
        cmp_bias = _stack_rows(jnp.where(cmp_ok, 0.0, NEG_INF), hg)
        cmp_keep = _stack_rows(jnp.where(cmp_ok, 1.0, 0.0), hg)
        jb = lax.broadcasted_iota(jnp.int32, (n_blk, tq), 0)
        qp = s0 + lax.broadcasted_iota(jnp.int32, (n_blk, tq), 1)
        cur = qp // SLC_BLOCK
        valid = jb * SLC_BLOCK <= qp
        forced = (jb == 0) | (jb == cur) | (jb == cur - 1)

        for g in range(NSA_KV_GROUPS):
            in_half = (lane >= HEAD_DIM) if g else (lane < HEAD_DIM)
            q_plain = jnp.concatenate([jnp.where(in_half, qf[h], zero) for h in range(hg)],
                                      axis=0).astype(MXU_DTYPE)

            sc = _nt_dot(q_plain, kc_ref[:nc_eff, :]) + cmp_bias
            pc = jnp.exp2(sc - jnp.max(sc, axis=-1, keepdims=True)) * cmp_keep
            lc = jnp.sum(pc, axis=-1, keepdims=True)
            pc = (pc / jnp.where(lc > 0.0, lc, 1.0)).astype(MXU_DTYPE)
            o_cmp = _dot(pc, vc_ref[:nc_eff, :])

            if n_blk <= top_n:
                selected = valid
            else:
                imp_t = None
                for h in range(hg):
                    part = _nt_dot(ovt_ref[:, :nc_eff], pc[h * tq:(h + 1) * tq])
                    imp_t = part if imp_t is None else imp_t + part
                key = jnp.where(forced, FORCED_IMPORTANCE, jnp.where(valid, imp_t[:n_blk], -1.0))
                n_grp = n_blk // SUBLANES
                key_g = [key[a * SUBLANES:(a + 1) * SUBLANES] for a in range(n_grp)]
                jb_g = lax.broadcasted_iota(jnp.int32, (SUBLANES, tq), 0)
                rank_g = [jnp.zeros((SUBLANES, tq), jnp.int32) for _ in range(n_grp)]
                for i in range(n_blk):
                    ri = key[i:i + 1, :]
                    for a in range(n_grp):
                        if a < i // SUBLANES:
                            ahead = jnp.where(ri > key_g[a], 1, 0)
                        elif a > i // SUBLANES:
                            ahead = jnp.where(ri >= key_g[a], 1, 0)
                        else:
                            ahead = (jnp.where(ri > key_g[a], 1, 0)
                                     + jnp.where((ri == key_g[a]) & (jb_g > i % SUBLANES), 1, 0))
                        rank_g[a] = rank_g[a] + ahead
                selected = (jnp.concatenate(rank_g, axis=0) < top_n) & valid
            bias_t = jnp.where(selected, 0.0, NEG_INF)
            if n_blk < SLC_BLOCK:
                bias_t = jnp.concatenate([bias_t, jnp.full((SLC_BLOCK - n_blk, tq), NEG_INF, F32)], axis=0)
            pad_t = jnp.zeros((SLC_BLOCK, tq), F32)
            bias = jnp.transpose(jnp.concatenate([pad_t, bias_t] if g == 0 else [bias_t, pad_t], axis=0))
            qsel_ref[g] = jnp.concatenate([jnp.where(in_half, qf[h], bias) for h in range(hg)],
                                          axis=0).astype(qsel_ref.dtype)

            sw = _nt_dot(q_plain, kw_ref[pl.ds(w0, wk), :]) + win_bias
            pw = jnp.exp2((sw - jnp.max(sw, axis=-1, keepdims=True)).astype(MXU_DTYPE))
            o_win = normalise(_dot(pw, (vw1_ref if g else vw0_ref)[pl.ds(w0, wk), :]), g)
            part_ref[g] = jnp.concatenate(
                [gate(g, h, 0) * o_cmp[h * tq:(h + 1) * tq] + gate(g, h, 2) * o_win[h * tq:(h + 1) * tq]
                 for h in range(hg)], axis=0)

    tiles_per_part = n_q_tiles // SEQ_PARTS
    for c in range(SEQ_PARTS):
        nc_part = min(nc, -(-(nc * (c + 1) // SEQ_PARTS) // LANES) * LANES)
        pl.when(qi // tiles_per_part == c)(
            functools.partial(phase1, n_sel * (c + 1) // SEQ_PARTS, nc_part))

    outs = []
    raw = _causal_flash([(qsel_ref[0], ka0_ref, vs0_ref, m_ref.at[0], None, acc_ref.at[0]),
                         (qsel_ref[1], ka1_ref, vs1_ref, m_ref.at[1], None, acc_ref.at[1])],
                        s0, tq, tk, tiles_per_trip=2)
    for g in range(NSA_KV_GROUPS):
        o_slc = normalise(raw[g], g)
        outs.append([part_ref[g, h * tq:(h + 1) * tq, :] + gate(g, h, 1) * o_slc[h * tq:(h + 1) * tq]
                     for h in range(hg)])

    for h in range(hg):
        o_ref[:, h * LANES:(h + 1) * LANES] = jnp.where(lane < HEAD_DIM, outs[0][h], outs[1][h]).astype(o_ref.dtype)


def _post_kernel(x_ref, od_ref, on_ref, wod_ref, won_ref, gpost_ref, gpre_ref, wg_ref, wu_ref, wd_ref,
                 gffn_ref, o_ref, act_ref, *, d_ff):
    mix = _dot(od_ref[...], wod_ref[...]) + _dot(on_ref[...], won_ref[...])
    x1 = x_ref[...] + _rms(mix, gpost_ref[...])
    h = _rms(x1, gpre_ref[...]).astype(MXU_DTYPE)
    for c in range(0, d_ff, FF_CHUNK):
        gate = _dot(h, wg_ref[:, c:c + FF_CHUNK])
        up = _dot(h, wu_ref[:, c:c + FF_CHUNK])
        act_ref[:, c:c + FF_CHUNK] = (gate * jax.nn.sigmoid(gate) * up).astype(act_ref.dtype)
    f = _dot(act_ref[...], wd_ref[...])
    o_ref[...] = x1 + _rms(f, gffn_ref[...])


def _resident(shape):
    nd = len(shape)
    return pl.BlockSpec(shape, lambda *_: (0,) * nd, pipeline_mode=pl.Buffered(1))


def _params(sem):
    return pltpu.CompilerParams(dimension_semantics=sem, vmem_limit_bytes=VMEM_LIMIT)


def _rope_tables(S):
    inv = 1.0 / (ROPE_THETA ** (jnp.arange(0, HEAD_DIM, 2, dtype=F32) / HEAD_DIM))
    ang = jnp.arange(S, dtype=F32)[:, None] * inv[None, :]
    cos, sin = jnp.cos(ang), jnp.sin(ang)
    return jnp.tile(cos, (1, 4)), jnp.concatenate([-sin, sin, -sin, sin], axis=1)


def _selection_overlap_t(nc, n_cmp, n_sel):
    c0 = np.arange(n_cmp)[:, None] * CMP_STRIDE
    b0 = np.arange(n_sel)[None, :] * SLC_BLOCK
    ov = np.clip(np.minimum(c0 + CMP_BLOCK, b0 + SLC_BLOCK) - np.maximum(c0, b0), 0, None) / CMP_BLOCK
    full = np.zeros((LANES, nc), np.float32)
    full[:n_sel, :n_cmp] = ov.T
    return full


def _layer(x, layer, attn_pre_norm, w_in, lq1, lk1, lq2, lk2, diff_subln, k_pos, k_w1, k_w2,
           v_pos, v_w1, v_w2, w_out, attn_post_norm, ffn_pre_norm, w_gate, w_up, w_down, ffn_post_norm):
    B, S, D = x.shape
    N = B * S
    d_ff = w_gate.shape[1]
    nc = S // CMP_STRIDE
    n_cmp = (S - CMP_BLOCK) // CMP_STRIDE + 1
    n_sel = S // SLC_BLOCK
    top_n = min(SLC_TOPK, n_sel)
    assert n_sel <= SLC_BLOCK
    assert S % max(INPROJ_ROWS, DIFF_TILE[1], NSA_TILE[1]) == 0 and S >= WINDOW + NSA_TILE[0]
    assert (S // NSA_TILE[0]) % SEQ_PARTS == 0 and (n_sel // SEQ_PARTS) % SUBLANES == 0
    assert d_ff % FF_CHUNK == 0 and N % max(INPROJ_ROWS, POST_ROWS) == 0
    lam_init = 0.8 - 0.6 * math.exp(-0.3 * layer)
    dt = MXU_DTYPE

    hg, G, d = NSA_HEADS_PER_GROUP, NSA_KV_GROUPS, HEAD_DIM
    diff_w = DIFF_HEADS * 2 * d
    nsa_w = G * hg * d
    kv_w = G * d
    off_nq = 3 * diff_w
    off_kc = off_nq + nsa_w
    off_ks = off_kc + 2 * kv_w
    off_gate = off_kc + 6 * kv_w
    n_gates = G * hg * N_BRANCHES
    head_starts = [(g * hg + h) * d for h in range(hg) for g in range(G)]
    w_cat = jnp.concatenate(
        [w_in[:, :off_nq]] + [w_in[:, off_nq + c:off_nq + c + d] for c in head_starts]
        + [w_in[:, off_ks:off_gate],
           w_in[:, off_kc:off_ks],
           w_in[:, off_gate:off_gate + n_gates],
           jnp.zeros((D, LANES - n_gates), w_in.dtype)], axis=1).astype(dt)
    w_out_d = w_out[:diff_w].astype(dt)
    w_out_n = jnp.concatenate([w_out[diff_w + c:diff_w + c + d] for c in head_starts], axis=0).astype(dt)
    cos_t, sin_t = _rope_tables(S)

    tm = INPROJ_ROWS
    seq_tiles = S // tm
    n_cols = (N_MAIN_SLABS + N_AUX_SLABS) * LANES
    p, aux = pl.pallas_call(
        functools.partial(_inproj_kernel, tm=tm, seq_tiles=seq_tiles),
        grid=(N // tm,),
        in_specs=[pl.BlockSpec((tm, D), lambda i: (i, 0)),
                  _resident((1, D)),
                  _resident((D, n_cols)),
                  pl.BlockSpec((tm, LANES), lambda i: (i % seq_tiles, 0)),
                  pl.BlockSpec((tm, LANES), lambda i: (i % seq_tiles, 0))],
        out_specs=[pl.BlockSpec((tm, N_P_SLABS * LANES), lambda i: (i, 0)),
                   pl.BlockSpec((tm, N_AUX_SLABS * LANES), lambda i: (i, 0))],
        out_shape=[jax.ShapeDtypeStruct((N, N_P_SLABS * LANES), dt),
                   jax.ShapeDtypeStruct((N, N_AUX_SLABS * LANES), F32)],
        compiler_params=_params(("parallel",)),
        name="inproj",
    )(x.reshape(N, D), attn_pre_norm.reshape(1, D), w_cat, cos_t, sin_t)
    p = p.reshape(B, S, N_P_SLABS * LANES)
    aux = aux.reshape(B, S, N_AUX_SLABS * LANES)

    def pos_rows(pos):
        tiled = jnp.broadcast_to(pos.reshape(2, CMP_STRIDE, 1, d), (2, CMP_STRIDE, G, d))
        return tiled[0].reshape(1, -1), tiled[1].reshape(1, -1)

    def w1_rows(w1):
        w = jnp.broadcast_to(w1.astype(dt).reshape(2, CMP_STRIDE, 1, d, CMP_HIDDEN),
                             (2, CMP_STRIDE, G, d, CMP_HIDDEN)).reshape(2, CMP_STRIDE * G * d, CMP_HIDDEN)
        return w[0], w[1]

    def w2_halves(w2):
        return jnp.stack([jnp.pad(w2, ((0, 0), (g * d, (G - 1 - g) * d))) for g in range(G)]).astype(dt)

    kpl, kph = pos_rows(k_pos)
    vpl, vph = pos_rows(v_pos)
    kwl, kwh = w1_rows(k_w1)
    vwl, vwh = w1_rows(v_w1)
    chunk_w = CMP_STRIDE * G * d
    w1_spec = _resident((chunk_w, CMP_HIDDEN))
    w2_spec = _resident((G, CMP_HIDDEN, LANES))
    kcmp, vcmp = pl.pallas_call(
        functools.partial(_compress_kernel, nc=nc),
        grid=(B,),
        in_specs=[pl.BlockSpec((None, S, LANES), lambda b: (b, 0, A_KC)),
                  pl.BlockSpec((None, S, LANES), lambda b: (b, 0, A_VC)),
                  _resident((1, chunk_w)), _resident((1, chunk_w)), _resident((1, chunk_w)), _resident((1, chunk_w)),
                  w1_spec, w1_spec, w2_spec, w1_spec, w1_spec, w2_spec],
        out_specs=[pl.BlockSpec((None, nc, LANES), lambda b: (b, 0, 0)),
                   pl.BlockSpec((None, nc, LANES), lambda b: (b, 0, 0))],
        out_shape=[jax.ShapeDtypeStruct((B, nc, LANES), dt), jax.ShapeDtypeStruct((B, nc, LANES), dt)],
        compiler_params=_params(("parallel",)),
        name="compress",
    )(aux, aux, kpl, kph, vpl, vph, kwl, kwh, w2_halves(k_w2), vwl, vwh, w2_halves(v_w2))

    tq_d, tk_d = DIFF_TILE
    lam_vec = [v.reshape(1, d) for v in (lq1, lk1, lq2, lk2)]
    hp = DIFF_HEADS_PER_STEP
    hw = hp * LANES
    n_hp = DIFF_HEADS // hp
    o_diff = pl.pallas_call(
        functools.partial(_diff_kernel, tq=tq_d, tk=tk_d, lam_init=lam_init, heads=hp),
        grid=(B, n_hp, S // tq_d),
        in_specs=[_resident((1, d))] * 4 + [_resident((1, LANES)),
                  pl.BlockSpec((None, tq_d, hw), lambda b, h, i: (b, i, h)),
                  pl.BlockSpec((None, S, hw), lambda b, h, i: (b, 0, P_DK // hp + h)),
                  pl.BlockSpec((None, S, hw), lambda b, h, i: (b, 0, P_DV // hp + h))],
        out_specs=pl.BlockSpec((None, tq_d, hw), lambda b, h, i: (b, i, h)),
        out_shape=jax.ShapeDtypeStruct((B, S, DIFF_HEADS * LANES), dt),
        scratch_shapes=[pltpu.VMEM((hp, 2 * tq_d, LANES), F32)] * 3,
        compiler_params=_params(("parallel", "parallel", "arbitrary")),
        name="diff_attn",
    )(*lam_vec, diff_subln.reshape(1, LANES), p, p, p)

    tq_n, tk_n = NSA_TILE
    ovt = jnp.asarray(_selection_overlap_t(nc, n_cmp, n_sel)).astype(dt)
    rows = NSA_HEADS_PER_GROUP * tq_n

    def seq_slab(c):
        return pl.BlockSpec((None, S, LANES), lambda b, i: (b, 0, c))

    o_nsa = pl.pallas_call(
        functools.partial(_nsa_kernel, tq=tq_n, tk=tk_n, nc=nc, n_sel=n_sel, top_n=top_n, n_q_tiles=S // tq_n),
        grid=(B, S // tq_n),
        in_specs=[pl.BlockSpec((None, tq_n, hg * LANES), lambda b, i: (b, i, P_NQ // hg)),
                  pl.BlockSpec((None, tq_n, LANES), lambda b, i: (b, i, A_GATE)),
                  pl.BlockSpec((None, nc, LANES), lambda b, i: (b, 0, 0)),
                  pl.BlockSpec((None, nc, LANES), lambda b, i: (b, 0, 0)),
                  _resident((LANES, nc)),
                  seq_slab(P_KS), seq_slab(P_KS + 1), seq_slab(P_VS), seq_slab(P_VS + 1), seq_slab(P_KW),
                  seq_slab(P_VW), seq_slab(P_VW + 1)],
        out_specs=pl.BlockSpec((None, tq_n, 4 * LANES), lambda b, i: (b, i, 0)),
        out_shape=jax.ShapeDtypeStruct((B, S, 4 * LANES), dt),
        scratch_shapes=[pltpu.VMEM((G, rows, LANES), F32)] * 2
        + [pltpu.VMEM((G, rows, LANES), dt), pltpu.VMEM((G, rows, LANES), F32)],
        compiler_params=_params(("parallel", "arbitrary")),
        name="nsa_attn",
    )(p, aux, kcmp, vcmp, ovt, p, p, p, p, p, p, p)

    tm2 = POST_ROWS
    row = lambda i: (i, 0)
    out = pl.pallas_call(
        functools.partial(_post_kernel, d_ff=d_ff),
        grid=(N // tm2,),
        in_specs=[pl.BlockSpec((tm2, D), row),
                  pl.BlockSpec((tm2, diff_w), row), pl.BlockSpec((tm2, nsa_w), row),
                  _resident((diff_w, D)), _resident((nsa_w, D)), _resident((1, D)), _resident((1, D)),
                  _resident((D, d_ff)), _resident((D, d_ff)), _resident((d_ff, D)), _resident((1, D))],
        out_specs=pl.BlockSpec((tm2, D), row),
        out_shape=jax.ShapeDtypeStruct((N, D), F32),
        scratch_shapes=[pltpu.VMEM((tm2, d_ff), dt)],
        compiler_params=_params(("parallel",)),
        name="post",
    )(x.reshape(N, D), o_diff.reshape(N, diff_w), o_nsa.reshape(N, nsa_w), w_out_d, w_out_n,
      attn_post_norm.reshape(1, D), ffn_pre_norm.reshape(1, D),
      w_gate.astype(dt), w_up.astype(dt), w_down.astype(dt), ffn_post_norm.reshape(1, D))
    return out.reshape(B, S, D)


def kernel(x, attn_pre_norm, w_in, lambda_q1, lambda_k1, lambda_q2, lambda_k2, diff_subln, k_cmp_pos, k_cmp_w1, k_cmp_w2, v_cmp_pos, v_cmp_w1, v_cmp_w2, w_out, attn_post_norm, ffn_pre_norm, w_gate, w_up, w_down, ffn_post_norm):
    for l in range(w_in.shape[0]):
        x = _layer(x, l, attn_pre_norm[l], w_in[l], lambda_q1[l], lambda_k1[l], lambda_q2[l], lambda_k2[l],
                   diff_subln[l], k_cmp_pos[l], k_cmp_w1[l], k_cmp_w2[l], v_cmp_pos[l], v_cmp_w1[l], v_cmp_w2[l],
                   w_out[l], attn_post_norm[l], ffn_pre_norm[l], w_gate[l], w_up[l], w_down[l], ffn_post_norm[l])
    return x
```

```python
import functools
import math

import numpy as np
import jax
import jax.numpy as jnp
from jax import lax
from jax.experimental import pallas as pl
from jax.experimental.pallas import tpu as pltpu

F32 = jnp.float32
MXU_DTYPE = jnp.bfloat16

LANES = 128
SUBLANES = 8
HEAD_DIM = 64
ROPE_THETA = 10000.0
NORM_EPS = 1e-6
NEG_INF = -1e30
LOG2_E = 1.4426950408889634
FORCED_IMPORTANCE = 3e38

DIFF_HEADS = 4
NSA_HEADS_PER_GROUP = 4
NSA_KV_GROUPS = 2
CMP_BLOCK = 32
CMP_STRIDE = 16
CMP_HIDDEN = 4 * HEAD_DIM
SLC_BLOCK = 64
SLC_TOPK = 16
WINDOW = 512
N_BRANCHES = 3
SEQ_PARTS = 4

M_DQ, M_DK, M_DV, M_NQ, M_KS, M_VS, M_KW, M_VW, M_KC, M_VC, M_GATE = 0, 4, 8, 12, 16, 17, 18, 19, 20, 21, 22
N_MAIN_SLABS = 20
N_AUX_SLABS = 3
P_DQ, P_DK, P_DV, P_NQ, P_KS, P_VS, P_KW, P_VW = 0, 4, 8, 12, 16, 18, 20, 21
N_P_SLABS = 23
A_KC, A_VC, A_GATE = 0, 1, 2
VMEM_LIMIT = 56 * 1024 * 1024

MXU_WIDTH = 256
INPROJ_ROWS = 1024
POST_ROWS = 1024
FF_CHUNK = MXU_WIDTH
DIFF_TILE = (512, 1024)
DIFF_HEADS_PER_STEP = 4
NSA_TILE = (256, 512)


def _nt_dot(a, b):
    return lax.dot_general(a, b, (((1,), (1,)), ((), ())), preferred_element_type=F32)


def _dot(a, b):
    return jnp.dot(a, b, preferred_element_type=F32)


def _rms(x, g):
    return x * lax.rsqrt(jnp.mean(x * x, axis=-1, keepdims=True) + NORM_EPS) * g


def _inproj_kernel(x_ref, g_ref, w_ref, cos_ref, sin_ref, p_ref, aux_ref, *, tm, seq_tiles):
    h = _rms(x_ref[...], g_ref[...]).astype(MXU_DTYPE)
    cos = cos_ref[...]
    sin = sin_ref[...]
    lane = lax.broadcasted_iota(jnp.int32, (tm, LANES), 1)
    low_half = (lane & (HEAD_DIM - 1)) < HEAD_DIM // 2

    def rope(y):
        fwd = pltpu.roll(y, HEAD_DIM // 2, 1)
        bwd = pltpu.roll(y, LANES - HEAD_DIM // 2, 1)
        return y * cos + jnp.where(low_half, bwd, fwd) * sin

    pos = (pl.program_id(0) % seq_tiles) * tm + lax.broadcasted_iota(jnp.int32, (tm, LANES), 0)
    blk = pos // SLC_BLOCK
    scale = HEAD_DIM ** -0.5 * LOG2_E

    n_slabs = N_MAIN_SLABS + N_AUX_SLABS
    per_dot = MXU_WIDTH // LANES
    for c0 in range(0, n_slabs, per_dot):
        c1 = min(c0 + per_dot, n_slabs)
        y2 = _dot(h, w_ref[:, c0 * LANES:c1 * LANES])
        for s in range(c0, c1):
            y = y2[:, (s - c0) * LANES:(s - c0 + 1) * LANES]

            def put(ref, slab, val):
                ref[:, slab * LANES:(slab + 1) * LANES] = val.astype(ref.dtype)

            if M_DQ <= s < M_DK or M_NQ <= s < M_KS:
                put(p_ref, s, rope(y) * scale)
            elif M_DK <= s < M_DV:
                put(p_ref, s, rope(y))
            elif M_DV <= s < M_NQ:
                put(p_ref, s, y)
            elif s == M_KS:
                r = rope(y)
                put(p_ref, P_KS, jnp.where(lane < HEAD_DIM, r, jnp.where(lane - HEAD_DIM == blk, 1.0, 0.0)))
                put(p_ref, P_KS + 1, jnp.where(lane >= HEAD_DIM, r, jnp.where(lane == blk, 1.0, 0.0)))
            elif s in (M_VS, M_VW):
                o = P_VS if s == M_VS else P_VW
                put(p_ref, o, jnp.where(lane < HEAD_DIM, y, 1.0))
                put(p_ref, o + 1, jnp.where(lane >= HEAD_DIM, y, 1.0))
            elif s == M_KW:
                put(p_ref, P_KW, rope(y))
            elif s == M_KC:
                put(aux_ref, A_KC, rope(y))
            elif s == M_VC:
                put(aux_ref, A_VC, y)
            else:
                put(aux_ref, A_GATE, jax.nn.sigmoid(y))


def _compress_kernel(tk_ref, tv_ref, kpl_ref, kph_ref, vpl_ref, vph_ref,
                     kwl_ref, kwh_ref, kw2_ref, vwl_ref, vwh_ref, vw2_ref, ko_ref, vo_ref, *, nc):
    lane = lax.broadcasted_iota(jnp.int32, (nc, CMP_STRIDE * LANES), 1)
    group_lanes = [(lane & HEAD_DIM) == 0, (lane & HEAD_DIM) != 0]

    def compress(t_ref, plo_ref, phi_ref, wlo_ref, whi_ref, w2_ref):
        x = jnp.concatenate([t_ref[pl.ds(l, nc, stride=CMP_STRIDE), :] for l in range(CMP_STRIDE)], axis=1)
        x_lo = x + plo_ref[...]
        x_hi = x + phi_ref[...]
        out = None
        for g in range(NSA_KV_GROUPS):
            a = _dot(jnp.where(group_lanes[g], x_lo, 0.0).astype(MXU_DTYPE), wlo_ref[...])
            b = _dot(jnp.where(group_lanes[g], x_hi, 0.0).astype(MXU_DTYPE), whi_ref[...])
            hid = a + pltpu.roll(b, nc - 1, 0)
            act = hid * jax.nn.sigmoid(hid)
            o = _dot(act.astype(MXU_DTYPE), w2_ref[g])
            out = o if out is None else out + o
        return out

    ko_ref[...] = compress(tk_ref, kpl_ref, kph_ref, kwl_ref, kwh_ref, kw2_ref).astype(ko_ref.dtype)
    vo_ref[...] = compress(tv_ref, vpl_ref, vph_ref, vwl_ref, vwh_ref, vw2_ref).astype(vo_ref.dtype)


def _flash_init(m_ref, l_ref, acc_ref):
    m_ref[...] = jnp.full(m_ref.shape, NEG_INF, F32)
    if l_ref is not None:
        l_ref[...] = jnp.zeros(l_ref.shape, F32)
    acc_ref[...] = jnp.zeros(acc_ref.shape, F32)


def _lane_tiles(x):
    return [x[:, c:c + LANES] for c in range(0, x.shape[1], LANES)]


def _stack_rows(x, n):
    return jnp.concatenate([x] * n, axis=0)


def _softmax_pv(s, v, m_ref, l_ref, acc_ref, bias=None):
    if bias is not None:
        s = s + bias
    n_tiles = s.shape[1] // LANES
    m_prev = m_ref[...]
    m_new = jnp.maximum(m_prev, jnp.max(s, axis=-1, keepdims=True))
    alpha = jnp.exp2(m_prev - m_new)
    x = s - jnp.concatenate([m_new] * n_tiles, axis=1)
    if l_ref is None:
        p = jnp.exp2(x.astype(MXU_DTYPE))
    else:
        p = jnp.exp2(x)
        l_ref[...] = alpha * l_ref[...] + functools.reduce(lambda a, b: a + b, _lane_tiles(p))
        p = p.astype(MXU_DTYPE)
    acc_ref[...] = alpha * acc_ref[...] + _dot(p, v)
    m_ref[...] = m_new


def _flash_finish(l_ref, acc_ref):
    if l_ref is None:
        return acc_ref[...]
    return acc_ref[...] / jnp.sum(l_ref[...], axis=-1, keepdims=True)


def _causal_flash(streams, s0, tq, tk, tiles_per_trip):
    assert tk % tq == 0 and tiles_per_trip in (1, 2)
    rows = streams[0][0].shape[0]
    for _, _, _, m_ref, l_ref, acc_ref in streams:
        _flash_init(m_ref, l_ref, acc_ref)

    def step(stream, k0, width, bias=None):
        q, k_ref, v_ref, m_ref, l_ref, acc_ref = stream
        k0 = pl.multiple_of(k0, tq)
        _softmax_pv(_nt_dot(q, k_ref[pl.ds(k0, width), :]), v_ref[pl.ds(k0, width), :],
                    m_ref, l_ref, acc_ref, bias)

    def tail_bias(width):
        qpos = (width - tq) + lax.broadcasted_iota(jnp.int32, (tq, width), 0)
        causal = lax.broadcasted_iota(jnp.int32, (tq, width), 1) <= qpos
        return _stack_rows(jnp.where(causal, 0.0, NEG_INF), rows // tq)

    n_full = s0 // tk
    widths = [tq * (i + 1) for i in range(tk // tq)]

    def tail(width, odd_tile=False):
        def body():
            bias = tail_bias(width)
            for stream in streams:
                if odd_tile:
                    step(stream, (n_full - 1) * tk, tk)
                step(stream, s0 + tq - width, width, bias)
        return body

    def when_width(width, extra=None):
        cond = None if tk == tq else (s0 - n_full * tk) == width - tq
        if extra is not None:
            cond = extra if cond is None else cond & extra
        return pl.when(cond) if cond is not None else (lambda f: f())

    def trip(jj, carry):
        for t in range(tiles_per_trip):
            for stream in streams:
                step(stream, (tiles_per_trip * jj + t) * tk, tk)
        return carry

    lax.fori_loop(0, n_full // tiles_per_trip, trip, 0)
    if tiles_per_trip == 1:
        for width in widths:
            when_width(width)(tail(width))
    else:
        for odd_tile in (False, True):
            parity = (n_full % 2 == 1) if odd_tile else (n_full % 2 == 0)
            for width in widths:
                when_width(width, parity)(tail(width, odd_tile))

    return [_flash_finish(l_ref, acc_ref) for _, _, _, _, l_ref, acc_ref in streams]


def _diff_kernel(lq1_ref, lk1_ref, lq2_ref, lk2_ref, subln_ref, q_ref, k_ref, v_ref, o_ref,
                 m_ref, l_ref, acc_ref, *, tq, tk, lam_init, heads):
    qi = pl.program_id(2)
    s0 = qi * tq
    lane = lax.broadcasted_iota(jnp.int32, (tq, LANES), 1)
    zero = jnp.zeros((tq, LANES), F32)
    streams = []
    for h in range(heads):
        slab = pl.ds(h * LANES, LANES)
        qf = q_ref[:, h * LANES:(h + 1) * LANES].astype(F32)
        q2 = jnp.concatenate([jnp.where(lane < HEAD_DIM, qf, zero),
                              jnp.where(lane >= HEAD_DIM, qf, zero)], axis=0).astype(MXU_DTYPE)
        streams.append((q2, k_ref.at[:, slab], v_ref.at[:, slab], m_ref.at[h], l_ref.at[h], acc_ref.at[h]))

    lam = (jnp.exp(jnp.sum(lq1_ref[...] * lk1_ref[...], axis=-1, keepdims=True))
           - jnp.exp(jnp.sum(lq2_ref[...] * lk2_ref[...], axis=-1, keepdims=True)) + lam_init)
    for h, o12 in enumerate(_causal_flash(streams, s0, tq, tk, tiles_per_trip=1)):
        o = o12[:tq] - lam * o12[tq:]
        o_ref[:, h * LANES:(h + 1) * LANES] = (_rms(o, subln_ref[...]) * (1.0 - lam_init)).astype(o_ref.dtype)


def _nsa_kernel(q_ref, gate_ref, kc_ref, vc_ref, ovt_ref, ka0_ref, ka1_ref, vs0_ref, vs1_ref,
                kw_ref, vw0_ref, vw1_ref, o_ref, m_ref, acc_ref, qsel_ref, part_ref,
                *, tq, tk, nc, n_sel, top_n, n_q_tiles):
    hg = NSA_HEADS_PER_GROUP
    rows = hg * tq
    qi = pl.program_id(1)
    s0 = qi * tq
    lane = lax.broadcasted_iota(jnp.int32, (tq, LANES), 1)
    qf = [q_ref[:, h * LANES:(h + 1) * LANES].astype(F32) for h in range(hg)]
    gates = gate_ref[...]
    zero = jnp.zeros((tq, LANES), F32)

    wk = WINDOW + tq
    w0 = pl.multiple_of(jnp.maximum(s0 - WINDOW, 0), tq)
    back = (s0 - w0) + lax.broadcasted_iota(jnp.int32, (tq, wk), 0) \
        - lax.broadcasted_iota(jnp.int32, (tq, wk), 1)
    win_bias = _stack_rows(jnp.where((back >= 0) & (back < WINDOW), 0.0, NEG_INF), hg)

    low = lane < HEAD_DIM

    def combine(per_group, h):
        r = slice(h * tq, (h + 1) * tq)
        return jnp.where(low, per_group[0][r], per_group[1][r])

    def normalised(raws, h):
        r = slice(h * tq, (h + 1) * tq)
        den = pltpu.roll(jnp.where(low, raws[1][r], raws[0][r]), HEAD_DIM, 1)
        return jnp.where(low, raws[0][r], raws[1][r]) / den

    def gate(h, branch):
        c0 = h * N_BRANCHES + branch
        c1 = (hg + h) * N_BRANCHES + branch
        return jnp.where(low, gates[:, c0:c0 + 1], gates[:, c1:c1 + 1])

    def phase1(n_blk, nc_eff):
        n_idx = lax.broadcasted_iota(jnp.int32, (tq, nc_eff), 1)
        qpos_c = s0 + lax.broadcasted_iota(jnp.int32, (tq, nc_eff), 0)
        cmp_ok = n_idx * CMP_STRIDE + (CMP_BLOCK - 1) <= qpos_c
        cmp_bias = _stack_rows(jnp.where(cmp_ok, 0.0, NEG_INF), hg)
        cmp_keep = _stack_rows(jnp.where(cmp_ok, 1.0, 0.0), hg)
        jb = lax.broadcasted_iota(jnp.int32, (n_blk, tq), 0)
        qp = s0 + lax.broadcasted_iota(jnp.int32, (n_blk, tq), 1)
        cur = qp // SLC_BLOCK
        valid = jb * SLC_BLOCK <= qp
        forced = (jb == 0) | (jb == cur) | (jb == cur - 1)

        o_cmps, raw_wins = [], []
        for g in range(NSA_KV_GROUPS):
            in_half = (lane >= HEAD_DIM) if g else (lane < HEAD_DIM)
            q_plain = jnp.concatenate([jnp.where(in_half, qf[h], zero) for h in range(hg)],
                                      axis=0).astype(MXU_DTYPE)

            sc = _nt_dot(q_plain, kc_ref[:nc_eff, :]) + cmp_bias
            pc = jnp.exp2(sc - jnp.max(sc, axis=-1, keepdims=True)) * cmp_keep
            lc = jnp.sum(pc, axis=-1, keepdims=True)
            pc = (pc / jnp.where(lc > 0.0, lc, 1.0)).astype(MXU_DTYPE)
            o_cmp = _dot(pc, vc_ref[:nc_eff, :])

            if n_blk <= top_n:
                selected = valid
            else:
                imp_t = None
                for h in range(hg):
                    part = _nt_dot(ovt_ref[:, :nc_eff], pc[h * tq:(h + 1) * tq])
                    imp_t = part if imp_t is None else imp_t + part
                key = jnp.where(forced, FORCED_IMPORTANCE, jnp.where(valid, imp_t[:n_blk], -1.0))
                n_grp = n_blk // SUBLANES
                key_g = [key[a * SUBLANES:(a + 1) * SUBLANES] for a in range(n_grp)]
                jb_g = lax.broadcasted_iota(jnp.int32, (SUBLANES, tq), 0)
                rank_g = [jnp.zeros((SUBLANES, tq), jnp.int32) for _ in range(n_grp)]
                for i in range(n_blk):
                    ri = key[i:i + 1, :]
                    for a in range(n_grp):
                        if a < i // SUBLANES:
                            ahead = jnp.where(ri > key_g[a], 1, 0)
                        elif a > i // SUBLANES:
                            ahead = jnp.where(ri >= key_g[a], 1, 0)
                        else:
                            ahead = (jnp.where(ri > key_g[a], 1, 0)
                                     + jnp.where((ri == key_g[a]) & (jb_g > i % SUBLANES), 1, 0))
                        rank_g[a] = rank_g[a] + ahead
                selected = (jnp.concatenate(rank_g, axis=0) < top_n) & valid
            bias_t = jnp.where(selected, 0.0, NEG_INF)
            if n_blk < SLC_BLOCK:
                bias_t = jnp.concatenate([bias_t, jnp.full((SLC_BLOCK - n_blk, tq), NEG_INF, F32)], axis=0)
            pad_t = jnp.zeros((SLC_BLOCK, tq), F32)
            bias = jnp.transpose(jnp.concatenate([pad_t, bias_t] if g == 0 else [bias_t, pad_t], axis=0))
            qsel_ref[g] = jnp.concatenate([jnp.where(in_half, qf[h], bias) for h in range(hg)],
                                          axis=0).astype(qsel_ref.dtype)

            sw = _nt_dot(q_plain, kw_ref[pl.ds(w0, wk), :]) + win_bias
            pw = jnp.exp2((sw - jnp.max(sw, axis=-1, keepdims=True)).astype(MXU_DTYPE))
            raw_wins.append(_dot(pw, (vw1_ref if g else vw0_ref)[pl.ds(w0, wk), :]))
            o_cmps.append(o_cmp)

        part_ref[...] = jnp.concatenate(
            [gate(h, 0) * combine(o_cmps, h) + gate(h, 2) * normalised(raw_wins, h) for h in range(hg)], axis=0)

    tiles_per_part = n_q_tiles // SEQ_PARTS
    for c in range(SEQ_PARTS):
        nc_part = min(nc, -(-(nc * (c + 1) // SEQ_PARTS) // LANES) * LANES)
        pl.when(qi // tiles_per_part == c)(
            functools.partial(phase1, n_sel * (c + 1) // SEQ_PARTS, nc_part))

    raw = _causal_flash([(qsel_ref[0], ka0_ref, vs0_ref, m_ref.at[0], None, acc_ref.at[0]),
                         (qsel_ref[1], ka1_ref, vs1_ref, m_ref.at[1], None, acc_ref.at[1])],
                        s0, tq, tk, tiles_per_trip=2)
    for h in range(hg):
        out = part_ref[h * tq:(h + 1) * tq, :] + gate(h, 1) * normalised(raw, h)
        o_ref[:, h * LANES:(h + 1) * LANES] = out.astype(o_ref.dtype)


def _post_kernel(x_ref, od_ref, on_ref, wod_ref, won_ref, gpost_ref, gpre_ref, wg_ref, wu_ref, wd_ref,
                 gffn_ref, o_ref, act_ref, *, d_ff):
    mix = _dot(od_ref[...], wod_ref[...]) + _dot(on_ref[...], won_ref[...])
    x1 = x_ref[...] + _rms(mix, gpost_ref[...])
    h = _rms(x1, gpre_ref[...]).astype(MXU_DTYPE)
    for c in range(0, d_ff, FF_CHUNK):
        gate = _dot(h, wg_ref[:, c:c + FF_CHUNK])
        up = _dot(h, wu_ref[:, c:c + FF_CHUNK])
        act_ref[:, c:c + FF_CHUNK] = (gate * jax.nn.sigmoid(gate) * up).astype(act_ref.dtype)
    f = _dot(act_ref[...], wd_ref[...])
    o_ref[...] = x1 + _rms(f, gffn_ref[...])


def _resident(shape):
    nd = len(shape)
    return pl.BlockSpec(shape, lambda *_: (0,) * nd, pipeline_mode=pl.Buffered(1))


def _params(sem):
    return pltpu.CompilerParams(dimension_semantics=sem, vmem_limit_bytes=VMEM_LIMIT)


def _rope_tables(S):
    inv = 1.0 / (ROPE_THETA ** (jnp.arange(0, HEAD_DIM, 2, dtype=F32) / HEAD_DIM))
    ang = jnp.arange(S, dtype=F32)[:, None] * inv[None, :]
    cos, sin = jnp.cos(ang), jnp.sin(ang)
    return jnp.tile(cos, (1, 4)), jnp.concatenate([-sin, sin, -sin, sin], axis=1)


def _selection_overlap_t(nc, n_cmp, n_sel):
    c0 = np.arange(n_cmp)[:, None] * CMP_STRIDE
    b0 = np.arange(n_sel)[None, :] * SLC_BLOCK
    ov = np.clip(np.minimum(c0 + CMP_BLOCK, b0 + SLC_BLOCK) - np.maximum(c0, b0), 0, None) / CMP_BLOCK
    full = np.zeros((LANES, nc), np.float32)
    full[:n_sel, :n_cmp] = ov.T
    return full


def _layer(x, layer, attn_pre_norm, w_in, lq1, lk1, lq2, lk2, diff_subln, k_pos, k_w1, k_w2,
           v_pos, v_w1, v_w2, w_out, attn_post_norm, ffn_pre_norm, w_gate, w_up, w_down, ffn_post_norm):
    B, S, D = x.shape
    N = B * S
    d_ff = w_gate.shape[1]
    nc = S // CMP_STRIDE
    n_cmp = (S - CMP_BLOCK) // CMP_STRIDE + 1
    n_sel = S // SLC_BLOCK
    top_n = min(SLC_TOPK, n_sel)
    assert n_sel <= SLC_BLOCK
    assert S % max(INPROJ_ROWS, DIFF_TILE[1], NSA_TILE[1]) == 0 and S >= WINDOW + NSA_TILE[0]
    assert (S // NSA_TILE[0]) % SEQ_PARTS == 0 and (n_sel // SEQ_PARTS) % SUBLANES == 0
    assert d_ff % FF_CHUNK == 0 and N % max(INPROJ_ROWS, POST_ROWS) == 0
    lam_init = 0.8 - 0.6 * math.exp(-0.3 * layer)
    dt = MXU_DTYPE

    hg, G, d = NSA_HEADS_PER_GROUP, NSA_KV_GROUPS, HEAD_DIM
    diff_w = DIFF_HEADS * 2 * d
    nsa_w = G * hg * d
    kv_w = G * d
    off_nq = 3 * diff_w
    off_kc = off_nq + nsa_w
    off_ks = off_kc + 2 * kv_w
    off_gate = off_kc + 6 * kv_w
    n_gates = G * hg * N_BRANCHES
    head_starts = [(g * hg + h) * d for h in range(hg) for g in range(G)]
    w_cat = jnp.concatenate(
        [w_in[:, :off_nq]] + [w_in[:, off_nq + c:off_nq + c + d] for c in head_starts]
        + [w_in[:, off_ks:off_gate],
           w_in[:, off_kc:off_ks],
           w_in[:, off_gate:off_gate + n_gates],
           jnp.zeros((D, LANES - n_gates), w_in.dtype)], axis=1).astype(dt)
    w_out_d = w_out[:diff_w].astype(dt)
    w_out_n = jnp.concatenate([w_out[diff_w + c:diff_w + c + d] for c in head_starts], axis=0).astype(dt)
    cos_t, sin_t = _rope_tables(S)

    tm = INPROJ_ROWS
    seq_tiles = S // tm
    n_cols = (N_MAIN_SLABS + N_AUX_SLABS) * LANES
    p, aux = pl.pallas_call(
        functools.partial(_inproj_kernel, tm=tm, seq_tiles=seq_tiles),
        grid=(N // tm,),
        in_specs=[pl.BlockSpec((tm, D), lambda i: (i, 0)),
                  _resident((1, D)),
                  _resident((D, n_cols)),
                  pl.BlockSpec((tm, LANES), lambda i: (i % seq_tiles, 0)),
                  pl.BlockSpec((tm, LANES), lambda i: (i % seq_tiles, 0))],
        out_specs=[pl.BlockSpec((tm, N_P_SLABS * LANES), lambda i: (i, 0)),
                   pl.BlockSpec((tm, N_AUX_SLABS * LANES), lambda i: (i, 0))],
        out_shape=[jax.ShapeDtypeStruct((N, N_P_SLABS * LANES), dt),
                   jax.ShapeDtypeStruct((N, N_AUX_SLABS * LANES), F32)],
        compiler_params=_params(("parallel",)),
        name="inproj",
    )(x.reshape(N, D), attn_pre_norm.reshape(1, D), w_cat, cos_t, sin_t)
    p = p.reshape(B, S, N_P_SLABS * LANES)
    aux = aux.reshape(B, S, N_AUX_SLABS * LANES)

    def pos_rows(pos):
        tiled = jnp.broadcast_to(pos.reshape(2, CMP_STRIDE, 1, d), (2, CMP_STRIDE, G, d))
        return tiled[0].reshape(1, -1), tiled[1].reshape(1, -1)

    def w1_rows(w1):
        w = jnp.broadcast_to(w1.astype(dt).reshape(2, CMP_STRIDE, 1, d, CMP_HIDDEN),
                             (2, CMP_STRIDE, G, d, CMP_HIDDEN)).reshape(2, CMP_STRIDE * G * d, CMP_HIDDEN)
        return w[0], w[1]

    def w2_halves(w2):
        return jnp.stack([jnp.pad(w2, ((0, 0), (g * d, (G - 1 - g) * d))) for g in range(G)]).astype(dt)

    kpl, kph = pos_rows(k_pos)
    vpl, vph = pos_rows(v_pos)
    kwl, kwh = w1_rows(k_w1)
    vwl, vwh = w1_rows(v_w1)
    chunk_w = CMP_STRIDE * G * d
    w1_spec = _resident((chunk_w, CMP_HIDDEN))
    w2_spec = _resident((G, CMP_HIDDEN, LANES))
    kcmp, vcmp = pl.pallas_call(
        functools.partial(_compress_kernel, nc=nc),
        grid=(B,),
        in_specs=[pl.BlockSpec((None, S, LANES), lambda b: (b, 0, A_KC)),
                  pl.BlockSpec((None, S, LANES), lambda b: (b, 0, A_VC)),
                  _resident((1, chunk_w)), _resident((1, chunk_w)), _resident((1, chunk_w)), _resident((1, chunk_w)),
                  w1_spec, w1_spec, w2_spec, w1_spec, w1_spec, w2_spec],
        out_specs=[pl.BlockSpec((None, nc, LANES), lambda b: (b, 0, 0)),
                   pl.BlockSpec((None, nc, LANES), lambda b: (b, 0, 0))],
        out_shape=[jax.ShapeDtypeStruct((B, nc, LANES), dt), jax.ShapeDtypeStruct((B, nc, LANES), dt)],
        compiler_params=_params(("parallel",)),
        name="compress",
    )(aux, aux, kpl, kph, vpl, vph, kwl, kwh, w2_halves(k_w2), vwl, vwh, w2_halves(v_w2))

    tq_d, tk_d = DIFF_TILE
    lam_vec = [v.reshape(1, d) for v in (lq1, lk1, lq2, lk2)]
    hp = DIFF_HEADS_PER_STEP
    hw = hp * LANES
    n_hp = DIFF_HEADS // hp
    o_diff = pl.pallas_call(
        functools.partial(_diff_kernel, tq=tq_d, tk=tk_d, lam_init=lam_init, heads=hp),
        grid=(B, n_hp, S // tq_d),
        in_specs=[_resident((1, d))] * 4 + [_resident((1, LANES)),
                  pl.BlockSpec((None, tq_d, hw), lambda b, h, i: (b, i, h)),
                  pl.BlockSpec((None, S, hw), lambda b, h, i: (b, 0, P_DK // hp + h)),
                  pl.BlockSpec((None, S, hw), lambda b, h, i: (b, 0, P_DV // hp + h))],
        out_specs=pl.BlockSpec((None, tq_d, hw), lambda b, h, i: (b, i, h)),
        out_shape=jax.ShapeDtypeStruct((B, S, DIFF_HEADS * LANES), dt),
        scratch_shapes=[pltpu.VMEM((hp, 2 * tq_d, LANES), F32)] * 3,
        compiler_params=_params(("parallel", "parallel", "arbitrary")),
        name="diff_attn",
    )(*lam_vec, diff_subln.reshape(1, LANES), p, p, p)

    tq_n, tk_n = NSA_TILE
    ovt = jnp.asarray(_selection_overlap_t(nc, n_cmp, n_sel)).astype(dt)
    rows = NSA_HEADS_PER_GROUP * tq_n

    def seq_slab(c):
        return pl.BlockSpec((None, S, LANES), lambda b, i: (b, 0, c))

    o_nsa = pl.pallas_call(
        functools.partial(_nsa_kernel, tq=tq_n, tk=tk_n, nc=nc, n_sel=n_sel, top_n=top_n, n_q_tiles=S // tq_n),
        grid=(B, S // tq_n),
        in_specs=[pl.BlockSpec((None, tq_n, hg * LANES), lambda b, i: (b, i, P_NQ // hg)),
                  pl.BlockSpec((None, tq_n, LANES), lambda b, i: (b, i, A_GATE)),
                  pl.BlockSpec((None, nc, LANES), lambda b, i: (b, 0, 0)),
                  pl.BlockSpec((None, nc, LANES), lambda b, i: (b, 0, 0)),
                  _resident((LANES, nc)),
                  seq_slab(P_KS), seq_slab(P_KS + 1), seq_slab(P_VS), seq_slab(P_VS + 1), seq_slab(P_KW),
                  seq_slab(P_VW), seq_slab(P_VW + 1)],
        out_specs=pl.BlockSpec((None, tq_n, 4 * LANES), lambda b, i: (b, i, 0)),
        out_shape=jax.ShapeDtypeStruct((B, S, 4 * LANES), dt),
        scratch_shapes=[pltpu.VMEM((G, rows, LANES), F32)] * 2
        + [pltpu.VMEM((G, rows, LANES), dt), pltpu.VMEM((rows, LANES), F32)],
        compiler_params=_params(("parallel", "arbitrary")),
        name="nsa_attn",
    )(p, aux, kcmp, vcmp, ovt, p, p, p, p, p, p, p)

    tm2 = POST_ROWS
    row = lambda i: (i, 0)
    out = pl.pallas_call(
        functools.partial(_post_kernel, d_ff=d_ff),
        grid=(N // tm2,),
        in_specs=[pl.BlockSpec((tm2, D), row),
                  pl.BlockSpec((tm2, diff_w), row), pl.BlockSpec((tm2, nsa_w), row),
                  _resident((diff_w, D)), _resident((nsa_w, D)), _resident((1, D)), _resident((1, D)),
                  _resident((D, d_ff)), _resident((D, d_ff)), _resident((d_ff, D)), _resident((1, D))],
        out_specs=pl.BlockSpec((tm2, D), row),
        out_shape=jax.ShapeDtypeStruct((N, D), F32),
        scratch_shapes=[pltpu.VMEM((tm2, d_ff), dt)],
        compiler_params=_params(("parallel",)),
        name="post",
    )(x.reshape(N, D), o_diff.reshape(N, diff_w), o_nsa.reshape(N, nsa_w), w_out_d, w_out_n,
      attn_post_norm.reshape(1, D), ffn_pre_norm.reshape(1, D),
      w_gate.astype(dt), w_up.astype(dt), w_down.astype(dt), ffn_post_norm.reshape(1, D))
    return out.reshape(B, S, D)


def kernel(x, attn_pre_norm, w_in, lambda_q1, lambda_k1, lambda_q2, lambda_k2, diff_subln, k_cmp_pos, k_cmp_w1, k_cmp_w2, v_cmp_pos, v_cmp_w1, v_cmp_w2, w_out, attn_post_norm, ffn_pre_norm, w_gate, w_up, w_down, ffn_post_norm):
    for l in range(w_in.shape[0]):
        x = _layer(x, l, attn_pre_norm[l], w_in[l], lambda_q1[l], lambda_k1[l], lambda_q2[l], lambda_k2[l],
                   diff_subln[l], k_cmp_pos[l], k_cmp_w1[l], k_cmp_w2[l], v_cmp_pos[l], v_cmp_w1[l], v_cmp_w2[l],
                   w_out[l], attn_post_norm[l], ffn_pre_norm[l], w_gate[l], w_up[l], w_down[l], ffn_post_norm[l])
    return x
```
